```python
import jax, jax.numpy as jnp
from jax import lax
import numpy as np

D_MODEL = 1024
BATCH = 8
SEQ = 2048
DEPTH = 4

CTX_LEN = 256
GRID_W = 64
HEAD_DIM = 64
N_MOD = 9
D_FF = 2816
EPS = 1e-6
NEG_INF = -1e30
NA_HEADS = 4
NA_WIN_H_MAX = 8
NA_WIN_W = 16
NA_QCB = 16
NA_KCB = 32
NA_WIDTH = NA_HEADS * HEAD_DIM
POOL_WINDOWS = (2, 4, 8, 16)
POOL_GROUPS = 4
POOL_WIDTH = 256
POOL_GC = POOL_WIDTH // POOL_GROUPS
GQA_Q_HEADS = 8
GQA_KV_HEADS = 2
GQA_GROUP = GQA_Q_HEADS // GQA_KV_HEADS
GQA_QB = 128
GQA_Q_WIDTH = GQA_Q_HEADS * HEAD_DIM
GQA_KV_WIDTH = GQA_KV_HEADS * HEAD_DIM
ROPE_THETA = 10000.0
OFF_A_Q = 0
OFF_A_K = OFF_A_Q + NA_WIDTH
OFF_A_V = OFF_A_K + NA_WIDTH
OFF_B_U = OFF_A_V + NA_WIDTH
OFF_C_Q = OFF_B_U + POOL_WIDTH
OFF_C_K = OFF_C_Q + GQA_Q_WIDTH
OFF_C_V = OFF_C_K + GQA_KV_WIDTH
D_IN = OFF_C_V + GQA_KV_WIDTH
D_MIX = NA_WIDTH + POOL_WIDTH + GQA_Q_WIDTH

kernel_name = 'hybrid_na_pool_gqa_macaron_dit'


def _rmsnorm(x, g):
    x32 = x.astype(jnp.float32)
    y = x32 * lax.rsqrt(jnp.mean(x32 * x32, axis=-1, keepdims=True) + EPS)
    return (y * g.astype(jnp.float32)).astype(x.dtype)


def _modulate(h, shift, scale):
    return h * (1 + scale) + shift


def _swiglu(h, w_up, w_down):
    a, b = jnp.split(h @ w_up, 2, axis=-1)
    return (jax.nn.silu(a) * b) @ w_down


def _half_ffn(h, g, shift, scale, gate, w_up, w_down):
    return h + 0.5 * gate * _swiglu(_modulate(_rmsnorm(h, g), shift, scale), w_up, w_down)


def _rope_axis(x, ang):
    m = x.shape[-1] // 2
    x32 = x.astype(jnp.float32)
    x1, x2 = x32[..., :m], x32[..., m:]
    cos = jnp.cos(ang)[:, None, :]
    sin = jnp.sin(ang)[:, None, :]
    return jnp.concatenate([x1 * cos - x2 * sin, x1 * sin + x2 * cos], axis=-1)


def _rope_2d(x, ang_row, ang_col):
    half = x.shape[-1] // 2
    out = jnp.concatenate([_rope_axis(x[..., :half], ang_row), _rope_axis(x[..., half:], ang_col)], axis=-1)
    return out.astype(x.dtype)


def _gqa_attend(q, k, v):
    s = jnp.einsum('bqkgd,bskd->bkgqs', q, k, preferred_element_type=jnp.float32) * (q.shape[-1] ** -0.5)
    p = jax.nn.softmax(s, axis=-1).astype(v.dtype)
    return jnp.einsum('bkgqs,bskd->bqkgd', p, v)


def _gqa_blocks(q, k_all, v_all):
    bn, n = q.shape[0], q.shape[1]
    nqb = n // GQA_QB
    qb = q.reshape(bn, nqb, GQA_QB, GQA_KV_HEADS, GQA_GROUP, HEAD_DIM).transpose(1, 0, 2, 3, 4, 5)
    o = lax.map(lambda blk: _gqa_attend(blk, k_all, v_all), qb)
    return o.transpose(1, 0, 2, 3, 4, 5).reshape(bn, n, GQA_Q_WIDTH)


def _neighbourhood_attn(q, k, v, k_ctx, v_ctx, rpb):
    bn, n, nh, dh = q.shape
    rows = n // GRID_W
    kh = min(NA_WIN_H_MAX, rows)
    ncb = GRID_W // NA_QCB
    col = jnp.arange(GRID_W)
    win_c0 = jnp.clip(col - NA_WIN_W // 2, 0, GRID_W - NA_WIN_W)
    blk_c0 = jnp.clip(win_c0[::NA_QCB], 0, GRID_W - NA_KCB)
    key_col = blk_c0[:, None] + jnp.arange(NA_KCB)
    q_c0 = win_c0.reshape(ncb, NA_QCB)[:, :, None]
    kc = key_col[:, None, :]
    valid = (kc >= q_c0) & (kc < q_c0 + NA_WIN_W)
    dx = kc - col.reshape(ncb, NA_QCB)[:, :, None]
    dx_idx = jnp.clip(dx, -(NA_WIN_W - 1), NA_WIN_W - 1) + NA_WIN_W - 1
    scale = dh ** -0.5
    q_grid = q.reshape(bn, rows, ncb, NA_QCB, nh, dh)

    def one_row(r):
        q_r = lax.dynamic_index_in_dim(q_grid, r, axis=1, keepdims=False)
        r0 = jnp.clip(r - kh // 2, 0, rows - kh)
        key_row = r0 + jnp.arange(kh)
        tok = key_row[None, :, None] * GRID_W + key_col[:, None, :]
        k_blk = k[:, tok]
        v_blk = v[:, tok]
        s_lat = jnp.einsum('bjqhd,bjykhd->bhjqyk', q_r, k_blk, preferred_element_type=jnp.float32) * scale
        dy_idx = key_row - r + NA_WIN_H_MAX - 1
        bias = rpb[:, dy_idx[None, None, :, None], dx_idx[:, :, None, :]]
        s_lat = jnp.where(valid[:, :, None, :], s_lat + bias.astype(jnp.float32), NEG_INF)
        s_lat = s_lat.reshape(bn, nh, ncb, NA_QCB, kh * NA_KCB)
        s_ctx = jnp.einsum('bjqhd,bchd->bhjqc', q_r, k_ctx, preferred_element_type=jnp.float32) * scale
        p = jax.nn.softmax(jnp.concatenate([s_lat, s_ctx], axis=-1), axis=-1).astype(v.dtype)
        p_lat = p[..., :kh * NA_KCB].reshape(bn, nh, ncb, NA_QCB, kh, NA_KCB)
        p_ctx = p[..., kh * NA_KCB:]
        o = jnp.einsum('bhjqyk,bjykhd->bjqhd', p_lat, v_blk) + jnp.einsum('bhjqc,bchd->bjqhd', p_ctx, v_ctx)
        return o.reshape(bn, GRID_W, nh, dh)

    o = lax.map(one_row, jnp.arange(rows))
    return o.transpose(1, 0, 2, 3, 4).reshape(bn, n, nh * dh)


def _pool_mix(u, w_pool, scale):
    bn, n, _ = u.shape
    u32 = u.astype(jnp.float32)
    cs = jnp.concatenate([jnp.zeros_like(u32[:, :1]), jnp.cumsum(u32, axis=1)], axis=1)
    cs = cs.reshape(bn, n + 1, POOL_GROUPS, POOL_GC)
    t = jnp.arange(n)[:, None]
    win = jnp.array(POOL_WINDOWS, dtype=jnp.int32)[None, :]
    lo = jnp.clip(t - win // 2, 0, n)
    hi = jnp.clip(t - win // 2 + win, 0, n)
    g_idx = jnp.arange(POOL_GROUPS)[None, :]
    sums = cs[:, hi, g_idx] - cs[:, lo, g_idx]
    mean = sums / (hi - lo).astype(jnp.float32)[None, :, :, None]
    y = (mean - u32.reshape(bn, n, POOL_GROUPS, POOL_GC)).astype(u.dtype)
    y = jnp.einsum('blgc,gcd->blgd', y, w_pool)
    return y.reshape(bn, n, POOL_WIDTH) * scale


def setup_inputs(seed: int = 0) -> dict:
    key = jax.random.key(seed)
    ks = jax.random.split(key, 20)
    nrm = jax.random.normal
    f32 = jnp.float32
    return {
        'x': nrm(ks[0], (BATCH, SEQ, D_MODEL), f32),
        'c': nrm(ks[1], (BATCH, D_MODEL), f32),
        'ctx': nrm(ks[2], (BATCH, CTX_LEN, D_MODEL), f32),
        'c_ctx': nrm(ks[3], (D_MODEL,), f32),
        'w_ada': nrm(ks[4], (DEPTH, D_MODEL, N_MOD * D_MODEL), f32) * D_MODEL ** -0.5,
        'b_ada': nrm(ks[5], (DEPTH, N_MOD * D_MODEL), f32) * 0.02,
        'norm_g': 1.0 + 0.05 * nrm(ks[6], (DEPTH, 3, D_MODEL), f32),
        'ffn1_up': nrm(ks[7], (DEPTH, D_MODEL, 2 * D_FF), f32) * D_MODEL ** -0.5,
        'ffn1_down': nrm(ks[8], (DEPTH, D_FF, D_MODEL), f32) * D_FF ** -0.5,
        'ffn2_up': nrm(ks[9], (DEPTH, D_MODEL, 2 * D_FF), f32) * D_MODEL ** -0.5,
        'ffn2_down': nrm(ks[10], (DEPTH, D_FF, D_MODEL), f32) * D_FF ** -0.5,
        'w_in': nrm(ks[11], (DEPTH, D_MODEL, D_IN), f32) * D_MODEL ** -0.5,
        'w_out': nrm(ks[12], (DEPTH, D_MIX, D_MODEL), f32) * D_MIX ** -0.5,
        'na_rpb': nrm(ks[13], (DEPTH, NA_HEADS, 2 * NA_WIN_H_MAX - 1, 2 * NA_WIN_W - 1), f32) * 0.1,
        'pool_w': nrm(ks[14], (DEPTH, POOL_GROUPS, POOL_GC, POOL_GC), f32) * POOL_GC ** -0.5,
        'pool_scale': 1.0 + 0.1 * nrm(ks[15], (DEPTH, POOL_WIDTH), f32),
        'q_norm_g': 1.0 + 0.05 * nrm(ks[16], (DEPTH, HEAD_DIM), f32),
        'k_norm_g': 1.0 + 0.05 * nrm(ks[17], (DEPTH, HEAD_DIM), f32),
        'final_g': 1.0 + 0.05 * nrm(ks[18], (D_MODEL,), f32),
    }


def reference(x, c, ctx, c_ctx, w_ada, b_ada, norm_g, ffn1_up, ffn1_down, ffn2_up, ffn2_down,
              w_in, w_out, na_rpb, pool_w, pool_scale, q_norm_g, k_norm_g, final_g):
    bn, n, d = x.shape
    ncx = ctx.shape[1]
    pos = jnp.arange(n)
    inv_freq = ROPE_THETA ** (-jnp.arange(0, HEAD_DIM // 2, 2, dtype=jnp.float32) / (HEAD_DIM // 2))
    ang_row = (pos // GRID_W).astype(jnp.float32)[:, None] * inv_freq[None, :]
    ang_col = (pos % GRID_W).astype(jnp.float32)[:, None] * inv_freq[None, :]
    h_x, h_c = x, ctx
    for l in range(DEPTH):
        last = l == DEPTH - 1
        mod_x = (jax.nn.silu(c) @ w_ada[l] + b_ada[l]).reshape(bn, 1, N_MOD, d)
        mod_c = (jax.nn.silu(c_ctx) @ w_ada[l] + b_ada[l]).reshape(1, 1, N_MOD, d)
        h_x = _half_ffn(h_x, norm_g[l, 0], mod_x[:, :, 0], mod_x[:, :, 1], mod_x[:, :, 2], ffn1_up[l], ffn1_down[l])
        h_c = _half_ffn(h_c, norm_g[l, 0], mod_c[:, :, 0], mod_c[:, :, 1], mod_c[:, :, 2], ffn1_up[l], ffn1_down[l])
        a_x = _modulate(_rmsnorm(h_x, norm_g[l, 1]), mod_x[:, :, 3], mod_x[:, :, 4])
        a_c = _modulate(_rmsnorm(h_c, norm_g[l, 1]), mod_c[:, :, 3], mod_c[:, :, 4])
        p_x = a_x @ w_in[l]
        kv_a_c = a_c @ w_in[l][:, OFF_A_K:OFF_B_U]
        kv_c_c = a_c @ w_in[l][:, OFF_C_K:D_IN]
        q_a = p_x[..., OFF_A_Q:OFF_A_K].reshape(bn, n, NA_HEADS, HEAD_DIM)
        k_a = p_x[..., OFF_A_K:OFF_A_V].reshape(bn, n, NA_HEADS, HEAD_DIM)
        v_a = p_x[..., OFF_A_V:OFF_B_U].reshape(bn, n, NA_HEADS, HEAD_DIM)
        k_a_c = kv_a_c[..., :NA_WIDTH].reshape(bn, ncx, NA_HEADS, HEAD_DIM)
        v_a_c = kv_a_c[..., NA_WIDTH:].reshape(bn, ncx, NA_HEADS, HEAD_DIM)
        q_c = _rope_2d(_rmsnorm(p_x[..., OFF_C_Q:OFF_C_K].reshape(bn, n, GQA_Q_HEADS, HEAD_DIM), q_norm_g[l]), ang_row, ang_col)
        q_c = q_c.reshape(bn, n, GQA_KV_HEADS, GQA_GROUP, HEAD_DIM)
        k_c = _rope_2d(_rmsnorm(p_x[..., OFF_C_K:OFF_C_V].reshape(bn, n, GQA_KV_HEADS, HEAD_DIM), k_norm_g[l]), ang_row, ang_col)
        v_c = p_x[..., OFF_C_V:D_IN].reshape(bn, n, GQA_KV_HEADS, HEAD_DIM)
        k_c_c = _rmsnorm(kv_c_c[..., :GQA_KV_WIDTH].reshape(bn, ncx, GQA_KV_HEADS, HEAD_DIM), k_norm_g[l])
        v_c_c = kv_c_c[..., GQA_KV_WIDTH:].reshape(bn, ncx, GQA_KV_HEADS, HEAD_DIM)
        o_a = _neighbourhood_attn(q_a, k_a, v_a, k_a_c, v_a_c, na_rpb[l])
        o_b = _pool_mix(p_x[..., OFF_B_U:OFF_C_Q], pool_w[l], pool_scale[l])
        o_c = _gqa_blocks(q_c, jnp.concatenate([k_c, k_c_c], axis=1), jnp.concatenate([v_c, v_c_c], axis=1))
        h_x = h_x + mod_x[:, :, 5] * (jnp.concatenate([o_a, o_b, o_c], axis=-1) @ w_out[l])
        h_x = _half_ffn(h_x, norm_g[l, 2], mod_x[:, :, 6], mod_x[:, :, 7], mod_x[:, :, 8], ffn2_up[l], ffn2_down[l])
        if not last:
            q_a_c = (a_c @ w_in[l][:, OFF_A_Q:OFF_A_K]).reshape(bn, ncx, NA_HEADS, 1, HEAD_DIM)
            u_c = a_c @ w_in[l][:, OFF_B_U:OFF_C_Q]
            q_c_c = _rmsnorm((a_c @ w_in[l][:, OFF_C_Q:OFF_C_K]).reshape(bn, ncx, GQA_Q_HEADS, HEAD_DIM), q_norm_g[l])
            q_c_c = q_c_c.reshape(bn, ncx, GQA_KV_HEADS, GQA_GROUP, HEAD_DIM)
            oa_c = _gqa_attend(q_a_c, k_a_c, v_a_c).reshape(bn, ncx, NA_WIDTH)
            ob_c = _pool_mix(u_c, pool_w[l], pool_scale[l])
            oc_c = _gqa_attend(q_c_c, k_c_c, v_c_c).reshape(bn, ncx, GQA_Q_WIDTH)
            h_c = h_c + mod_c[:, :, 5] * (jnp.concatenate([oa_c, ob_c, oc_c], axis=-1) @ w_out[l])
            h_c = _half_ffn(h_c, norm_g[l, 2], mod_c[:, :, 6], mod_c[:, :, 7], mod_c[:, :, 8], ffn2_up[l], ffn2_down[l])
    return _rmsnorm(h_x, final_g)
```

```python
import functools

import numpy as np
import jax
import jax.numpy as jnp
from jax import lax
from jax.experimental import pallas as pl
from jax.experimental.pallas import tpu as pltpu

D_MODEL = 1024
SEQ = 2048
CTX_LEN = 256
TOK = SEQ + CTX_LEN
GRID_W = 64
GRID_H = SEQ // GRID_W
HEAD_DIM = 64
N_MOD = 9
D_FF = 2816
EPS = 1e-6
NEG_INF = -1e30
NA_HEADS = 4
NA_WIN_H = 8
NA_WIN_W = 16
NA_WIDTH = NA_HEADS * HEAD_DIM
POOL_WINDOWS = (2, 4, 8, 16)
POOL_WIDTH = 256
GQA_Q_HEADS = 8
GQA_KV_HEADS = 2
GQA_GROUP = GQA_Q_HEADS // GQA_KV_HEADS
GQA_Q_WIDTH = GQA_Q_HEADS * HEAD_DIM
GQA_KV_WIDTH = GQA_KV_HEADS * HEAD_DIM
ROPE_THETA = 10000.0
OFF_A_Q = 0
OFF_A_K = OFF_A_Q + NA_WIDTH
OFF_A_V = OFF_A_K + NA_WIDTH
OFF_B_U = OFF_A_V + NA_WIDTH
OFF_C_Q = OFF_B_U + POOL_WIDTH
OFF_C_K = OFF_C_Q + GQA_Q_WIDTH
OFF_C_V = OFF_C_K + GQA_KV_WIDTH
D_IN = OFF_C_V + GQA_KV_WIDTH
D_MIX = NA_WIDTH + POOL_WIDTH + GQA_Q_WIDTH
QK_SCALE = HEAD_DIM ** -0.5

TM = 256
N_TILES = TOK // TM
LAT_TILES = SEQ // TM
NA_QROWS = TM // GRID_W
NA_KROWS = 12
NA_KEYS = NA_KROWS * GRID_W
FF_CHUNK = 256
ADA_TN = 1152
ADA_ROWS = 16
VMEM_LIMIT = 56 * 1024 * 1024

BF16 = jnp.bfloat16
F32 = jnp.float32


def _dot(a, b):
    return jnp.dot(a, b, preferred_element_type=F32)


def _dot_nt(a, b):
    return lax.dot_general(a, b, (((1,), (1,)), ((), ())), preferred_element_type=F32)


def _rms(x, g):
    ms = jnp.mean(x * x, axis=-1, keepdims=True)
    return x * lax.rsqrt(ms + EPS) * g


def _resident(shape):
    nd = len(shape)
    return pl.BlockSpec(shape, lambda *_: (0,) * nd, pipeline_mode=pl.Buffered(1))


def _params():
    return pltpu.CompilerParams(
        dimension_semantics=("arbitrary", "arbitrary"), vmem_limit_bytes=VMEM_LIMIT)


def _ada_kernel(c_ref, w_ref, b_ref, o_ref):
    c = c_ref[...]
    s = c * jax.nn.sigmoid(c)
    o_ref[0] = jnp.dot(s, w_ref[0], preferred_element_type=F32,
                       precision=lax.Precision.HIGHEST) + b_ref[0]


def _ada_call(cc, w_ada, b_ada):
    depth = w_ada.shape[0]
    n = w_ada.shape[2]
    return pl.pallas_call(
        _ada_kernel,
        grid=(depth, n // ADA_TN),
        in_specs=[
            pl.BlockSpec((ADA_ROWS, D_MODEL), lambda l, j: (0, 0)),
            pl.BlockSpec((1, D_MODEL, ADA_TN), lambda l, j: (l, 0, j)),
            pl.BlockSpec((1, 1, ADA_TN), lambda l, j: (l, 0, j)),
        ],
        out_specs=pl.BlockSpec((1, ADA_ROWS, ADA_TN), lambda l, j: (l, 0, j)),
        out_shape=jax.ShapeDtypeStruct((depth, ADA_ROWS, n), F32),
        compiler_params=_params(),
        name="ada",
    )(cc, w_ada, b_ada.reshape(depth, 1, n))


def _ffn_kernel(*refs, mod_base, with_mix, final):
    it = iter(refs)
    h_ref = next(it)
    if with_mix:
        oa_ref, ob_ref, oc_ref = next(it), next(it), next(it)
    mod_ref, g_ref = next(it), next(it)
    if with_mix:
        wout_ref = next(it)
    wup_ref, wdn_ref = next(it), next(it)
    if final:
        fg_ref = next(it)
    o_ref, hid_ref = next(it), next(it)

    x = h_ref[0]
    mod = mod_ref[0]
    if with_mix:
        mixed = jnp.concatenate([oa_ref[0], ob_ref[0], oc_ref[0]], axis=-1)
        x = x + mod[5:6] * _dot(mixed, wout_ref[...])
    y = _rms(x, g_ref[...])
    xn = (y * (1.0 + mod[mod_base + 1:mod_base + 2]) + mod[mod_base:mod_base + 1]).astype(BF16)
    for c in range(D_FF // FF_CHUNK):
        lo = c * FF_CHUNK
        a = _dot(xn, wup_ref[:, lo:lo + FF_CHUNK])
        b = _dot(xn, wup_ref[:, D_FF + lo:D_FF + lo + FF_CHUNK])
        hid_ref[:, lo:lo + FF_CHUNK] = (a * jax.nn.sigmoid(a) * b).astype(BF16)
    out = x + (0.5 * mod[mod_base + 2:mod_base + 3]) * _dot(hid_ref[...], wdn_ref[...])
    if final:
        out = _rms(out, fg_ref[...])
    o_ref[0] = out


def _ffn_call(h, modtab, g, w_up, w_dn, *, mod_base, mix=None, final_g=None):
    bn = h.shape[0]
    with_mix = mix is not None
    final = final_g is not None
    n_tiles = LAT_TILES if final else N_TILES
    tile = lambda w: pl.BlockSpec((1, TM, w), lambda b, t: (b, t, 0))
    args, specs = [h], [tile(D_MODEL)]
    if with_mix:
        oa, ob, oc, w_out = mix
        args += [oa, ob, oc]
        specs += [tile(NA_WIDTH), tile(POOL_WIDTH), tile(GQA_Q_WIDTH)]
    args += [modtab, g.reshape(1, D_MODEL)]
    specs += [pl.BlockSpec((1, N_MOD, D_MODEL), lambda b, t: (2 * b + t // LAT_TILES, 0, 0)),
              pl.BlockSpec((1, D_MODEL), lambda b, t: (0, 0))]
    if with_mix:
        args.append(w_out)
        specs.append(_resident((D_MIX, D_MODEL)))
    args += [w_up, w_dn]
    specs += [_resident((D_MODEL, 2 * D_FF)), _resident((D_FF, D_MODEL))]
    if final:
        args.append(final_g.reshape(1, D_MODEL))
        specs.append(pl.BlockSpec((1, D_MODEL), lambda b, t: (0, 0)))
    return pl.pallas_call(
        functools.partial(_ffn_kernel, mod_base=mod_base, with_mix=with_mix, final=final),
        grid=(bn, n_tiles),
        in_specs=specs,
        out_specs=tile(D_MODEL),
        out_shape=jax.ShapeDtypeStruct((bn, n_tiles * TM, D_MODEL), F32),
        scratch_shapes=[pltpu.VMEM((TM, D_FF), BF16)],
        compiler_params=_params(),
        name="ffn_mix" if with_mix else "ffn",
    )(*args)


def _head_norm_rope(z, gain, cs, sn, ones_bd):
    width = z.shape[-1]
    zz = z * z
    hi = zz.astype(BF16)
    lo = (zz - hi.astype(F32)).astype(BF16)
    ms = _dot(hi, ones_bd) + _dot(lo, ones_bd)
    zn = z * lax.rsqrt(ms + EPS) * gain
    lane = lax.broadcasted_iota(jnp.int32, zn.shape, 1)
    partner = jnp.where((lane & 16) != 0,
                        pltpu.roll(zn, 16, axis=1), pltpu.roll(zn, width - 16, axis=1))
    reps = width // cs.shape[-1]
    if reps > 1:
        cs = jnp.concatenate([cs] * reps, axis=-1)
        sn = jnp.concatenate([sn] * reps, axis=-1)
    return zn * cs + partner * sn


def _proj_kernel(h_ref, mod_ref, g_ref, win_ref, cs_ref, sn_ref, qg_ref, kg_ref, bd_ref,
                 qa_ref, ka_ref, va_ref, u_ref, qc_ref, kc_ref, vc_ref):
    x = h_ref[0]
    mod = mod_ref[0]
    y = _rms(x, g_ref[...])
    a = (y * (1.0 + mod[4:5]) + mod[3:4]).astype(BF16)
    p = _dot(a, win_ref[...])
    qa_ref[0] = (p[:, OFF_A_Q:OFF_A_K] * QK_SCALE).astype(BF16)
    ka_ref[0] = p[:, OFF_A_K:OFF_A_V].astype(BF16)
    va_ref[0] = p[:, OFF_A_V:OFF_B_U].astype(BF16)
    u_ref[0] = p[:, OFF_B_U:OFF_C_Q]
    cs, sn = cs_ref[...], sn_ref[...]
    bd = bd_ref[...]
    qc = _head_norm_rope(p[:, OFF_C_Q:OFF_C_K], qg_ref[...], cs, sn, bd)
    qc_ref[0] = (qc * QK_SCALE).astype(BF16)
    kc = _head_norm_rope(p[:, OFF_C_K:OFF_C_V], kg_ref[...], cs, sn,
                         bd[:GQA_KV_WIDTH, :GQA_KV_WIDTH])
    kc_ref[0] = kc.astype(BF16)
    vc_ref[0] = p[:, OFF_C_V:D_IN].astype(BF16)


def _proj_call(h, modtab, g, w_in, rope_cs, rope_sn, qg, kg, ones_bd):
    bn = h.shape[0]
    tile = lambda w: pl.BlockSpec((1, TM, w), lambda b, t: (b, t, 0))
    const = lambda r, w: pl.BlockSpec((r, w), lambda b, t: (0, 0))
    out = lambda w, dt: jax.ShapeDtypeStruct((bn, TOK, w), dt)
    return pl.pallas_call(
        _proj_kernel,
        grid=(bn, N_TILES),
        in_specs=[
            tile(D_MODEL),
            pl.BlockSpec((1, N_MOD, D_MODEL), lambda b, t: (2 * b + t // LAT_TILES, 0, 0)),
            const(1, D_MODEL),
            _resident((D_MODEL, D_IN)),
            pl.BlockSpec((TM, 2 * HEAD_DIM), lambda b, t: (t, 0)),
            pl.BlockSpec((TM, 2 * HEAD_DIM), lambda b, t: (t, 0)),
            const(1, GQA_Q_WIDTH),
            const(1, GQA_KV_WIDTH),
            const(GQA_Q_WIDTH, GQA_Q_WIDTH),
        ],
        out_specs=[tile(NA_WIDTH), tile(NA_WIDTH), tile(NA_WIDTH), tile(POOL_WIDTH),
                   tile(GQA_Q_WIDTH), tile(GQA_KV_WIDTH), tile(GQA_KV_WIDTH)],
        out_shape=[out(NA_WIDTH, BF16), out(NA_WIDTH, BF16), out(NA_WIDTH, BF16),
                   out(POOL_WIDTH, F32), out(GQA_Q_WIDTH, BF16), out(GQA_KV_WIDTH, BF16),
                   out(GQA_KV_WIDTH, BF16)],
        compiler_params=_params(),
        name="proj",
    )(h, modtab, g.reshape(1, D_MODEL), w_in, rope_cs, rope_sn, qg, kg, ones_bd)


def _softmax_pv(scores, values):
    m = scores[0].max(axis=-1, keepdims=True)
    for s in scores[1:]:
        m = jnp.maximum(m, s.max(axis=-1, keepdims=True))
    den = None
    acc = None
    for s, v in zip(scores, values):
        e = jnp.exp(s - m)
        d = e.sum(axis=-1, keepdims=True)
        o = _dot(e.astype(BF16), v)
        den = d if den is None else den + d
        acc = o if acc is None else acc + o
    return acc / den


def _na_kernel(q_ref, k_ref, v_ref, bias_ref, o_ref):
    g = pl.program_id(1)
    q = q_ref[0]
    head_of_lane = lax.broadcasted_iota(jnp.int32, (1, NA_WIDTH), 1) // HEAD_DIM
    k_ctx = k_ref[0, SEQ:TOK, :]
    v_ctx = v_ref[0, SEQ:TOK, :]

    @pl.when(g < LAT_TILES)
    def _latent():
        first_row = jnp.clip(g * NA_QROWS - NA_WIN_H // 2, 0, GRID_H - NA_KROWS)
        start = pl.multiple_of(first_row * GRID_W, GRID_W)
        k_lat = k_ref[0, pl.ds(start, NA_KEYS), :]
        v_lat = v_ref[0, pl.ds(start, NA_KEYS), :]
        pattern = jnp.minimum(g, 2) + (g == LAT_TILES - 1).astype(jnp.int32)
        out = jnp.zeros((TM, NA_WIDTH), F32)
        for h in range(NA_HEADS):
            mine = head_of_lane == h
            qh = jnp.where(mine, q, jnp.zeros_like(q))
            s_lat = _dot_nt(qh, k_lat) + bias_ref[pattern, h]
            s_ctx = _dot_nt(qh, k_ctx)
            out = jnp.where(mine, _softmax_pv([s_lat, s_ctx], [v_lat, v_ctx]), out)
        o_ref[0] = out.astype(BF16)

    @pl.when(g == LAT_TILES)
    def _context():
        out = jnp.zeros((TM, NA_WIDTH), F32)
        for h in range(NA_HEADS):
            mine = head_of_lane == h
            qh = jnp.where(mine, q, jnp.zeros_like(q))
            out = jnp.where(mine, _softmax_pv([_dot_nt(qh, k_ctx)], [v_ctx]), out)
        o_ref[0] = out.astype(BF16)


def _na_call(qa, ka, va, bias):
    bn = qa.shape[0]
    return pl.pallas_call(
        _na_kernel,
        grid=(bn, N_TILES),
        in_specs=[
            pl.BlockSpec((1, TM, NA_WIDTH), lambda b, t: (b, t, 0)),
            pl.BlockSpec((1, TOK, NA_WIDTH), lambda b, t: (b, 0, 0)),
            pl.BlockSpec((1, TOK, NA_WIDTH), lambda b, t: (b, 0, 0)),
            _resident(bias.shape),
        ],
        out_specs=pl.BlockSpec((1, TM, NA_WIDTH), lambda b, t: (b, t, 0)),
        out_shape=jax.ShapeDtypeStruct((bn, TOK, NA_WIDTH), BF16),
        compiler_params=_params(),
        name="na",
    )(qa, ka, va, bias)


def _na_bias_index():
    col = np.arange(GRID_W)
    win_c0 = np.clip(col - NA_WIN_W // 2, 0, GRID_W - NA_WIN_W)
    dys, dxs, valids = [], [], []
    for grp in (0, 1, 2, LAT_TILES - 1):
        first_row = int(np.clip(grp * NA_QROWS - NA_WIN_H // 2, 0, GRID_H - NA_KROWS))
        q_row = grp * NA_QROWS + np.arange(NA_QROWS)
        q_r0 = np.clip(q_row - NA_WIN_H // 2, 0, GRID_H - NA_WIN_H)
        k_row = first_row + np.arange(NA_KROWS)
        row_ok = (k_row[None, :] >= q_r0[:, None]) & (k_row[None, :] < q_r0[:, None] + NA_WIN_H)
        dy = k_row[None, :] - q_row[:, None] + NA_WIN_H - 1
        col_ok = (col[None, :] >= win_c0[:, None]) & (col[None, :] < win_c0[:, None] + NA_WIN_W)
        dx = np.clip(col[None, :] - col[:, None], -(NA_WIN_W - 1), NA_WIN_W - 1) + NA_WIN_W - 1
        valid = row_ok[:, None, :, None] & col_ok[None, :, None, :]
        dy4 = np.broadcast_to(np.clip(dy, 0, 2 * NA_WIN_H - 2)[:, None, :, None], valid.shape)
        dx4 = np.broadcast_to(dx[None, :, None, :], valid.shape)
        dys.append(dy4.reshape(TM, NA_KEYS))
        dxs.append(dx4.reshape(TM, NA_KEYS))
        valids.append(valid.reshape(TM, NA_KEYS))
    return np.stack(dys), np.stack(dxs), np.stack(valids)


_NA_DY, _NA_DX, _NA_VALID = _na_bias_index()


def _na_bias(rpb):
    b = rpb[:, _NA_DY, _NA_DX]
    b = jnp.where(_NA_VALID[None], b, NEG_INF)
    return jnp.transpose(b, (1, 0, 2, 3))


def _shift_rows(a, d):
    n = a.shape[0]
    row = lax.broadcasted_iota(jnp.int32, a.shape, 0)
    rolled = pltpu.roll(a, d % n, axis=0)
    ok = (row >= d) if d > 0 else (row < n + d)
    return jnp.where(ok, rolled, 0.0)


def _pool_segment(u, w_bd, scale):
    n = u.shape[0]
    row = lax.broadcasted_iota(jnp.int32, u.shape, 0)
    grp = lax.broadcasted_iota(jnp.int32, u.shape, 1) // (POOL_WIDTH // len(POOL_WINDOWS))
    trail, lead = u, u
    total = jnp.zeros_like(u)
    count = jnp.ones_like(u)
    k = 1
    for i, w in enumerate(POOL_WINDOWS):
        while k < w // 2:
            trail = trail + _shift_rows(trail, k)
            lead = lead + _shift_rows(lead, -k)
            k *= 2
        win = _shift_rows(trail, 1) + lead
        cnt = (jnp.minimum(row + w // 2, n) - jnp.maximum(row - w // 2, 0)).astype(F32)
        total = jnp.where(grp == i, win, total)
        count = jnp.where(grp == i, cnt, count)
    y = (total / count - u).astype(BF16)
    return (_dot(y, w_bd) * scale).astype(BF16)


def _pool_kernel(u_ref, w_ref, s_ref, o_ref):
    w_bd = w_ref[...]
    scale = s_ref[...]
    o_ref[0, :SEQ, :] = _pool_segment(u_ref[0, :SEQ, :], w_bd, scale)
    o_ref[0, SEQ:, :] = _pool_segment(u_ref[0, SEQ:, :], w_bd, scale)


def _pool_call(u, w_bd, scale):
    bn = u.shape[0]
    return pl.pallas_call(
        _pool_kernel,
        grid=(bn,),
        in_specs=[
            pl.BlockSpec((1, TOK, POOL_WIDTH), lambda b: (b, 0, 0)),
            pl.BlockSpec((POOL_WIDTH, POOL_WIDTH), lambda b: (0, 0)),
            pl.BlockSpec((1, POOL_WIDTH), lambda b: (0, 0)),
        ],
        out_specs=pl.BlockSpec((1, TOK, POOL_WIDTH), lambda b: (b, 0, 0)),
        out_shape=jax.ShapeDtypeStruct((bn, TOK, POOL_WIDTH), BF16),
        compiler_params=pltpu.CompilerParams(
            dimension_semantics=("arbitrary",), vmem_limit_bytes=VMEM_LIMIT),
        name="pool",
    )(u, w_bd, scale.reshape(1, POOL_WIDTH))


def _gqa_tile(q, k, v):
    half_of_lane = lax.broadcasted_iota(jnp.int32, (1, GQA_KV_WIDTH), 1) // HEAD_DIM
    blocks = []
    for j in range(GQA_GROUP):
        qb = q[:, j * GQA_KV_WIDTH:(j + 1) * GQA_KV_WIDTH]
        out = jnp.zeros((q.shape[0], GQA_KV_WIDTH), F32)
        for kv in range(GQA_KV_HEADS):
            mine = half_of_lane == kv
            qh = jnp.where(mine, qb, jnp.zeros_like(qb))
            out = jnp.where(mine, _softmax_pv([_dot_nt(qh, k)], [v]), out)
        blocks.append(out)
    return jnp.concatenate(blocks, axis=-1).astype(BF16)


def _gqa_kernel(q_ref, k_ref, v_ref, o_ref):
    t = pl.program_id(1)

    @pl.when(t < LAT_TILES)
    def _latent():
        o_ref[0] = _gqa_tile(q_ref[0], k_ref[0], v_ref[0])

    @pl.when(t == LAT_TILES)
    def _context():
        o_ref[0] = _gqa_tile(q_ref[0], k_ref[0, SEQ:TOK, :], v_ref[0, SEQ:TOK, :])


def _gqa_call(qc, kc, vc):
    bn = qc.shape[0]
    return pl.pallas_call(
        _gqa_kernel,
        grid=(bn, N_TILES),
        in_specs=[
            pl.BlockSpec((1, TM, GQA_Q_WIDTH), lambda b, t: (b, t, 0)),
            pl.BlockSpec((1, TOK, GQA_KV_WIDTH), lambda b, t: (b, 0, 0)),
            pl.BlockSpec((1, TOK, GQA_KV_WIDTH), lambda b, t: (b, 0, 0)),
        ],
        out_specs=pl.BlockSpec((1, TM, GQA_Q_WIDTH), lambda b, t: (b, t, 0)),
        out_shape=jax.ShapeDtypeStruct((bn, TOK, GQA_Q_WIDTH), BF16),
        compiler_params=_params(),
        name="gqa",
    )(qc, kc, vc)


def _rope_tables():
    pos = np.arange(SEQ)
    inv_freq = ROPE_THETA ** (-np.arange(0, HEAD_DIM // 2, 2, dtype=np.float32) / (HEAD_DIM // 2))
    ang_row = (pos // GRID_W).astype(np.float32)[:, None] * inv_freq[None, :].astype(np.float32)
    ang_col = (pos % GRID_W).astype(np.float32)[:, None] * inv_freq[None, :].astype(np.float32)
    ang = np.concatenate([ang_row, ang_row, ang_col, ang_col], axis=-1).astype(np.float32)
    sign = np.tile(np.repeat(np.array([-1.0, 1.0], np.float32), HEAD_DIM // 4), 2)
    return ang, sign


_ROPE_ANG, _ROPE_SIGN = _rope_tables()

_Q_HEAD_ORDER = np.array([kv * GQA_GROUP + j for j in range(GQA_GROUP) for kv in range(GQA_KV_HEADS)])
_Q_LANE_ORDER = (_Q_HEAD_ORDER[:, None] * HEAD_DIM + np.arange(HEAD_DIM)[None, :]).reshape(-1)


def kernel(x, c, ctx, c_ctx, w_ada, b_ada, norm_g, ffn1_up, ffn1_down, ffn2_up, ffn2_down,
           w_in, w_out, na_rpb, pool_w, pool_scale, q_norm_g, k_norm_g, final_g):
    bn = x.shape[0]
    depth = w_ada.shape[0]
    assert x.shape == (bn, SEQ, D_MODEL) and ctx.shape == (bn, CTX_LEN, D_MODEL)
    assert 2 * bn <= ADA_ROWS

    cc = jnp.zeros((ADA_ROWS, D_MODEL), F32).at[:bn].set(c).at[bn].set(c_ctx)
    mods = _ada_call(cc, w_ada, b_ada)[:, :bn + 1].reshape(depth, bn + 1, N_MOD, D_MODEL)
    mod_ctx = jnp.broadcast_to(mods[:, bn:], (depth, bn, N_MOD, D_MODEL))
    modtab = jnp.stack([mods[:, :bn], mod_ctx], axis=2).reshape(depth, 2 * bn, N_MOD, D_MODEL)

    ang = jnp.asarray(_ROPE_ANG)
    cs_lat = jnp.tile(jnp.cos(ang), (1, 2))
    sn_lat = jnp.tile(jnp.sin(ang) * jnp.asarray(_ROPE_SIGN)[None, :], (1, 2))
    rope_cs = jnp.concatenate([cs_lat, jnp.ones((CTX_LEN, 2 * HEAD_DIM), F32)], axis=0)
    rope_sn = jnp.concatenate([sn_lat, jnp.zeros((CTX_LEN, 2 * HEAD_DIM), F32)], axis=0)
    ones_bd = jnp.asarray(
        np.kron(np.eye(GQA_Q_HEADS, dtype=np.float32), np.full((HEAD_DIM, HEAD_DIM), 1.0 / HEAD_DIM,
                                                               np.float32))).astype(BF16)

    q_cols = OFF_C_Q + _Q_LANE_ORDER
    w_in_p = jnp.concatenate([w_in[:, :, :OFF_C_Q], w_in[:, :, q_cols], w_in[:, :, OFF_C_K:]],
                             axis=-1).astype(BF16)
    c_rows = NA_WIDTH + POOL_WIDTH + _Q_LANE_ORDER
    w_out_p = jnp.concatenate([w_out[:, :NA_WIDTH + POOL_WIDTH], w_out[:, c_rows]], axis=1).astype(BF16)
    up1, dn1 = ffn1_up.astype(BF16), ffn1_down.astype(BF16)
    up2, dn2 = ffn2_up.astype(BF16), ffn2_down.astype(BF16)
    pool_bd = jnp.einsum('gh,lgcd->lgchd', jnp.eye(len(POOL_WINDOWS), dtype=F32), pool_w)
    pool_bd = pool_bd.reshape(depth, POOL_WIDTH, POOL_WIDTH).astype(BF16)
    qg = jnp.tile(q_norm_g, (1, GQA_Q_HEADS)).reshape(depth, 1, GQA_Q_WIDTH)
    kg = jnp.tile(k_norm_g, (1, GQA_KV_HEADS)).reshape(depth, 1, GQA_KV_WIDTH)

    h = jnp.concatenate([x, ctx], axis=1)
    for l in range(depth):
        last = l == depth - 1
        h = _ffn_call(h, modtab[l], norm_g[l, 0], up1[l], dn1[l], mod_base=0)
        qa, ka, va, u, qc, kc, vc = _proj_call(
            h, modtab[l], norm_g[l, 1], w_in_p[l], rope_cs, rope_sn, qg[l], kg[l], ones_bd)
        oa = _na_call(qa, ka, va, _na_bias(na_rpb[l]))
        ob = _pool_call(u, pool_bd[l], pool_scale[l])
        oc = _gqa_call(qc, kc, vc)
        h = _ffn_call(h, modtab[l], norm_g[l, 2], up2[l], dn2[l], mod_base=6,
                      mix=(oa, ob, oc, w_out_p[l]), final_g=final_g if last else None)
    return h
```

```python
import functools

import numpy as np
import jax
import jax.numpy as jnp
from jax import lax
from jax.experimental import pallas as pl
from jax.experimental.pallas import tpu as pltpu

D_MODEL = 1024
SEQ = 2048
CTX_LEN = 256
TOK = SEQ + CTX_LEN
GRID_W = 64
GRID_H = SEQ // GRID_W
HEAD_DIM = 64
N_MOD = 9
D_FF = 2816
EPS = 1e-6
NEG_INF = -1e30
NA_HEADS = 4
NA_WIN_H = 8
NA_WIN_W = 16
NA_WIDTH = NA_HEADS * HEAD_DIM
POOL_WINDOWS = (2, 4, 8, 16)
POOL_WIDTH = 256
GQA_Q_HEADS = 8
GQA_KV_HEADS = 2
GQA_GROUP = GQA_Q_HEADS // GQA_KV_HEADS
GQA_Q_WIDTH = GQA_Q_HEADS * HEAD_DIM
GQA_KV_WIDTH = GQA_KV_HEADS * HEAD_DIM
ROPE_THETA = 10000.0
OFF_A_Q = 0
OFF_A_K = OFF_A_Q + NA_WIDTH
OFF_A_V = OFF_A_K + NA_WIDTH
OFF_B_U = OFF_A_V + NA_WIDTH
OFF_C_Q = OFF_B_U + POOL_WIDTH
OFF_C_K = OFF_C_Q + GQA_Q_WIDTH
OFF_C_V = OFF_C_K + GQA_KV_WIDTH
D_IN = OFF_C_V + GQA_KV_WIDTH
D_MIX = NA_WIDTH + POOL_WIDTH + GQA_Q_WIDTH
QK_SCALE = HEAD_DIM ** -0.5

TM = 256
N_TILES = TOK // TM
LAT_TILES = SEQ // TM
NA_QROWS = TM // GRID_W
NA_KROWS = 12
NA_KEYS = NA_KROWS * GRID_W
FF_CHUNK = 256
ADA_TN = 1152
ADA_ROWS = 16
VMEM_LIMIT = 56 * 1024 * 1024

BF16 = jnp.bfloat16
F32 = jnp.float32


def _dot(a, b):
    return jnp.dot(a, b, preferred_element_type=F32)


def _dot_nt(a, b):
    return lax.dot_general(a, b, (((1,), (1,)), ((), ())), preferred_element_type=F32)


def _rms(x, g):
    ms = jnp.mean(x * x, axis=-1, keepdims=True)
    return x * lax.rsqrt(ms + EPS) * g


def _resident(shape):
    nd = len(shape)
    return pl.BlockSpec(shape, lambda *_: (0,) * nd, pipeline_mode=pl.Buffered(1))


def _params():
    return pltpu.CompilerParams(
        dimension_semantics=("arbitrary", "arbitrary"), vmem_limit_bytes=VMEM_LIMIT)


def _ada_kernel(c_ref, w_ref, b_ref, o_ref):
    c = c_ref[...]
    s = c * jax.nn.sigmoid(c)
    o_ref[0] = jnp.dot(s, w_ref[0], preferred_element_type=F32,
                       precision=lax.Precision.HIGHEST) + b_ref[0]


def _ada_call(cc, w_ada, b_ada):
    depth = w_ada.shape[0]
    n = w_ada.shape[2]
    return pl.pallas_call(
        _ada_kernel,
        grid=(depth, n // ADA_TN),
        in_specs=[
            pl.BlockSpec((ADA_ROWS, D_MODEL), lambda l, j: (0, 0)),
            pl.BlockSpec((1, D_MODEL, ADA_TN), lambda l, j: (l, 0, j)),
            pl.BlockSpec((1, 1, ADA_TN), lambda l, j: (l, 0, j)),
        ],
        out_specs=pl.BlockSpec((1, ADA_ROWS, ADA_TN), lambda l, j: (l, 0, j)),
        out_shape=jax.ShapeDtypeStruct((depth, ADA_ROWS, n), F32),
        compiler_params=_params(),
        name="ada",
    )(cc, w_ada, b_ada.reshape(depth, 1, n))


def _ffn_kernel(*refs, mod_base, with_mix, final):
    it = iter(refs)
    h_ref = next(it)
    if with_mix:
        oa_ref, ob_ref, oc_ref = next(it), next(it), next(it)
    mod_ref, g_ref = next(it), next(it)
    if with_mix:
        wout_ref = next(it)
    wup_ref, wdn_ref = next(it), next(it)
    if final:
        fg_ref = next(it)
    o_ref, hid_ref = next(it), next(it)

    x = h_ref[0]
    mod = mod_ref[0]
    if with_mix:
        mixed = jnp.concatenate([oa_ref[0], ob_ref[0], oc_ref[0]], axis=-1)
        x = x + mod[5:6] * _dot(mixed, wout_ref[...])
    y = _rms(x, g_ref[...])
    xn = (y * (1.0 + mod[mod_base + 1:mod_base + 2]) + mod[mod_base:mod_base + 1]).astype(BF16)
    for c in range(D_FF // FF_CHUNK):
        lo = c * FF_CHUNK
        a = _dot(xn, wup_ref[:, lo:lo + FF_CHUNK])
        b = _dot(xn, wup_ref[:, D_FF + lo:D_FF + lo + FF_CHUNK])
        hid_ref[:, lo:lo + FF_CHUNK] = (a * jax.nn.sigmoid(a) * b).astype(BF16)
    out = x + (0.5 * mod[mod_base + 2:mod_base + 3]) * _dot(hid_ref[...], wdn_ref[...])
    if final:
        out = _rms(out, fg_ref[...])
    o_ref[0] = out


def _ffn_call(h, modtab, g, w_up, w_dn, *, mod_base, mix=None, final_g=None):
    bn = h.shape[0]
    with_mix = mix is not None
    final = final_g is not None
    n_tiles = LAT_TILES if final else N_TILES
    tile = lambda w: pl.BlockSpec((1, TM, w), lambda b, t: (b, t, 0))
    args, specs = [h], [tile(D_MODEL)]
    if with_mix:
        oa, ob, oc, w_out = mix
        args += [oa, ob, oc]
        specs += [tile(NA_WIDTH), tile(POOL_WIDTH), tile(GQA_Q_WIDTH)]
    args += [modtab, g.reshape(1, D_MODEL)]
    specs += [pl.BlockSpec((1, N_MOD, D_MODEL), lambda b, t: (2 * b + t // LAT_TILES, 0, 0)),
              pl.BlockSpec((1, D_MODEL), lambda b, t: (0, 0))]
    if with_mix:
        args.append(w_out)
        specs.append(_resident((D_MIX, D_MODEL)))
    args += [w_up, w_dn]
    specs += [_resident((D_MODEL, 2 * D_FF)), _resident((D_FF, D_MODEL))]
    if final:
        args.append(final_g.reshape(1, D_MODEL))
        specs.append(pl.BlockSpec((1, D_MODEL), lambda b, t: (0, 0)))
    return pl.pallas_call(
        functools.partial(_ffn_kernel, mod_base=mod_base, with_mix=with_mix, final=final),
        grid=(bn, n_tiles),
        in_specs=specs,
        out_specs=tile(D_MODEL),
        out_shape=jax.ShapeDtypeStruct((bn, n_tiles * TM, D_MODEL), F32),
        scratch_shapes=[pltpu.VMEM((TM, D_FF), BF16)],
        compiler_params=_params(),
        name="ffn_mix" if with_mix else "ffn",
    )(*args)


def _head_norm_rope(z, gain, cs, sn, ones_bd):
    width = z.shape[-1]
    zz = z * z
    hi = zz.astype(BF16)
    lo = (zz - hi.astype(F32)).astype(BF16)
    ms = _dot(hi, ones_bd) + _dot(lo, ones_bd)
    zn = z * lax.rsqrt(ms + EPS) * gain
    lane = lax.broadcasted_iota(jnp.int32, zn.shape, 1)
    partner = jnp.where((lane & 16) != 0,
                        pltpu.roll(zn, 16, axis=1), pltpu.roll(zn, width - 16, axis=1))
    reps = width // cs.shape[-1]
    if reps > 1:
        cs = jnp.concatenate([cs] * reps, axis=-1)
        sn = jnp.concatenate([sn] * reps, axis=-1)
    return zn * cs + partner * sn


def _proj_kernel(h_ref, mod_ref, g_ref, win_ref, cs_ref, sn_ref, qg_ref, kg_ref, bd_ref,
                 qa_ref, ka_ref, va_ref, u_ref, qc_ref, kc_ref, vc_ref):
    x = h_ref[0]
    mod = mod_ref[0]
    y = _rms(x, g_ref[...])
    a = (y * (1.0 + mod[4:5]) + mod[3:4]).astype(BF16)
    p = _dot(a, win_ref[...])
    qa_ref[0] = (p[:, OFF_A_Q:OFF_A_K] * QK_SCALE).astype(BF16)
    ka_ref[0] = p[:, OFF_A_K:OFF_A_V].astype(BF16)
    va_ref[0] = p[:, OFF_A_V:OFF_B_U].astype(BF16)
    u_ref[0] = p[:, OFF_B_U:OFF_C_Q]
    cs, sn = cs_ref[...], sn_ref[...]
    bd = bd_ref[...]
    qc = _head_norm_rope(p[:, OFF_C_Q:OFF_C_K], qg_ref[...], cs, sn, bd)
    qc_ref[0] = (qc * QK_SCALE).astype(BF16)
    kc = _head_norm_rope(p[:, OFF_C_K:OFF_C_V], kg_ref[...], cs, sn,
                         bd[:GQA_KV_WIDTH, :GQA_KV_WIDTH])
    kc_ref[0] = kc.astype(BF16)
    vc_ref[0] = p[:, OFF_C_V:D_IN].astype(BF16)


def _proj_call(h, modtab, g, w_in, rope_cs, rope_sn, qg, kg, ones_bd):
    bn = h.shape[0]
    tile = lambda w: pl.BlockSpec((1, TM, w), lambda b, t: (b, t, 0))
    const = lambda r, w: pl.BlockSpec((r, w), lambda b, t: (0, 0))
    out = lambda w, dt: jax.ShapeDtypeStruct((bn, TOK, w), dt)
    return pl.pallas_call(
        _proj_kernel,
        grid=(bn, N_TILES),
        in_specs=[
            tile(D_MODEL),
            pl.BlockSpec((1, N_MOD, D_MODEL), lambda b, t: (2 * b + t // LAT_TILES, 0, 0)),
            const(1, D_MODEL),
            _resident((D_MODEL, D_IN)),
            pl.BlockSpec((TM, 2 * HEAD_DIM), lambda b, t: (t, 0)),
            pl.BlockSpec((TM, 2 * HEAD_DIM), lambda b, t: (t, 0)),
            const(1, GQA_Q_WIDTH),
            const(1, GQA_KV_WIDTH),
            const(GQA_Q_WIDTH, GQA_Q_WIDTH),
        ],
        out_specs=[tile(NA_WIDTH), tile(NA_WIDTH), tile(NA_WIDTH), tile(POOL_WIDTH),
                   tile(GQA_Q_WIDTH), tile(GQA_KV_WIDTH), tile(GQA_KV_WIDTH)],
        out_shape=[out(NA_WIDTH, BF16), out(NA_WIDTH, BF16), out(NA_WIDTH, BF16),
                   out(POOL_WIDTH, F32), out(GQA_Q_WIDTH, BF16), out(GQA_KV_WIDTH, BF16),
                   out(GQA_KV_WIDTH, BF16)],
        compiler_params=_params(),
        name="proj",
    )(h, modtab, g.reshape(1, D_MODEL), w_in, rope_cs, rope_sn, qg, kg, ones_bd)


def _softmax_pv(scores, values):
    m = scores[0].max(axis=-1, keepdims=True)
    for s in scores[1:]:
        m = jnp.maximum(m, s.max(axis=-1, keepdims=True))
    den = None
    acc = None
    for s, v in zip(scores, values):
        e = jnp.exp(s - m)
        d = e.sum(axis=-1, keepdims=True)
        o = _dot(e.astype(BF16), v)
        den = d if den is None else den + d
        acc = o if acc is None else acc + o
    return acc / den


def _na_kernel(q_ref, k_ref, v_ref, bias_ref, o_ref):
    g = pl.program_id(1)
    q = q_ref[0]
    head_of_lane = lax.broadcasted_iota(jnp.int32, (1, NA_WIDTH), 1) // HEAD_DIM
    k_ctx = k_ref[0, SEQ:TOK, :]
    v_ctx = v_ref[0, SEQ:TOK, :]

    @pl.when(g < LAT_TILES)
    def _latent():
        first_row = jnp.clip(g * NA_QROWS - NA_WIN_H // 2, 0, GRID_H - NA_KROWS)
        start = pl.multiple_of(first_row * GRID_W, GRID_W)
        k_lat = k_ref[0, pl.ds(start, NA_KEYS), :]
        v_lat = v_ref[0, pl.ds(start, NA_KEYS), :]
        pattern = jnp.minimum(g, 2) + (g == LAT_TILES - 1).astype(jnp.int32)
        out = jnp.zeros((TM, NA_WIDTH), F32)
        for h in range(NA_HEADS):
            mine = head_of_lane == h
            qh = jnp.where(mine, q, jnp.zeros_like(q))
            s_lat = _dot_nt(qh, k_lat) + bias_ref[pattern, h]
            s_ctx = _dot_nt(qh, k_ctx)
            out = jnp.where(mine, _softmax_pv([s_lat, s_ctx], [v_lat, v_ctx]), out)
        o_ref[0] = out.astype(BF16)

    @pl.when(g == LAT_TILES)
    def _context():
        out = jnp.zeros((TM, NA_WIDTH), F32)
        for h in range(NA_HEADS):
            mine = head_of_lane == h
            qh = jnp.where(mine, q, jnp.zeros_like(q))
            out = jnp.where(mine, _softmax_pv([_dot_nt(qh, k_ctx)], [v_ctx]), out)
        o_ref[0] = out.astype(BF16)


def _na_call(qa, ka, va, bias):
    bn = qa.shape[0]
    return pl.pallas_call(
        _na_kernel,
        grid=(bn, N_TILES),
        in_specs=[
            pl.BlockSpec((1, TM, NA_WIDTH), lambda b, t: (b, t, 0)),
            pl.BlockSpec((1, TOK, NA_WIDTH), lambda b, t: (b, 0, 0)),
            pl.BlockSpec((1, TOK, NA_WIDTH), lambda b, t: (b, 0, 0)),
            _resident(bias.shape),
        ],
        out_specs=pl.BlockSpec((1, TM, NA_WIDTH), lambda b, t: (b, t, 0)),
        out_shape=jax.ShapeDtypeStruct((bn, TOK, NA_WIDTH), BF16),
        compiler_params=_params(),
        name="na",
    )(qa, ka, va, bias)


def _na_bias_index():
    col = np.arange(GRID_W)
    win_c0 = np.clip(col - NA_WIN_W // 2, 0, GRID_W - NA_WIN_W)
    col_ok = (col[None, :] >= win_c0[:, None]) & (col[None, :] < win_c0[:, None] + NA_WIN_W)
    dx = np.clip(col[None, :] - col[:, None], -(NA_WIN_W - 1), NA_WIN_W - 1) + NA_WIN_W - 1
    sel_x = np.eye(2 * NA_WIN_W - 1, dtype=np.float32)[dx]
    sel_y, valids = [], []
    for grp in (0, 1, 2, LAT_TILES - 1):
        first_row = int(np.clip(grp * NA_QROWS - NA_WIN_H // 2, 0, GRID_H - NA_KROWS))
        q_row = grp * NA_QROWS + np.arange(NA_QROWS)
        q_r0 = np.clip(q_row - NA_WIN_H // 2, 0, GRID_H - NA_WIN_H)
        k_row = first_row + np.arange(NA_KROWS)
        row_ok = (k_row[None, :] >= q_r0[:, None]) & (k_row[None, :] < q_r0[:, None] + NA_WIN_H)
        dy = np.clip(k_row[None, :] - q_row[:, None] + NA_WIN_H - 1, 0, 2 * NA_WIN_H - 2)
        sel_y.append(np.eye(2 * NA_WIN_H - 1, dtype=np.float32)[dy])
        valid = row_ok[:, None, :, None] & col_ok[None, :, None, :]
        valids.append(valid.reshape(TM, NA_KEYS))
    return np.stack(sel_y), sel_x, np.stack(valids)


_NA_SEL_Y, _NA_SEL_X, _NA_VALID = _na_bias_index()


def _na_bias(rpb):
    hi = lax.Precision.HIGHEST
    rows = jnp.einsum('pqka,lhab->lphqkb', _NA_SEL_Y, rpb, precision=hi)
    b = jnp.einsum('lphqkb,xcb->lphqxkc', rows, _NA_SEL_X, precision=hi)
    b = b.reshape(rpb.shape[0], _NA_SEL_Y.shape[0], NA_HEADS, TM, NA_KEYS)
    return jnp.where(_NA_VALID[None, :, None], b, NEG_INF)


def _shift_rows(a, d):
    n = a.shape[0]
    row = lax.broadcasted_iota(jnp.int32, a.shape, 0)
    rolled = pltpu.roll(a, d % n, axis=0)
    ok = (row >= d) if d > 0 else (row < n + d)
    return jnp.where(ok, rolled, 0.0)


def _pool_segment(u, w_bd, scale):
    n = u.shape[0]
    row = lax.broadcasted_iota(jnp.int32, u.shape, 0)
    grp = lax.broadcasted_iota(jnp.int32, u.shape, 1) // (POOL_WIDTH // len(POOL_WINDOWS))
    trail, lead = u, u
    total = jnp.zeros_like(u)
    count = jnp.ones_like(u)
    k = 1
    for i, w in enumerate(POOL_WINDOWS):
        while k < w // 2:
            trail = trail + _shift_rows(trail, k)
            lead = lead + _shift_rows(lead, -k)
            k *= 2
        win = _shift_rows(trail, 1) + lead
        cnt = (jnp.minimum(row + w // 2, n) - jnp.maximum(row - w // 2, 0)).astype(F32)
        total = jnp.where(grp == i, win, total)
        count = jnp.where(grp == i, cnt, count)
    y = (total / count - u).astype(BF16)
    return (_dot(y, w_bd) * scale).astype(BF16)


def _pool_kernel(u_ref, w_ref, s_ref, o_ref):
    w_bd = w_ref[...]
    scale = s_ref[...]
    o_ref[0, :SEQ, :] = _pool_segment(u_ref[0, :SEQ, :], w_bd, scale)
    o_ref[0, SEQ:, :] = _pool_segment(u_ref[0, SEQ:, :], w_bd, scale)


def _pool_call(u, w_bd, scale):
    bn = u.shape[0]
    return pl.pallas_call(
        _pool_kernel,
        grid=(bn,),
        in_specs=[
            pl.BlockSpec((1, TOK, POOL_WIDTH), lambda b: (b, 0, 0)),
            pl.BlockSpec((POOL_WIDTH, POOL_WIDTH), lambda b: (0, 0)),
            pl.BlockSpec((1, POOL_WIDTH), lambda b: (0, 0)),
        ],
        out_specs=pl.BlockSpec((1, TOK, POOL_WIDTH), lambda b: (b, 0, 0)),
        out_shape=jax.ShapeDtypeStruct((bn, TOK, POOL_WIDTH), BF16),
        compiler_params=pltpu.CompilerParams(
            dimension_semantics=("arbitrary",), vmem_limit_bytes=VMEM_LIMIT),
        name="pool",
    )(u, w_bd, scale.reshape(1, POOL_WIDTH))


def _gqa_tile(q, k, v):
    half_of_lane = lax.broadcasted_iota(jnp.int32, (1, GQA_KV_WIDTH), 1) // HEAD_DIM
    blocks = []
    for j in range(GQA_GROUP):
        qb = q[:, j * GQA_KV_WIDTH:(j + 1) * GQA_KV_WIDTH]
        out = jnp.zeros((q.shape[0], GQA_KV_WIDTH), F32)
        for kv in range(GQA_KV_HEADS):
            mine = half_of_lane == kv
            qh = jnp.where(mine, qb, jnp.zeros_like(qb))
            out = jnp.where(mine, _softmax_pv([_dot_nt(qh, k)], [v]), out)
        blocks.append(out)
    return jnp.concatenate(blocks, axis=-1).astype(BF16)


def _gqa_kernel(q_ref, k_ref, v_ref, o_ref):
    t = pl.program_id(1)

    @pl.when(t < LAT_TILES)
    def _latent():
        o_ref[0] = _gqa_tile(q_ref[0], k_ref[0], v_ref[0])

    @pl.when(t == LAT_TILES)
    def _context():
        o_ref[0] = _gqa_tile(q_ref[0], k_ref[0, SEQ:TOK, :], v_ref[0, SEQ:TOK, :])


def _gqa_call(qc, kc, vc):
    bn = qc.shape[0]
    return pl.pallas_call(
        _gqa_kernel,
        grid=(bn, N_TILES),
        in_specs=[
            pl.BlockSpec((1, TM, GQA_Q_WIDTH), lambda b, t: (b, t, 0)),
            pl.BlockSpec((1, TOK, GQA_KV_WIDTH), lambda b, t: (b, 0, 0)),
            pl.BlockSpec((1, TOK, GQA_KV_WIDTH), lambda b, t: (b, 0, 0)),
        ],
        out_specs=pl.BlockSpec((1, TM, GQA_Q_WIDTH), lambda b, t: (b, t, 0)),
        out_shape=jax.ShapeDtypeStruct((bn, TOK, GQA_Q_WIDTH), BF16),
        compiler_params=_params(),
        name="gqa",
    )(qc, kc, vc)


def _rope_tables():
    pos = np.arange(SEQ)
    inv_freq = ROPE_THETA ** (-np.arange(0, HEAD_DIM // 2, 2, dtype=np.float32) / (HEAD_DIM // 2))
    ang_row = (pos // GRID_W).astype(np.float32)[:, None] * inv_freq[None, :].astype(np.float32)
    ang_col = (pos % GRID_W).astype(np.float32)[:, None] * inv_freq[None, :].astype(np.float32)
    ang = np.concatenate([ang_row, ang_row, ang_col, ang_col], axis=-1).astype(np.float32)
    sign = np.tile(np.repeat(np.array([-1.0, 1.0], np.float32), HEAD_DIM // 4), 2)
    return ang, sign


_ROPE_ANG, _ROPE_SIGN = _rope_tables()

_Q_HEAD_ORDER = np.array([kv * GQA_GROUP + j for j in range(GQA_GROUP) for kv in range(GQA_KV_HEADS)])
_Q_LANE_ORDER = (_Q_HEAD_ORDER[:, None] * HEAD_DIM + np.arange(HEAD_DIM)[None, :]).reshape(-1)


def kernel(x, c, ctx, c_ctx, w_ada, b_ada, norm_g, ffn1_up, ffn1_down, ffn2_up, ffn2_down,
           w_in, w_out, na_rpb, pool_w, pool_scale, q_norm_g, k_norm_g, final_g):
    bn = x.shape[0]
    depth = w_ada.shape[0]
    assert x.shape == (bn, SEQ, D_MODEL) and ctx.shape == (bn, CTX_LEN, D_MODEL)
    assert 2 * bn <= ADA_ROWS

    cc = jnp.zeros((ADA_ROWS, D_MODEL), F32).at[:bn].set(c).at[bn].set(c_ctx)
    mods = _ada_call(cc, w_ada, b_ada)[:, :bn + 1].reshape(depth, bn + 1, N_MOD, D_MODEL)
    mod_ctx = jnp.broadcast_to(mods[:, bn:], (depth, bn, N_MOD, D_MODEL))
    modtab = jnp.stack([mods[:, :bn], mod_ctx], axis=2).reshape(depth, 2 * bn, N_MOD, D_MODEL)

    ang = jnp.asarray(_ROPE_ANG)
    cs_lat = jnp.tile(jnp.cos(ang), (1, 2))
    sn_lat = jnp.tile(jnp.sin(ang) * jnp.asarray(_ROPE_SIGN)[None, :], (1, 2))
    rope_cs = jnp.concatenate([cs_lat, jnp.ones((CTX_LEN, 2 * HEAD_DIM), F32)], axis=0)
    rope_sn = jnp.concatenate([sn_lat, jnp.zeros((CTX_LEN, 2 * HEAD_DIM), F32)], axis=0)
    ones_bd = jnp.asarray(
        np.kron(np.eye(GQA_Q_HEADS, dtype=np.float32), np.full((HEAD_DIM, HEAD_DIM), 1.0 / HEAD_DIM,
                                                               np.float32))).astype(BF16)

    q_cols = OFF_C_Q + _Q_LANE_ORDER
    w_in_p = jnp.concatenate([w_in[:, :, :OFF_C_Q], w_in[:, :, q_cols], w_in[:, :, OFF_C_K:]],
                             axis=-1).astype(BF16)
    c_rows = NA_WIDTH + POOL_WIDTH + _Q_LANE_ORDER
    w_out_p = jnp.concatenate([w_out[:, :NA_WIDTH + POOL_WIDTH], w_out[:, c_rows]], axis=1).astype(BF16)
    up1, dn1 = ffn1_up.astype(BF16), ffn1_down.astype(BF16)
    up2, dn2 = ffn2_up.astype(BF16), ffn2_down.astype(BF16)
    pool_bd = jnp.einsum('gh,lgcd->lgchd', jnp.eye(len(POOL_WINDOWS), dtype=F32), pool_w)
    pool_bd = pool_bd.reshape(depth, POOL_WIDTH, POOL_WIDTH).astype(BF16)
    qg = jnp.tile(q_norm_g, (1, GQA_Q_HEADS)).reshape(depth, 1, GQA_Q_WIDTH)
    kg = jnp.tile(k_norm_g, (1, GQA_KV_HEADS)).reshape(depth, 1, GQA_KV_WIDTH)
    na_bias = _na_bias(na_rpb)

    h = jnp.concatenate([x, ctx], axis=1)
    for l in range(depth):
        last = l == depth - 1
        h = _ffn_call(h, modtab[l], norm_g[l, 0], up1[l], dn1[l], mod_base=0)
        qa, ka, va, u, qc, kc, vc = _proj_call(
            h, modtab[l], norm_g[l, 1], w_in_p[l], rope_cs, rope_sn, qg[l], kg[l], ones_bd)
        oa = _na_call(qa, ka, va, na_bias[l])
        ob = _pool_call(u, pool_bd[l], pool_scale[l])
        oc = _gqa_call(qc, kc, vc)
        h = _ffn_call(h, modtab[l], norm_g[l, 2], up2[l], dn2[l], mod_base=6,
                      mix=(oa, ob, oc, w_out_p[l]), final_g=final_g if last else None)
    return h
```

```python
import functools

import numpy as np
import jax
import jax.numpy as jnp
from jax import lax
from jax.experimental import pallas as pl
from jax.experimental.pallas import tpu as pltpu

D_MODEL = 1024
SEQ = 2048
CTX_LEN = 256
TOK = SEQ + CTX_LEN
GRID_W = 64
GRID_H = SEQ // GRID_W
HEAD_DIM = 64
N_MOD = 9
D_FF = 2816
EPS = 1e-6
NEG_INF = -1e30
NA_HEADS = 4
NA_WIN_H = 8
NA_WIN_W = 16
NA_WIDTH = NA_HEADS * HEAD_DIM
POOL_WINDOWS = (2, 4, 8, 16)
POOL_WIDTH = 256
GQA_Q_HEADS = 8
GQA_KV_HEADS = 2
GQA_GROUP = GQA_Q_HEADS // GQA_KV_HEADS
GQA_Q_WIDTH = GQA_Q_HEADS * HEAD_DIM
GQA_KV_WIDTH = GQA_KV_HEADS * HEAD_DIM
ROPE_THETA = 10000.0
OFF_A_Q = 0
OFF_A_K = OFF_A_Q + NA_WIDTH
OFF_A_V = OFF_A_K + NA_WIDTH
OFF_B_U = OFF_A_V + NA_WIDTH
OFF_C_Q = OFF_B_U + POOL_WIDTH
OFF_C_K = OFF_C_Q + GQA_Q_WIDTH
OFF_C_V = OFF_C_K + GQA_KV_WIDTH
D_IN = OFF_C_V + GQA_KV_WIDTH
D_MIX = NA_WIDTH + POOL_WIDTH + GQA_Q_WIDTH
QK_SCALE = HEAD_DIM ** -0.5

TM = 256
N_TILES = TOK // TM
LAT_TILES = SEQ // TM
NA_QROWS = TM // GRID_W
NA_KROWS = 12
NA_KEYS = NA_KROWS * GRID_W
NA_KPAIRS = NA_KROWS // 2
NA_DY = 2 * NA_WIN_H - 1
NA_DY_PAD = 4
NA_DY_SLOTS = NA_DY + 2 * NA_DY_PAD - 1
NA_PATTERN_GROUPS = (0, 1, 2, LAT_TILES - 1)
FF_CHUNK = 256
ADA_TN = 1152
ADA_ROWS = 16
VMEM_LIMIT = 56 * 1024 * 1024

BF16 = jnp.bfloat16
F32 = jnp.float32


def _dot(a, b):
    return jnp.dot(a, b, preferred_element_type=F32)


def _dot_nt(a, b):
    return lax.dot_general(a, b, (((1,), (1,)), ((), ())), preferred_element_type=F32)


def _rms(x, g):
    ms = jnp.mean(x * x, axis=-1, keepdims=True)
    return x * lax.rsqrt(ms + EPS) * g


def _layer_block(l, shape, *, single_buffer=False):
    nd = len(shape)
    mode = dict(pipeline_mode=pl.Buffered(1)) if single_buffer else {}
    return pl.BlockSpec((None,) + tuple(shape), lambda *_: (l,) + (0,) * nd, **mode)


def _mod_block(l, bn):
    return pl.BlockSpec((None, None, N_MOD, D_MODEL),
                        lambda b, t: (l, b + (t // LAT_TILES) * (bn - b), 0, 0))


def _gain_block(l, which):
    return pl.BlockSpec((None, None, 1, D_MODEL), lambda b, t: (l, which, 0, 0))


def _token_tile(width):
    return pl.BlockSpec((None, TM, width), lambda b, t: (b, t, 0))


def _params(n_axes=2):
    return pltpu.CompilerParams(
        dimension_semantics=("arbitrary",) * n_axes, vmem_limit_bytes=VMEM_LIMIT)


def _ada_kernel(c_ref, w_ref, b_ref, o_ref):
    c = c_ref[...]
    s = c * jax.nn.sigmoid(c)
    o_ref[...] = jnp.dot(s, w_ref[...], preferred_element_type=F32,
                         precision=lax.Precision.HIGHEST) + b_ref[...]


def _ada_call(cc, w_ada, b_ada):
    depth = w_ada.shape[0]
    n = w_ada.shape[2]
    return pl.pallas_call(
        _ada_kernel,
        grid=(depth, n // ADA_TN),
        in_specs=[
            pl.BlockSpec((ADA_ROWS, D_MODEL), lambda l, j: (0, 0)),
            pl.BlockSpec((None, D_MODEL, ADA_TN), lambda l, j: (l, 0, j)),
            pl.BlockSpec((None, 1, ADA_TN), lambda l, j: (l, 0, j)),
        ],
        out_specs=pl.BlockSpec((None, ADA_ROWS, ADA_TN), lambda l, j: (l, 0, j)),
        out_shape=jax.ShapeDtypeStruct((depth, ADA_ROWS, n), F32),
        compiler_params=_params(),
        name="ada",
    )(cc, w_ada, b_ada.reshape(depth, 1, n))


def _ffn_kernel(*refs, mod_base, with_mix, final):
    it = iter(refs)
    h_ref = next(it)
    if with_mix:
        oa_ref, ob_ref, oc_ref = next(it), next(it), next(it)
    mod_ref, g_ref = next(it), next(it)
    if with_mix:
        wout_ref = next(it)
    wup_ref, wdn_ref = next(it), next(it)
    if final:
        fg_ref = next(it)
    o_ref, hid_ref = next(it), next(it)

    x = h_ref[...]
    mod = mod_ref[...]
    if with_mix:
        mixed = jnp.concatenate([oa_ref[...], ob_ref[...], oc_ref[...]], axis=-1)
        x = x + mod[5:6] * _dot(mixed, wout_ref[...])
    y = _rms(x, g_ref[...])
    xn = (y * (1.0 + mod[mod_base + 1:mod_base + 2]) + mod[mod_base:mod_base + 1]).astype(BF16)
    for c in range(D_FF // FF_CHUNK):
        lo = c * FF_CHUNK
        a = _dot(xn, wup_ref[:, lo:lo + FF_CHUNK])
        b = _dot(xn, wup_ref[:, D_FF + lo:D_FF + lo + FF_CHUNK])
        hid_ref[:, lo:lo + FF_CHUNK] = (a * jax.nn.sigmoid(a) * b).astype(BF16)
    out = x + (0.5 * mod[mod_base + 2:mod_base + 3]) * _dot(hid_ref[...], wdn_ref[...])
    if final:
        out = _rms(out, fg_ref[...])
    o_ref[...] = out


def _ffn_call(h, l, mods, gains, which_gain, w_up, w_dn, *, mod_base, mix=None, final_g=None):
    bn = h.shape[0]
    with_mix = mix is not None
    final = final_g is not None
    n_tiles = LAT_TILES if final else N_TILES
    args, specs = [h], [_token_tile(D_MODEL)]
    if with_mix:
        oa, ob, oc, w_out = mix
        args += [oa, ob, oc]
        specs += [_token_tile(NA_WIDTH), _token_tile(POOL_WIDTH), _token_tile(GQA_Q_WIDTH)]
    args += [mods, gains]
    specs += [_mod_block(l, bn), _gain_block(l, which_gain)]
    if with_mix:
        args.append(w_out)
        specs.append(_layer_block(l, (D_MIX, D_MODEL), single_buffer=True))
    args += [w_up, w_dn]
    specs += [_layer_block(l, (D_MODEL, 2 * D_FF), single_buffer=True),
              _layer_block(l, (D_FF, D_MODEL), single_buffer=True)]
    if final:
        args.append(final_g.reshape(1, D_MODEL))
        specs.append(pl.BlockSpec((1, D_MODEL), lambda b, t: (0, 0)))
    return pl.pallas_call(
        functools.partial(_ffn_kernel, mod_base=mod_base, with_mix=with_mix, final=final),
        grid=(bn, n_tiles),
        in_specs=specs,
        out_specs=_token_tile(D_MODEL),
        out_shape=jax.ShapeDtypeStruct((bn, n_tiles * TM, D_MODEL), F32),
        scratch_shapes=[pltpu.VMEM((TM, D_FF), BF16)],
        compiler_params=_params(),
        name="ffn_mix" if with_mix else "ffn",
    )(*args)


def _head_norm_rope(z, gain, cs, sn, ones_bd):
    width = z.shape[-1]
    zz = z * z
    hi = zz.astype(BF16)
    lo = (zz - hi.astype(F32)).astype(BF16)
    ms = _dot(hi, ones_bd) + _dot(lo, ones_bd)
    zn = z * lax.rsqrt(ms + EPS) * gain
    lane = lax.broadcasted_iota(jnp.int32, zn.shape, 1)
    partner = jnp.where((lane & 16) != 0,
                        pltpu.roll(zn, 16, axis=1), pltpu.roll(zn, width - 16, axis=1))
    reps = width // cs.shape[-1]
    if reps > 1:
        cs = jnp.concatenate([cs] * reps, axis=-1)
        sn = jnp.concatenate([sn] * reps, axis=-1)
    return zn * cs + partner * sn


def _proj_kernel(h_ref, mod_ref, g_ref, win_ref, cs_ref, sn_ref, qg_ref, kg_ref, bd_ref,
                 qa_ref, ka_ref, va_ref, u_ref, qc_ref, kc_ref, vc_ref):
    x = h_ref[...]
    mod = mod_ref[...]
    y = _rms(x, g_ref[...])
    a = (y * (1.0 + mod[4:5]) + mod[3:4]).astype(BF16)
    p = _dot(a, win_ref[...])
    qa_ref[...] = (p[:, OFF_A_Q:OFF_A_K] * QK_SCALE).astype(BF16)
    ka_ref[...] = p[:, OFF_A_K:OFF_A_V].astype(BF16)
    va_ref[...] = p[:, OFF_A_V:OFF_B_U].astype(BF16)
    u_ref[...] = p[:, OFF_B_U:OFF_C_Q]
    cs, sn = cs_ref[...], sn_ref[...]
    bd = bd_ref[...]
    qc = _head_norm_rope(p[:, OFF_C_Q:OFF_C_K], qg_ref[...], cs, sn, bd)
    qc_ref[...] = (qc * QK_SCALE).astype(BF16)
    kc = _head_norm_rope(p[:, OFF_C_K:OFF_C_V], kg_ref[...], cs, sn,
                         bd[:GQA_KV_WIDTH, :GQA_KV_WIDTH])
    kc_ref[...] = kc.astype(BF16)
    vc_ref[...] = p[:, OFF_C_V:D_IN].astype(BF16)


def _proj_call(h, l, mods, gains, w_in, rope_cs, rope_sn, qg, kg, ones_bd):
    bn = h.shape[0]
    const = lambda r, w: pl.BlockSpec((r, w), lambda b, t: (0, 0))
    out = lambda w, dt: jax.ShapeDtypeStruct((bn, TOK, w), dt)
    return pl.pallas_call(
        _proj_kernel,
        grid=(bn, N_TILES),
        in_specs=[
            _token_tile(D_MODEL),
            _mod_block(l, bn),
            _gain_block(l, 1),
            _layer_block(l, (D_MODEL, D_IN), single_buffer=True),
            pl.BlockSpec((TM, 2 * HEAD_DIM), lambda b, t: (t, 0)),
            pl.BlockSpec((TM, 2 * HEAD_DIM), lambda b, t: (t, 0)),
            _layer_block(l, (1, GQA_Q_WIDTH)),
            _layer_block(l, (1, GQA_KV_WIDTH)),
            const(GQA_Q_WIDTH, GQA_Q_WIDTH),
        ],
        out_specs=[_token_tile(NA_WIDTH), _token_tile(NA_WIDTH), _token_tile(NA_WIDTH),
                   _token_tile(POOL_WIDTH), _token_tile(GQA_Q_WIDTH), _token_tile(GQA_KV_WIDTH),
                   _token_tile(GQA_KV_WIDTH)],
        out_shape=[out(NA_WIDTH, BF16), out(NA_WIDTH, BF16), out(NA_WIDTH, BF16),
                   out(POOL_WIDTH, F32), out(GQA_Q_WIDTH, BF16), out(GQA_KV_WIDTH, BF16),
                   out(GQA_KV_WIDTH, BF16)],
        compiler_params=_params(),
        name="proj",
    )(h, mods, gains, w_in, rope_cs, rope_sn, qg, kg, ones_bd)


def _softmax_pv(scores, values):
    m = scores[0].max(axis=-1, keepdims=True)
    for s in scores[1:]:
        m = jnp.maximum(m, s.max(axis=-1, keepdims=True))
    den = None
    acc = None
    for s, v in zip(scores, values):
        e = jnp.exp(s - m)
        d = e.sum(axis=-1, keepdims=True)
        o = _dot(e.astype(BF16), v)
        den = d if den is None else den + d
        acc = o if acc is None else acc + o
    return acc / den


def _na_kernel(q_ref, k_ref, v_ref, tb_ref, rm_ref, o_ref):
    g = pl.program_id(1)
    q = q_ref[...]
    head_of_lane = lax.broadcasted_iota(jnp.int32, (1, NA_WIDTH), 1) // HEAD_DIM
    k_ctx = k_ref[SEQ:TOK, :]
    v_ctx = v_ref[SEQ:TOK, :]

    @pl.when(g < LAT_TILES)
    def _latent():
        first_row = jnp.clip(g * NA_QROWS - NA_WIN_H // 2, 0, GRID_H - NA_KROWS)
        start = pl.multiple_of(first_row * GRID_W, GRID_W)
        k_lat = k_ref[pl.ds(start, NA_KEYS), :]
        v_lat = v_ref[pl.ds(start, NA_KEYS), :]
        pattern = jnp.minimum(g, 2) + (g == LAT_TILES - 1).astype(jnp.int32)
        slot0 = first_row - g * NA_QROWS + NA_WIN_H - 1 + NA_DY_PAD
        out = jnp.zeros((TM, NA_WIDTH), F32)
        for h in range(NA_HEADS):
            bias = jnp.concatenate([
                jnp.concatenate([
                    tb_ref[h, slot0 + 2 * m - qr]
                    + rm_ref[pattern, qr * NA_KPAIRS + m:qr * NA_KPAIRS + m + 1, :]
                    for m in range(NA_KPAIRS)], axis=-1)
                for qr in range(NA_QROWS)], axis=0)
            mine = head_of_lane == h
            qh = jnp.where(mine, q, jnp.zeros_like(q))
            s_lat = _dot_nt(qh, k_lat) + bias
            s_ctx = _dot_nt(qh, k_ctx)
            out = jnp.where(mine, _softmax_pv([s_lat, s_ctx], [v_lat, v_ctx]), out)
        o_ref[...] = out.astype(BF16)

    @pl.when(g == LAT_TILES)
    def _context():
        out = jnp.zeros((TM, NA_WIDTH), F32)
        for h in range(NA_HEADS):
            mine = head_of_lane == h
            qh = jnp.where(mine, q, jnp.zeros_like(q))
            out = jnp.where(mine, _softmax_pv([_dot_nt(qh, k_ctx)], [v_ctx]), out)
        o_ref[...] = out.astype(BF16)


def _na_call(qa, ka, va, l, tb, rm):
    bn = qa.shape[0]
    whole = pl.BlockSpec((None, TOK, NA_WIDTH), lambda b, t: (b, 0, 0))
    return pl.pallas_call(
        _na_kernel,
        grid=(bn, N_TILES),
        in_specs=[
            _token_tile(NA_WIDTH), whole, whole,
            _layer_block(l, tb.shape[1:], single_buffer=True),
            pl.BlockSpec(rm.shape, lambda b, t: (0, 0, 0)),
        ],
        out_specs=_token_tile(NA_WIDTH),
        out_shape=jax.ShapeDtypeStruct((bn, TOK, NA_WIDTH), BF16),
        compiler_params=_params(),
        name="na",
    )(qa, ka, va, tb, rm)


def _na_static_tables():
    col = np.arange(GRID_W)
    win_c0 = np.clip(col - NA_WIN_W // 2, 0, GRID_W - NA_WIN_W)
    col_ok = (col[None, :] >= win_c0[:, None]) & (col[None, :] < win_c0[:, None] + NA_WIN_W)
    dx = np.clip(col[None, :] - col[:, None], -(NA_WIN_W - 1), NA_WIN_W - 1) + NA_WIN_W - 1
    sel_x = np.eye(2 * NA_WIN_W - 1, dtype=np.float32)[dx]
    row_mask = np.zeros((len(NA_PATTERN_GROUPS), NA_QROWS, NA_KPAIRS, 2, GRID_W), np.float32)
    for p, grp in enumerate(NA_PATTERN_GROUPS):
        first_row = int(np.clip(grp * NA_QROWS - NA_WIN_H // 2, 0, GRID_H - NA_KROWS))
        q_row = grp * NA_QROWS + np.arange(NA_QROWS)
        q_r0 = np.clip(q_row - NA_WIN_H // 2, 0, GRID_H - NA_WIN_H)
        k_row = first_row + np.arange(NA_KROWS)
        row_ok = (k_row[None, :] >= q_r0[:, None]) & (k_row[None, :] < q_r0[:, None] + NA_WIN_H)
        row_mask[p] = np.where(row_ok, 0.0, NEG_INF).reshape(NA_QROWS, NA_KPAIRS, 2, 1)
    row_mask = row_mask.reshape(len(NA_PATTERN_GROUPS), NA_QROWS * NA_KPAIRS, 2 * GRID_W)
    return sel_x, col_ok, row_mask


_NA_SEL_X, _NA_COL_OK, _NA_ROW_MASK = _na_static_tables()


def _na_bias_table(rpb):
    t = jnp.einsum('lhab,xcb->lhaxc', rpb, _NA_SEL_X, precision=lax.Precision.HIGHEST)
    t = jnp.where(_NA_COL_OK, t, NEG_INF)
    t = jnp.pad(t, ((0, 0), (0, 0), (NA_DY_PAD, NA_DY_PAD), (0, 0), (0, 0)))
    return jnp.concatenate([t[:, :, :-1], t[:, :, 1:]], axis=-1)


def _shift_rows(a, d):
    n = a.shape[0]
    row = lax.broadcasted_iota(jnp.int32, a.shape, 0)
    rolled = pltpu.roll(a, d % n, axis=0)
    ok = (row >= d) if d > 0 else (row < n + d)
    return jnp.where(ok, rolled, 0.0)


def _pool_segment(u, w_bd, scale):
    n = u.shape[0]
    row = lax.broadcasted_iota(jnp.int32, u.shape, 0)
    grp = lax.broadcasted_iota(jnp.int32, u.shape, 1) // (POOL_WIDTH // len(POOL_WINDOWS))
    trail, lead = u, u
    total = jnp.zeros_like(u)
    count = jnp.ones_like(u)
    k = 1
    for i, w in enumerate(POOL_WINDOWS):
        while k < w // 2:
            trail = trail + _shift_rows(trail, k)
            lead = lead + _shift_rows(lead, -k)
            k *= 2
        win = _shift_rows(trail, 1) + lead
        cnt = (jnp.minimum(row + w // 2, n) - jnp.maximum(row - w // 2, 0)).astype(F32)
        total = jnp.where(grp == i, win, total)
        count = jnp.where(grp == i, cnt, count)
    y = (total / count - u).astype(BF16)
    return (_dot(y, w_bd) * scale).astype(BF16)


def _pool_kernel(u_ref, w_ref, s_ref, o_ref):
    w_bd = w_ref[...]
    scale = s_ref[...]
    o_ref[:SEQ, :] = _pool_segment(u_ref[:SEQ, :], w_bd, scale)
    o_ref[SEQ:, :] = _pool_segment(u_ref[SEQ:, :], w_bd, scale)


def _pool_call(u, l, w_bd, scale):
    bn = u.shape[0]
    whole = pl.BlockSpec((None, TOK, POOL_WIDTH), lambda b: (b, 0, 0))
    return pl.pallas_call(
        _pool_kernel,
        grid=(bn,),
        in_specs=[whole, _layer_block(l, (POOL_WIDTH, POOL_WIDTH)), _layer_block(l, (1, POOL_WIDTH))],
        out_specs=whole,
        out_shape=jax.ShapeDtypeStruct((bn, TOK, POOL_WIDTH), BF16),
        compiler_params=_params(1),
        name="pool",
    )(u, w_bd, scale)


def _gqa_tile(q, k, v):
    half_of_lane = lax.broadcasted_iota(jnp.int32, (1, GQA_KV_WIDTH), 1) // HEAD_DIM
    blocks = []
    for j in range(GQA_GROUP):
        qb = q[:, j * GQA_KV_WIDTH:(j + 1) * GQA_KV_WIDTH]
        out = jnp.zeros((q.shape[0], GQA_KV_WIDTH), F32)
        for kv in range(GQA_KV_HEADS):
            mine = half_of_lane == kv
            qh = jnp.where(mine, qb, jnp.zeros_like(qb))
            out = jnp.where(mine, _softmax_pv([_dot_nt(qh, k)], [v]), out)
        blocks.append(out)
    return jnp.concatenate(blocks, axis=-1).astype(BF16)


def _gqa_kernel(q_ref, k_ref, v_ref, o_ref):
    t = pl.program_id(1)

    @pl.when(t < LAT_TILES)
    def _latent():
        o_ref[...] = _gqa_tile(q_ref[...], k_ref[...], v_ref[...])

    @pl.when(t == LAT_TILES)
    def _context():
        o_ref[...] = _gqa_tile(q_ref[...], k_ref[SEQ:TOK, :], v_ref[SEQ:TOK, :])


def _gqa_call(qc, kc, vc):
    bn = qc.shape[0]
    whole = pl.BlockSpec((None, TOK, GQA_KV_WIDTH), lambda b, t: (b, 0, 0))
    return pl.pallas_call(
        _gqa_kernel,
        grid=(bn, N_TILES),
        in_specs=[_token_tile(GQA_Q_WIDTH), whole, whole],
        out_specs=_token_tile(GQA_Q_WIDTH),
        out_shape=jax.ShapeDtypeStruct((bn, TOK, GQA_Q_WIDTH), BF16),
        compiler_params=_params(),
        name="gqa",
    )(qc, kc, vc)


def _rope_tables():
    pos = np.arange(SEQ)
    inv_freq = ROPE_THETA ** (-np.arange(0, HEAD_DIM // 2, 2, dtype=np.float32) / (HEAD_DIM // 2))
    ang_row = (pos // GRID_W).astype(np.float32)[:, None] * inv_freq[None, :].astype(np.float32)
    ang_col = (pos % GRID_W).astype(np.float32)[:, None] * inv_freq[None, :].astype(np.float32)
    ang = np.concatenate([ang_row, ang_row, ang_col, ang_col], axis=-1).astype(np.float32)
    sign = np.tile(np.repeat(np.array([-1.0, 1.0], np.float32), HEAD_DIM // 4), 2)
    return ang, sign


_ROPE_ANG, _ROPE_SIGN = _rope_tables()


def _group_major(a, axis):
    shape = a.shape
    a = a.reshape(shape[:axis] + (GQA_KV_HEADS, GQA_GROUP, HEAD_DIM) + shape[axis + 1:])
    return jnp.swapaxes(a, axis, axis + 1).reshape(shape)


def kernel(x, c, ctx, c_ctx, w_ada, b_ada, norm_g, ffn1_up, ffn1_down, ffn2_up, ffn2_down,
           w_in, w_out, na_rpb, pool_w, pool_scale, q_norm_g, k_norm_g, final_g):
    bn = x.shape[0]
    depth = w_ada.shape[0]
    assert x.shape == (bn, SEQ, D_MODEL) and ctx.shape == (bn, CTX_LEN, D_MODEL)
    assert bn < ADA_ROWS

    cc = jnp.zeros((ADA_ROWS, D_MODEL), F32).at[:bn].set(c).at[bn].set(c_ctx)
    mods = _ada_call(cc, w_ada, b_ada).reshape(depth, ADA_ROWS, N_MOD, D_MODEL)
    gains = norm_g.reshape(depth, 3, 1, D_MODEL)

    ang = jnp.asarray(_ROPE_ANG)
    cs_lat = jnp.tile(jnp.cos(ang), (1, 2))
    sn_lat = jnp.tile(jnp.sin(ang) * jnp.asarray(_ROPE_SIGN)[None, :], (1, 2))
    rope_cs = jnp.concatenate([cs_lat, jnp.ones((CTX_LEN, 2 * HEAD_DIM), F32)], axis=0)
    rope_sn = jnp.concatenate([sn_lat, jnp.zeros((CTX_LEN, 2 * HEAD_DIM), F32)], axis=0)
    ones_bd = jnp.asarray(
        np.kron(np.eye(GQA_Q_HEADS, dtype=np.float32), np.full((HEAD_DIM, HEAD_DIM), 1.0 / HEAD_DIM,
                                                               np.float32))).astype(BF16)

    w_in_p = jnp.concatenate([w_in[:, :, :OFF_C_Q], _group_major(w_in[:, :, OFF_C_Q:OFF_C_K], 2),
                              w_in[:, :, OFF_C_K:]], axis=-1).astype(BF16)
    w_out_p = jnp.concatenate([w_out[:, :NA_WIDTH + POOL_WIDTH],
                               _group_major(w_out[:, NA_WIDTH + POOL_WIDTH:], 1)], axis=1).astype(BF16)
    up1, dn1 = ffn1_up.astype(BF16), ffn1_down.astype(BF16)
    up2, dn2 = ffn2_up.astype(BF16), ffn2_down.astype(BF16)
    pool_bd = jnp.einsum('gh,lgcd->lgchd', jnp.eye(len(POOL_WINDOWS), dtype=F32), pool_w)
    pool_bd = pool_bd.reshape(depth, POOL_WIDTH, POOL_WIDTH).astype(BF16)
    pool_sc = pool_scale.reshape(depth, 1, POOL_WIDTH)
    qg = jnp.tile(q_norm_g, (1, GQA_Q_HEADS)).reshape(depth, 1, GQA_Q_WIDTH)
    kg = jnp.tile(k_norm_g, (1, GQA_KV_HEADS)).reshape(depth, 1, GQA_KV_WIDTH)
    na_tb = _na_bias_table(na_rpb)
    na_rm = jnp.asarray(_NA_ROW_MASK)

    h = jnp.concatenate([x, ctx], axis=1)
    for l in range(depth):
        last = l == depth - 1
        h = _ffn_call(h, l, mods, gains, 0, up1, dn1, mod_base=0)
        qa, ka, va, u, qc, kc, vc = _proj_call(h, l, mods, gains, w_in_p, rope_cs, rope_sn, qg, kg, ones_bd)
        oa = _na_call(qa, ka, va, l, na_tb, na_rm)
        ob = _pool_call(u, l, pool_bd, pool_sc)
        oc = _gqa_call(qc, kc, vc)
        h = _ffn_call(h, l, mods, gains, 2, up2, dn2, mod_base=6,
                      mix=(oa, ob, oc, w_out_p), final_g=final_g if last else None)
    return h
```

```python
import functools

import numpy as np
import jax
import jax.numpy as jnp
from jax import lax
from jax.experimental import pallas as pl
from jax.experimental.pallas import tpu as pltpu

D_MODEL = 1024
SEQ = 2048
CTX_LEN = 256
TOK = SEQ + CTX_LEN
GRID_W = 64
GRID_H = SEQ // GRID_W
HEAD_DIM = 64
N_MOD = 9
D_FF = 2816
EPS = 1e-6
NEG_INF = -1e30
NA_HEADS = 4
NA_WIN_H = 8
NA_WIN_W = 16
NA_WIDTH = NA_HEADS * HEAD_DIM
POOL_WINDOWS = (2, 4, 8, 16)
POOL_WIDTH = 256
GQA_Q_HEADS = 8
GQA_KV_HEADS = 2
GQA_GROUP = GQA_Q_HEADS // GQA_KV_HEADS
GQA_Q_WIDTH = GQA_Q_HEADS * HEAD_DIM
GQA_KV_WIDTH = GQA_KV_HEADS * HEAD_DIM
ROPE_THETA = 10000.0
OFF_A_Q = 0
OFF_A_K = OFF_A_Q + NA_WIDTH
OFF_A_V = OFF_A_K + NA_WIDTH
OFF_B_U = OFF_A_V + NA_WIDTH
OFF_C_Q = OFF_B_U + POOL_WIDTH
OFF_C_K = OFF_C_Q + GQA_Q_WIDTH
OFF_C_V = OFF_C_K + GQA_KV_WIDTH
D_IN = OFF_C_V + GQA_KV_WIDTH
D_MIX = NA_WIDTH + POOL_WIDTH + GQA_Q_WIDTH
QK_SCALE = HEAD_DIM ** -0.5
LOG2_E = float(np.log2(np.e))

TM = 256
N_TILES = TOK // TM
LAT_TILES = SEQ // TM
NA_QROWS = TM // GRID_W
NA_KROWS = 12
NA_KEYS = NA_KROWS * GRID_W
NA_KPAIRS = NA_KROWS // 2
NA_DY = 2 * NA_WIN_H - 1
NA_DY_PAD = 4
NA_DY_SLOTS = NA_DY + 2 * NA_DY_PAD - 1
NA_PATTERN_GROUPS = (0, 1, 2, LAT_TILES - 1)
GQA_KEY_CHUNK = 768
FF_CHUNK = 256
ADA_TN = 1152
ADA_ROWS = 16
VMEM_LIMIT = 56 * 1024 * 1024

BF16 = jnp.bfloat16
F32 = jnp.float32


def _dot(a, b):
    return jnp.dot(a, b, preferred_element_type=F32)


def _dot_nt(a, b):
    return lax.dot_general(a, b, (((1,), (1,)), ((), ())), preferred_element_type=F32)


def _rms(x, g):
    ms = jnp.mean(x * x, axis=-1, keepdims=True)
    return x * lax.rsqrt(ms + EPS) * g


def _layer_block(l, shape, *, single_buffer=False):
    nd = len(shape)
    mode = dict(pipeline_mode=pl.Buffered(1)) if single_buffer else {}
    return pl.BlockSpec((None,) + tuple(shape), lambda *_: (l,) + (0,) * nd, **mode)


def _mod_block(l, bn):
    return pl.BlockSpec((None, None, N_MOD, D_MODEL),
                        lambda b, t: (l, b + (t // LAT_TILES) * (bn - b), 0, 0))


def _gain_block(l, which):
    return pl.BlockSpec((None, None, 1, D_MODEL), lambda b, t: (l, which, 0, 0))


def _token_tile(width):
    return pl.BlockSpec((None, TM, width), lambda b, t: (b, t, 0))


def _params(n_axes=2):
    return pltpu.CompilerParams(
        dimension_semantics=("arbitrary",) * n_axes, vmem_limit_bytes=VMEM_LIMIT)


def _ada_kernel(c_ref, w_ref, b_ref, o_ref):
    c = c_ref[...]
    s = c * jax.nn.sigmoid(c)
    o_ref[...] = jnp.dot(s, w_ref[...], preferred_element_type=F32,
                         precision=lax.Precision.HIGHEST) + b_ref[...]


def _ada_call(cc, w_ada, b_ada):
    depth = w_ada.shape[0]
    n = w_ada.shape[2]
    return pl.pallas_call(
        _ada_kernel,
        grid=(depth, n // ADA_TN),
        in_specs=[
            pl.BlockSpec((ADA_ROWS, D_MODEL), lambda l, j: (0, 0)),
            pl.BlockSpec((None, D_MODEL, ADA_TN), lambda l, j: (l, 0, j)),
            pl.BlockSpec((None, 1, ADA_TN), lambda l, j: (l, 0, j)),
        ],
        out_specs=pl.BlockSpec((None, ADA_ROWS, ADA_TN), lambda l, j: (l, 0, j)),
        out_shape=jax.ShapeDtypeStruct((depth, ADA_ROWS, n), F32),
        compiler_params=_params(),
        name="ada",
    )(cc, w_ada, b_ada.reshape(depth, 1, n))


def _ffn_kernel(*refs, mod_base, with_mix, final):
    it = iter(refs)
    h_ref = next(it)
    if with_mix:
        oa_ref, ob_ref, oc_ref = next(it), next(it), next(it)
    mod_ref, g_ref = next(it), next(it)
    if with_mix:
        wout_ref = next(it)
    wup_ref, wdn_ref = next(it), next(it)
    if final:
        fg_ref = next(it)
    o_ref, hid_ref = next(it), next(it)

    x = h_ref[...]
    mod = mod_ref[...]
    if with_mix:
        mixed = jnp.concatenate([oa_ref[...], ob_ref[...], oc_ref[...]], axis=-1)
        x = x + mod[5:6] * _dot(mixed, wout_ref[...])
    y = _rms(x, g_ref[...])
    xn = (y * (1.0 + mod[mod_base + 1:mod_base + 2]) + mod[mod_base:mod_base + 1]).astype(BF16)
    for c in range(D_FF // FF_CHUNK):
        lo = c * FF_CHUNK
        a = _dot(xn, wup_ref[:, lo:lo + FF_CHUNK])
        b = _dot(xn, wup_ref[:, D_FF + lo:D_FF + lo + FF_CHUNK])
        hid_ref[:, lo:lo + FF_CHUNK] = (a * jax.nn.sigmoid(a) * b).astype(BF16)
    out = x + (0.5 * mod[mod_base + 2:mod_base + 3]) * _dot(hid_ref[...], wdn_ref[...])
    if final:
        out = _rms(out, fg_ref[...])
    o_ref[...] = out


def _ffn_call(h, l, mods, gains, which_gain, w_up, w_dn, *, mod_base, mix=None, final_g=None):
    bn = h.shape[0]
    with_mix = mix is not None
    final = final_g is not None
    n_tiles = LAT_TILES if final else N_TILES
    args, specs = [h], [_token_tile(D_MODEL)]
    if with_mix:
        oa, ob, oc, w_out = mix
        args += [oa, ob, oc]
        specs += [_token_tile(NA_WIDTH), _token_tile(POOL_WIDTH), _token_tile(GQA_Q_WIDTH)]
    args += [mods, gains]
    specs += [_mod_block(l, bn), _gain_block(l, which_gain)]
    if with_mix:
        args.append(w_out)
        specs.append(_layer_block(l, (D_MIX, D_MODEL), single_buffer=True))
    args += [w_up, w_dn]
    specs += [_layer_block(l, (D_MODEL, 2 * D_FF), single_buffer=True),
              _layer_block(l, (D_FF, D_MODEL), single_buffer=True)]
    if final:
        args.append(final_g.reshape(1, D_MODEL))
        specs.append(pl.BlockSpec((1, D_MODEL), lambda b, t: (0, 0)))
    return pl.pallas_call(
        functools.partial(_ffn_kernel, mod_base=mod_base, with_mix=with_mix, final=final),
        grid=(bn, n_tiles),
        in_specs=specs,
        out_specs=_token_tile(D_MODEL),
        out_shape=jax.ShapeDtypeStruct((bn, n_tiles * TM, D_MODEL), F32),
        scratch_shapes=[pltpu.VMEM((TM, D_FF), BF16)],
        compiler_params=_params(),
        name="ffn_mix" if with_mix else "ffn",
    )(*args)


def _head_norm_rope(z, gain, cs, sn, ones_bd):
    width = z.shape[-1]
    zz = z * z
    hi = zz.astype(BF16)
    lo = (zz - hi.astype(F32)).astype(BF16)
    ms = _dot(hi, ones_bd) + _dot(lo, ones_bd)
    zn = z * lax.rsqrt(ms + EPS) * gain
    lane = lax.broadcasted_iota(jnp.int32, zn.shape, 1)
    partner = jnp.where((lane & 16) != 0,
                        pltpu.roll(zn, 16, axis=1), pltpu.roll(zn, width - 16, axis=1))
    reps = width // cs.shape[-1]
    if reps > 1:
        cs = jnp.concatenate([cs] * reps, axis=-1)
        sn = jnp.concatenate([sn] * reps, axis=-1)
    return zn * cs + partner * sn


def _proj_kernel(h_ref, mod_ref, g_ref, win_ref, cs_ref, sn_ref, qg_ref, kg_ref, bd_ref,
                 qa_ref, ka_ref, va_ref, u_ref, qc_ref, kc_ref, vct_ref):
    x = h_ref[...]
    mod = mod_ref[...]
    y = _rms(x, g_ref[...])
    a = (y * (1.0 + mod[4:5]) + mod[3:4]).astype(BF16)
    p = _dot(a, win_ref[...])
    qa_ref[...] = (p[:, OFF_A_Q:OFF_A_K] * QK_SCALE).astype(BF16)
    ka_ref[...] = p[:, OFF_A_K:OFF_A_V].astype(BF16)
    va_ref[...] = p[:, OFF_A_V:OFF_B_U].astype(BF16)
    u_ref[...] = p[:, OFF_B_U:OFF_C_Q]
    cs, sn = cs_ref[...], sn_ref[...]
    bd = bd_ref[...]
    qc = _head_norm_rope(p[:, OFF_C_Q:OFF_C_K], qg_ref[...], cs, sn, bd)
    qc_ref[...] = (qc * (QK_SCALE * LOG2_E)).astype(BF16)
    kc = _head_norm_rope(p[:, OFF_C_K:OFF_C_V], kg_ref[...], cs, sn,
                         bd[:GQA_KV_WIDTH, :GQA_KV_WIDTH])
    kc_ref[...] = kc.astype(BF16)
    vct_ref[...] = p[:, OFF_C_V:D_IN].T.astype(BF16)


def _proj_call(h, l, mods, gains, w_in, rope_cs, rope_sn, qg, kg, ones_bd):
    bn = h.shape[0]
    const = lambda r, w: pl.BlockSpec((r, w), lambda b, t: (0, 0))
    out = lambda w, dt: jax.ShapeDtypeStruct((bn, TOK, w), dt)
    return pl.pallas_call(
        _proj_kernel,
        grid=(bn, N_TILES),
        in_specs=[
            _token_tile(D_MODEL),
            _mod_block(l, bn),
            _gain_block(l, 1),
            _layer_block(l, (D_MODEL, D_IN), single_buffer=True),
            pl.BlockSpec((TM, 2 * HEAD_DIM), lambda b, t: (t, 0)),
            pl.BlockSpec((TM, 2 * HEAD_DIM), lambda b, t: (t, 0)),
            _layer_block(l, (1, GQA_Q_WIDTH)),
            _layer_block(l, (1, GQA_KV_WIDTH)),
            const(GQA_Q_WIDTH, GQA_Q_WIDTH),
        ],
        out_specs=[_token_tile(NA_WIDTH), _token_tile(NA_WIDTH), _token_tile(NA_WIDTH),
                   _token_tile(POOL_WIDTH), _token_tile(GQA_Q_WIDTH), _token_tile(GQA_KV_WIDTH),
                   pl.BlockSpec((None, GQA_KV_WIDTH, TM), lambda b, t: (b, 0, t))],
        out_shape=[out(NA_WIDTH, BF16), out(NA_WIDTH, BF16), out(NA_WIDTH, BF16),
                   out(POOL_WIDTH, F32), out(GQA_Q_WIDTH, BF16), out(GQA_KV_WIDTH, BF16),
                   jax.ShapeDtypeStruct((bn, GQA_KV_WIDTH, TOK), BF16)],
        compiler_params=_params(),
        name="proj",
    )(h, mods, gains, w_in, rope_cs, rope_sn, qg, kg, ones_bd)


def _softmax_pv(scores, values):
    m = scores[0].max(axis=-1, keepdims=True)
    for s in scores[1:]:
        m = jnp.maximum(m, s.max(axis=-1, keepdims=True))
    den = None
    acc = None
    for s, v in zip(scores, values):
        e = jnp.exp(s - m)
        d = e.sum(axis=-1, keepdims=True)
        o = _dot(e.astype(BF16), v)
        den = d if den is None else den + d
        acc = o if acc is None else acc + o
    return acc / den


def _na_kernel(q_ref, k_ref, v_ref, tb_ref, rm_ref, o_ref):
    g = pl.program_id(1)
    q = q_ref[...]
    head_of_lane = lax.broadcasted_iota(jnp.int32, (1, NA_WIDTH), 1) // HEAD_DIM
    k_ctx = k_ref[SEQ:TOK, :]
    v_ctx = v_ref[SEQ:TOK, :]

    @pl.when(g < LAT_TILES)
    def _latent():
        first_row = jnp.clip(g * NA_QROWS - NA_WIN_H // 2, 0, GRID_H - NA_KROWS)
        start = pl.multiple_of(first_row * GRID_W, GRID_W)
        k_lat = k_ref[pl.ds(start, NA_KEYS), :]
        v_lat = v_ref[pl.ds(start, NA_KEYS), :]
        pattern = jnp.minimum(g, 2) + (g == LAT_TILES - 1).astype(jnp.int32)
        slot0 = first_row - g * NA_QROWS + NA_WIN_H - 1 + NA_DY_PAD
        out = jnp.zeros((TM, NA_WIDTH), F32)
        for h in range(NA_HEADS):
            bias = jnp.concatenate([
                jnp.concatenate([
                    tb_ref[h, slot0 + 2 * m - qr]
                    + rm_ref[pattern, qr * NA_KPAIRS + m:qr * NA_KPAIRS + m + 1, :]
                    for m in range(NA_KPAIRS)], axis=-1)
                for qr in range(NA_QROWS)], axis=0)
            mine = head_of_lane == h
            qh = jnp.where(mine, q, jnp.zeros_like(q))
            s_lat = _dot_nt(qh, k_lat) + bias
            s_ctx = _dot_nt(qh, k_ctx)
            out = jnp.where(mine, _softmax_pv([s_lat, s_ctx], [v_lat, v_ctx]), out)
        o_ref[...] = out.astype(BF16)

    @pl.when(g == LAT_TILES)
    def _context():
        out = jnp.zeros((TM, NA_WIDTH), F32)
        for h in range(NA_HEADS):
            mine = head_of_lane == h
            qh = jnp.where(mine, q, jnp.zeros_like(q))
            out = jnp.where(mine, _softmax_pv([_dot_nt(qh, k_ctx)], [v_ctx]), out)
        o_ref[...] = out.astype(BF16)


def _na_call(qa, ka, va, l, tb, rm):
    bn = qa.shape[0]
    whole = pl.BlockSpec((None, TOK, NA_WIDTH), lambda b, t: (b, 0, 0))
    return pl.pallas_call(
        _na_kernel,
        grid=(bn, N_TILES),
        in_specs=[
            _token_tile(NA_WIDTH), whole, whole,
            _layer_block(l, tb.shape[1:], single_buffer=True),
            pl.BlockSpec(rm.shape, lambda b, t: (0, 0, 0)),
        ],
        out_specs=_token_tile(NA_WIDTH),
        out_shape=jax.ShapeDtypeStruct((bn, TOK, NA_WIDTH), BF16),
        compiler_params=_params(),
        name="na",
    )(qa, ka, va, tb, rm)


def _na_static_tables():
    col = np.arange(GRID_W)
    win_c0 = np.clip(col - NA_WIN_W // 2, 0, GRID_W - NA_WIN_W)
    col_ok = (col[None, :] >= win_c0[:, None]) & (col[None, :] < win_c0[:, None] + NA_WIN_W)
    dx = np.clip(col[None, :] - col[:, None], -(NA_WIN_W - 1), NA_WIN_W - 1) + NA_WIN_W - 1
    sel_x = np.eye(2 * NA_WIN_W - 1, dtype=np.float32)[dx]
    row_mask = np.zeros((len(NA_PATTERN_GROUPS), NA_QROWS, NA_KPAIRS, 2, GRID_W), np.float32)
    for p, grp in enumerate(NA_PATTERN_GROUPS):
        first_row = int(np.clip(grp * NA_QROWS - NA_WIN_H // 2, 0, GRID_H - NA_KROWS))
        q_row = grp * NA_QROWS + np.arange(NA_QROWS)
        q_r0 = np.clip(q_row - NA_WIN_H // 2, 0, GRID_H - NA_WIN_H)
        k_row = first_row + np.arange(NA_KROWS)
        row_ok = (k_row[None, :] >= q_r0[:, None]) & (k_row[None, :] < q_r0[:, None] + NA_WIN_H)
        row_mask[p] = np.where(row_ok, 0.0, NEG_INF).reshape(NA_QROWS, NA_KPAIRS, 2, 1)
    row_mask = row_mask.reshape(len(NA_PATTERN_GROUPS), NA_QROWS * NA_KPAIRS, 2 * GRID_W)
    return sel_x, col_ok, row_mask


_NA_SEL_X, _NA_COL_OK, _NA_ROW_MASK = _na_static_tables()


def _na_bias_table(rpb):
    t = jnp.einsum('lhab,xcb->lhaxc', rpb, _NA_SEL_X, precision=lax.Precision.HIGHEST)
    t = jnp.where(_NA_COL_OK, t, NEG_INF)
    t = jnp.pad(t, ((0, 0), (0, 0), (NA_DY_PAD, NA_DY_PAD), (0, 0), (0, 0)))
    return jnp.concatenate([t[:, :, :-1], t[:, :, 1:]], axis=-1)


def _shift_rows(a, d):
    n = a.shape[0]
    row = lax.broadcasted_iota(jnp.int32, a.shape, 0)
    rolled = pltpu.roll(a, d % n, axis=0)
    ok = (row >= d) if d > 0 else (row < n + d)
    return jnp.where(ok, rolled, 0.0)


def _pool_segment(u, w_bd, scale):
    n = u.shape[0]
    row = lax.broadcasted_iota(jnp.int32, u.shape, 0)
    grp = lax.broadcasted_iota(jnp.int32, u.shape, 1) // (POOL_WIDTH // len(POOL_WINDOWS))
    trail, lead = u, u
    total = jnp.zeros_like(u)
    count = jnp.ones_like(u)
    k = 1
    for i, w in enumerate(POOL_WINDOWS):
        while k < w // 2:
            trail = trail + _shift_rows(trail, k)
            lead = lead + _shift_rows(lead, -k)
            k *= 2
        win = _shift_rows(trail, 1) + lead
        cnt = (jnp.minimum(row + w // 2, n) - jnp.maximum(row - w // 2, 0)).astype(F32)
        total = jnp.where(grp == i, win, total)
        count = jnp.where(grp == i, cnt, count)
    y = (total / count - u).astype(BF16)
    return (_dot(y, w_bd) * scale).astype(BF16)


def _pool_kernel(u_ref, w_ref, s_ref, o_ref):
    w_bd = w_ref[...]
    scale = s_ref[...]
    o_ref[:SEQ, :] = _pool_segment(u_ref[:SEQ, :], w_bd, scale)
    o_ref[SEQ:, :] = _pool_segment(u_ref[SEQ:, :], w_bd, scale)


def _pool_call(u, l, w_bd, scale):
    bn = u.shape[0]
    whole = pl.BlockSpec((None, TOK, POOL_WIDTH), lambda b: (b, 0, 0))
    return pl.pallas_call(
        _pool_kernel,
        grid=(bn,),
        in_specs=[whole, _layer_block(l, (POOL_WIDTH, POOL_WIDTH)), _layer_block(l, (1, POOL_WIDTH))],
        out_specs=whole,
        out_shape=jax.ShapeDtypeStruct((bn, TOK, POOL_WIDTH), BF16),
        compiler_params=_params(1),
        name="pool",
    )(u, w_bd, scale)


def _gqa_tile(q, k_ref, vt_ref, key_chunks):
    half_of_lane = lax.broadcasted_iota(jnp.int32, (1, GQA_KV_WIDTH), 1) // HEAD_DIM
    stages = [(j, lo, hi) for j in range(GQA_GROUP) for lo, hi in key_chunks]

    def scores_t(stage):
        j, lo, hi = stage
        qb = q[:, j * GQA_KV_WIDTH:(j + 1) * GQA_KV_WIDTH]
        q2 = jnp.concatenate([jnp.where(half_of_lane == kv, qb, jnp.zeros_like(qb))
                              for kv in range(GQA_KV_HEADS)], axis=0)
        return _dot_nt(k_ref[lo:hi, :], q2)

    partial = {j: [] for j in range(GQA_GROUP)}
    s_next = scores_t(stages[0])
    for i, (j, lo, hi) in enumerate(stages):
        s_t = s_next
        if i + 1 < len(stages):
            s_next = scores_t(stages[i + 1])
        m = s_t.max(axis=0, keepdims=True)
        e = jnp.exp2(s_t - m)
        den = e.sum(axis=0, keepdims=True)
        e = e.astype(BF16)
        o_t = [_dot(vt_ref[kv * HEAD_DIM:(kv + 1) * HEAD_DIM, lo:hi], e[:, kv * TM:(kv + 1) * TM])
               for kv in range(GQA_KV_HEADS)]
        partial[j].append((m, den, o_t))

    blocks = []
    for j in range(GQA_GROUP):
        m_all = functools.reduce(jnp.maximum, [m for m, _, _ in partial[j]])
        weights = [jnp.exp2(m - m_all) for m, _, _ in partial[j]]
        den = sum(w * d for w, (_, d, _) in zip(weights, partial[j]))
        o_t = jnp.concatenate([
            sum(w[:, kv * TM:(kv + 1) * TM] * o[kv] for w, (_, _, o) in zip(weights, partial[j]))
            / den[:, kv * TM:(kv + 1) * TM]
            for kv in range(GQA_KV_HEADS)], axis=0)
        blocks.append(o_t.T)
    return jnp.concatenate(blocks, axis=-1).astype(BF16)


def _gqa_kernel(q_ref, k_ref, vt_ref, o_ref):
    t = pl.program_id(1)

    @pl.when(t < LAT_TILES)
    def _latent():
        chunks = [(lo, lo + GQA_KEY_CHUNK) for lo in range(0, TOK, GQA_KEY_CHUNK)]
        o_ref[...] = _gqa_tile(q_ref[...], k_ref, vt_ref, chunks)

    @pl.when(t == LAT_TILES)
    def _context():
        o_ref[...] = _gqa_tile(q_ref[...], k_ref, vt_ref, [(SEQ, TOK)])


def _gqa_call(qc, kc, vc_t):
    bn = qc.shape[0]
    return pl.pallas_call(
        _gqa_kernel,
        grid=(bn, N_TILES),
        in_specs=[_token_tile(GQA_Q_WIDTH),
                  pl.BlockSpec((None, TOK, GQA_KV_WIDTH), lambda b, t: (b, 0, 0)),
                  pl.BlockSpec((None, GQA_KV_WIDTH, TOK), lambda b, t: (b, 0, 0))],
        out_specs=_token_tile(GQA_Q_WIDTH),
        out_shape=jax.ShapeDtypeStruct((bn, TOK, GQA_Q_WIDTH), BF16),
        compiler_params=_params(),
        name="gqa",
    )(qc, kc, vc_t)


def _rope_tables():
    pos = np.arange(SEQ)
    inv_freq = ROPE_THETA ** (-np.arange(0, HEAD_DIM // 2, 2, dtype=np.float32) / (HEAD_DIM // 2))
    ang_row = (pos // GRID_W).astype(np.float32)[:, None] * inv_freq[None, :].astype(np.float32)
    ang_col = (pos % GRID_W).astype(np.float32)[:, None] * inv_freq[None, :].astype(np.float32)
    ang = np.concatenate([ang_row, ang_row, ang_col, ang_col], axis=-1).astype(np.float32)
    sign = np.tile(np.repeat(np.array([-1.0, 1.0], np.float32), HEAD_DIM // 4), 2)
    return ang, sign


_ROPE_ANG, _ROPE_SIGN = _rope_tables()


def _group_major(a, axis):
    shape = a.shape
    a = a.reshape(shape[:axis] + (GQA_KV_HEADS, GQA_GROUP, HEAD_DIM) + shape[axis + 1:])
    return jnp.swapaxes(a, axis, axis + 1).reshape(shape)


def kernel(x, c, ctx, c_ctx, w_ada, b_ada, norm_g, ffn1_up, ffn1_down, ffn2_up, ffn2_down,
           w_in, w_out, na_rpb, pool_w, pool_scale, q_norm_g, k_norm_g, final_g):
    bn = x.shape[0]
    depth = w_ada.shape[0]
    assert x.shape == (bn, SEQ, D_MODEL) and ctx.shape == (bn, CTX_LEN, D_MODEL)
    assert bn < ADA_ROWS

    cc = jnp.zeros((ADA_ROWS, D_MODEL), F32).at[:bn].set(c).at[bn].set(c_ctx)
    mods = _ada_call(cc, w_ada, b_ada).reshape(depth, ADA_ROWS, N_MOD, D_MODEL)
    gains = norm_g.reshape(depth, 3, 1, D_MODEL)

    ang = jnp.asarray(_ROPE_ANG)
    cs_lat = jnp.tile(jnp.cos(ang), (1, 2))
    sn_lat = jnp.tile(jnp.sin(ang) * jnp.asarray(_ROPE_SIGN)[None, :], (1, 2))
    rope_cs = jnp.concatenate([cs_lat, jnp.ones((CTX_LEN, 2 * HEAD_DIM), F32)], axis=0)
    rope_sn = jnp.concatenate([sn_lat, jnp.zeros((CTX_LEN, 2 * HEAD_DIM), F32)], axis=0)
    ones_bd = jnp.asarray(
        np.kron(np.eye(GQA_Q_HEADS, dtype=np.float32), np.full((HEAD_DIM, HEAD_DIM), 1.0 / HEAD_DIM,
                                                               np.float32))).astype(BF16)

    w_in_p = jnp.concatenate([w_in[:, :, :OFF_C_Q], _group_major(w_in[:, :, OFF_C_Q:OFF_C_K], 2),
                              w_in[:, :, OFF_C_K:]], axis=-1).astype(BF16)
    w_out_p = jnp.concatenate([w_out[:, :NA_WIDTH + POOL_WIDTH],
                               _group_major(w_out[:, NA_WIDTH + POOL_WIDTH:], 1)], axis=1).astype(BF16)
    up1, dn1 = ffn1_up.astype(BF16), ffn1_down.astype(BF16)
    up2, dn2 = ffn2_up.astype(BF16), ffn2_down.astype(BF16)
    pool_bd = jnp.einsum('gh,lgcd->lgchd', jnp.eye(len(POOL_WINDOWS), dtype=F32), pool_w)
    pool_bd = pool_bd.reshape(depth, POOL_WIDTH, POOL_WIDTH).astype(BF16)
    pool_sc = pool_scale.reshape(depth, 1, POOL_WIDTH)
    qg = jnp.tile(q_norm_g, (1, GQA_Q_HEADS)).reshape(depth, 1, GQA_Q_WIDTH)
    kg = jnp.tile(k_norm_g, (1, GQA_KV_HEADS)).reshape(depth, 1, GQA_KV_WIDTH)
    na_tb = _na_bias_table(na_rpb)
    na_rm = jnp.asarray(_NA_ROW_MASK)

    h = jnp.concatenate([x, ctx], axis=1)
    for l in range(depth):
        last = l == depth - 1
        h = _ffn_call(h, l, mods, gains, 0, up1, dn1, mod_base=0)
        qa, ka, va, u, qc, kc, vc_t = _proj_call(h, l, mods, gains, w_in_p, rope_cs, rope_sn, qg, kg, ones_bd)
        oa = _na_call(qa, ka, va, l, na_tb, na_rm)
        ob = _pool_call(u, l, pool_bd, pool_sc)
        oc = _gqa_call(qc, kc, vc_t)
        h = _ffn_call(h, l, mods, gains, 2, up2, dn2, mod_base=6,
                      mix=(oa, ob, oc, w_out_p), final_g=final_g if last else None)
    return h
```

```python
import functools

import numpy as np
import jax
import jax.numpy as jnp
from jax import lax
from jax.experimental import pallas as pl
from jax.experimental.pallas import tpu as pltpu

D_MODEL = 1024
SEQ = 2048
CTX_LEN = 256
TOK = SEQ + CTX_LEN
GRID_W = 64
GRID_H = SEQ // GRID_W
HEAD_DIM = 64
N_MOD = 9
D_FF = 2816
EPS = 1e-6
NEG_INF = -1e30
NA_HEADS = 4
NA_WIN_H = 8
NA_WIN_W = 16
NA_WIDTH = NA_HEADS * HEAD_DIM
POOL_WINDOWS = (2, 4, 8, 16)
POOL_WIDTH = 256
GQA_Q_HEADS = 8
GQA_KV_HEADS = 2
GQA_GROUP = GQA_Q_HEADS // GQA_KV_HEADS
GQA_Q_WIDTH = GQA_Q_HEADS * HEAD_DIM
GQA_KV_WIDTH = GQA_KV_HEADS * HEAD_DIM
ROPE_THETA = 10000.0
OFF_A_Q = 0
OFF_A_K = OFF_A_Q + NA_WIDTH
OFF_A_V = OFF_A_K + NA_WIDTH
OFF_B_U = OFF_A_V + NA_WIDTH
OFF_C_Q = OFF_B_U + POOL_WIDTH
OFF_C_K = OFF_C_Q + GQA_Q_WIDTH
OFF_C_V = OFF_C_K + GQA_KV_WIDTH
D_IN = OFF_C_V + GQA_KV_WIDTH
D_MIX = NA_WIDTH + POOL_WIDTH + GQA_Q_WIDTH
QK_SCALE = HEAD_DIM ** -0.5
LOG2_E = float(np.log2(np.e))

TM = 256
LAT_TILES = SEQ // TM
STEP = 2 * TM
LAT_STEPS = SEQ // STEP
N_STEPS = LAT_STEPS + 1
NA_QROWS = TM // GRID_W
NA_KROWS = 12
NA_KEYS = NA_KROWS * GRID_W
NA_KPAIRS = NA_KROWS // 2
NA_DY = 2 * NA_WIN_H - 1
NA_DY_PAD = 4
NA_DY_SLOTS = NA_DY + 2 * NA_DY_PAD - 1
NA_PATTERN_GROUPS = (0, 1, 2, LAT_TILES - 1)
GQA_KEY_CHUNK = 768
FF_CHUNK = 256
ADA_TN = 1152
ADA_ROWS = 16
VMEM_LIMIT = 56 * 1024 * 1024

BF16 = jnp.bfloat16
F32 = jnp.float32


def _dot(a, b):
    return jnp.dot(a, b, preferred_element_type=F32)


def _dot_nt(a, b):
    return lax.dot_general(a, b, (((1,), (1,)), ((), ())), preferred_element_type=F32)


def _rms(x, g):
    ms = jnp.mean(x * x, axis=-1, keepdims=True)
    return x * lax.rsqrt(ms + EPS) * g


def _layer_block(l, shape, *, single_buffer=False):
    nd = len(shape)
    mode = dict(pipeline_mode=pl.Buffered(1)) if single_buffer else {}
    return pl.BlockSpec((None,) + tuple(shape), lambda *_: (l,) + (0,) * nd, **mode)


def _mod_block(l, bn):
    return pl.BlockSpec((None, None, N_MOD, D_MODEL),
                        lambda b, t: (l, b + (t // LAT_STEPS) * (bn - b), 0, 0))


def _gain_block(l, which):
    return pl.BlockSpec((None, None, 1, D_MODEL), lambda b, t: (l, which, 0, 0))


def _token_tile(width):
    return pl.BlockSpec((None, STEP, width), lambda b, t: (b, t, 0))


def _for_each_subtile(body, *, with_context=True):
    t = pl.program_id(1)
    subtiles = [pl.ds(s * TM, TM) for s in range(STEP // TM)]
    if not with_context:
        for rows in subtiles:
            body(rows, None)
        return

    @pl.when(t < LAT_STEPS)
    def _latent():
        for s, rows in enumerate(subtiles):
            body(rows, t * (STEP // TM) + s)

    @pl.when(t == LAT_STEPS)
    def _context():
        body(subtiles[0], None)


def _params(n_axes=2):
    return pltpu.CompilerParams(
        dimension_semantics=("arbitrary",) * n_axes, vmem_limit_bytes=VMEM_LIMIT)


def _ada_kernel(c_ref, w_ref, b_ref, o_ref):
    c = c_ref[...]
    s = c * jax.nn.sigmoid(c)
    o_ref[...] = jnp.dot(s, w_ref[...], preferred_element_type=F32,
                         precision=lax.Precision.HIGHEST) + b_ref[...]


def _ada_call(cc, w_ada, b_ada):
    depth = w_ada.shape[0]
    n = w_ada.shape[2]
    return pl.pallas_call(
        _ada_kernel,
        grid=(depth, n // ADA_TN),
        in_specs=[
            pl.BlockSpec((ADA_ROWS, D_MODEL), lambda l, j: (0, 0)),
            pl.BlockSpec((None, D_MODEL, ADA_TN), lambda l, j: (l, 0, j)),
            pl.BlockSpec((None, 1, ADA_TN), lambda l, j: (l, 0, j)),
        ],
        out_specs=pl.BlockSpec((None, ADA_ROWS, ADA_TN), lambda l, j: (l, 0, j)),
        out_shape=jax.ShapeDtypeStruct((depth, ADA_ROWS, n), F32),
        compiler_params=_params(),
        name="ada",
    )(cc, w_ada, b_ada.reshape(depth, 1, n))


def _ffn_kernel(*refs, mod_base, with_mix, final):
    it = iter(refs)
    h_ref = next(it)
    if with_mix:
        oa_ref, ob_ref, oc_ref = next(it), next(it), next(it)
    mod_ref, g_ref = next(it), next(it)
    if with_mix:
        wout_ref = next(it)
    wup_ref, wdn_ref = next(it), next(it)
    if final:
        fg_ref = next(it)
    o_ref, hid_ref = next(it), next(it)

    mod = mod_ref[...]

    def body(rows, _):
        x = h_ref[rows, :]
        if with_mix:
            mixed = jnp.concatenate([oa_ref[rows, :], ob_ref[rows, :], oc_ref[rows, :]], axis=-1)
            x = x + mod[5:6] * _dot(mixed, wout_ref[...])
        y = _rms(x, g_ref[...])
        xn = (y * (1.0 + mod[mod_base + 1:mod_base + 2]) + mod[mod_base:mod_base + 1]).astype(BF16)
        for c in range(D_FF // FF_CHUNK):
            lo = c * FF_CHUNK
            a = _dot(xn, wup_ref[:, lo:lo + FF_CHUNK])
            b = _dot(xn, wup_ref[:, D_FF + lo:D_FF + lo + FF_CHUNK])
            hid_ref[rows, lo:lo + FF_CHUNK] = (a * jax.nn.sigmoid(a) * b).astype(BF16)
        out = x + (0.5 * mod[mod_base + 2:mod_base + 3]) * _dot(hid_ref[rows, :], wdn_ref[...])
        if final:
            out = _rms(out, fg_ref[...])
        o_ref[rows, :] = out

    _for_each_subtile(body, with_context=not final)


def _ffn_call(h, l, mods, gains, which_gain, w_up, w_dn, *, mod_base, mix=None, final_g=None):
    bn = h.shape[0]
    with_mix = mix is not None
    final = final_g is not None
    n_steps = LAT_STEPS if final else N_STEPS
    args, specs = [h], [_token_tile(D_MODEL)]
    if with_mix:
        oa, ob, oc, w_out = mix
        args += [oa, ob, oc]
        specs += [_token_tile(NA_WIDTH), _token_tile(POOL_WIDTH), _token_tile(GQA_Q_WIDTH)]
    args += [mods, gains]
    specs += [_mod_block(l, bn), _gain_block(l, which_gain)]
    if with_mix:
        args.append(w_out)
        specs.append(_layer_block(l, (D_MIX, D_MODEL), single_buffer=True))
    args += [w_up, w_dn]
    specs += [_layer_block(l, (D_MODEL, 2 * D_FF), single_buffer=True),
              _layer_block(l, (D_FF, D_MODEL), single_buffer=True)]
    if final:
        args.append(final_g.reshape(1, D_MODEL))
        specs.append(pl.BlockSpec((1, D_MODEL), lambda b, t: (0, 0)))
    return pl.pallas_call(
        functools.partial(_ffn_kernel, mod_base=mod_base, with_mix=with_mix, final=final),
        grid=(bn, n_steps),
        in_specs=specs,
        out_specs=_token_tile(D_MODEL),
        out_shape=jax.ShapeDtypeStruct((bn, SEQ if final else TOK, D_MODEL), F32),
        scratch_shapes=[pltpu.VMEM((STEP, D_FF), BF16)],
        compiler_params=_params(),
        name="ffn_mix" if with_mix else "ffn",
    )(*args)


def _head_norm_rope(z, gain, cs, sn, ones_bd):
    width = z.shape[-1]
    zz = z * z
    hi = zz.astype(BF16)
    lo = (zz - hi.astype(F32)).astype(BF16)
    ms = _dot(hi, ones_bd) + _dot(lo, ones_bd)
    zn = z * lax.rsqrt(ms + EPS) * gain
    lane = lax.broadcasted_iota(jnp.int32, zn.shape, 1)
    partner = jnp.where((lane & 16) != 0,
                        pltpu.roll(zn, 16, axis=1), pltpu.roll(zn, width - 16, axis=1))
    reps = width // cs.shape[-1]
    if reps > 1:
        cs = jnp.concatenate([cs] * reps, axis=-1)
        sn = jnp.concatenate([sn] * reps, axis=-1)
    return zn * cs + partner * sn


def _proj_kernel(h_ref, mod_ref, g_ref, win_ref, cs_ref, sn_ref, qg_ref, kg_ref, bd_ref,
                 qa_ref, ka_ref, va_ref, u_ref, qc_ref, kc_ref, vct_ref):
    mod = mod_ref[...]
    bd = bd_ref[...]

    def body(rows, _):
        x = h_ref[rows, :]
        y = _rms(x, g_ref[...])
        a = (y * (1.0 + mod[4:5]) + mod[3:4]).astype(BF16)
        p = _dot(a, win_ref[...])
        qa_ref[rows, :] = (p[:, OFF_A_Q:OFF_A_K] * QK_SCALE).astype(BF16)
        ka_ref[rows, :] = p[:, OFF_A_K:OFF_A_V].astype(BF16)
        va_ref[rows, :] = p[:, OFF_A_V:OFF_B_U].astype(BF16)
        u_ref[rows, :] = p[:, OFF_B_U:OFF_C_Q]
        cs, sn = cs_ref[rows, :], sn_ref[rows, :]
        qc = _head_norm_rope(p[:, OFF_C_Q:OFF_C_K], qg_ref[...], cs, sn, bd)
        qc_ref[rows, :] = (qc * (QK_SCALE * LOG2_E)).astype(BF16)
        kc = _head_norm_rope(p[:, OFF_C_K:OFF_C_V], kg_ref[...], cs, sn,
                             bd[:GQA_KV_WIDTH, :GQA_KV_WIDTH])
        kc_ref[rows, :] = kc.astype(BF16)
        vct_ref[:, rows] = p[:, OFF_C_V:D_IN].T.astype(BF16)

    _for_each_subtile(body)


def _proj_call(h, l, mods, gains, w_in, rope_cs, rope_sn, qg, kg, ones_bd):
    bn = h.shape[0]
    const = lambda r, w: pl.BlockSpec((r, w), lambda b, t: (0, 0))
    out = lambda w, dt: jax.ShapeDtypeStruct((bn, TOK, w), dt)
    return pl.pallas_call(
        _proj_kernel,
        grid=(bn, N_STEPS),
        in_specs=[
            _token_tile(D_MODEL),
            _mod_block(l, bn),
            _gain_block(l, 1),
            _layer_block(l, (D_MODEL, D_IN), single_buffer=True),
            pl.BlockSpec((STEP, 2 * HEAD_DIM), lambda b, t: (t, 0)),
            pl.BlockSpec((STEP, 2 * HEAD_DIM), lambda b, t: (t, 0)),
            _layer_block(l, (1, GQA_Q_WIDTH)),
            _layer_block(l, (1, GQA_KV_WIDTH)),
            const(GQA_Q_WIDTH, GQA_Q_WIDTH),
        ],
        out_specs=[_token_tile(NA_WIDTH), _token_tile(NA_WIDTH), _token_tile(NA_WIDTH),
                   _token_tile(POOL_WIDTH), _token_tile(GQA_Q_WIDTH), _token_tile(GQA_KV_WIDTH),
                   pl.BlockSpec((None, GQA_KV_WIDTH, STEP), lambda b, t: (b, 0, t))],
        out_shape=[out(NA_WIDTH, BF16), out(NA_WIDTH, BF16), out(NA_WIDTH, BF16),
                   out(POOL_WIDTH, F32), out(GQA_Q_WIDTH, BF16), out(GQA_KV_WIDTH, BF16),
                   jax.ShapeDtypeStruct((bn, GQA_KV_WIDTH, TOK), BF16)],
        compiler_params=_params(),
        name="proj",
    )(h, mods, gains, w_in, rope_cs, rope_sn, qg, kg, ones_bd)


def _softmax_pv(scores, values):
    m = scores[0].max(axis=-1, keepdims=True)
    for s in scores[1:]:
        m = jnp.maximum(m, s.max(axis=-1, keepdims=True))
    den = None
    acc = None
    for s, v in zip(scores, values):
        e = jnp.exp(s - m)
        d = e.sum(axis=-1, keepdims=True)
        o = _dot(e.astype(BF16), v)
        den = d if den is None else den + d
        acc = o if acc is None else acc + o
    return acc / den


def _na_kernel(q_ref, k_ref, v_ref, tb_ref, rm_ref, o_ref):
    head_of_lane = lax.broadcasted_iota(jnp.int32, (1, NA_WIDTH), 1) // HEAD_DIM
    k_ctx = k_ref[SEQ:TOK, :]
    v_ctx = v_ref[SEQ:TOK, :]

    def context(rows):
        q = q_ref[rows, :]
        out = jnp.zeros((TM, NA_WIDTH), F32)
        for h in range(NA_HEADS):
            mine = head_of_lane == h
            qh = jnp.where(mine, q, jnp.zeros_like(q))
            out = jnp.where(mine, _softmax_pv([_dot_nt(qh, k_ctx)], [v_ctx]), out)
        o_ref[rows, :] = out.astype(BF16)

    def body(rows, g):
        if g is None:
            return context(rows)
        q = q_ref[rows, :]
        first_row = jnp.clip(g * NA_QROWS - NA_WIN_H // 2, 0, GRID_H - NA_KROWS)
        start = pl.multiple_of(first_row * GRID_W, GRID_W)
        k_lat = k_ref[pl.ds(start, NA_KEYS), :]
        v_lat = v_ref[pl.ds(start, NA_KEYS), :]
        pattern = jnp.minimum(g, 2) + (g == LAT_TILES - 1).astype(jnp.int32)
        slot0 = first_row - g * NA_QROWS + NA_WIN_H - 1 + NA_DY_PAD

        def scores(h):
            bias = jnp.concatenate([
                jnp.concatenate([
                    tb_ref[h, slot0 + 2 * m - qr]
                    + rm_ref[pattern, qr * NA_KPAIRS + m:qr * NA_KPAIRS + m + 1, :]
                    for m in range(NA_KPAIRS)], axis=-1)
                for qr in range(NA_QROWS)], axis=0)
            qh = jnp.where(head_of_lane == h, q, jnp.zeros_like(q))
            return [_dot_nt(qh, k_lat) + bias, _dot_nt(qh, k_ctx)]

        out = jnp.zeros((TM, NA_WIDTH), F32)
        s_next = scores(0)
        for h in range(NA_HEADS):
            s_cur = s_next
            if h + 1 < NA_HEADS:
                s_next = scores(h + 1)
            out = jnp.where(head_of_lane == h, _softmax_pv(s_cur, [v_lat, v_ctx]), out)
        o_ref[rows, :] = out.astype(BF16)

    _for_each_subtile(body)


def _na_call(qa, ka, va, l, tb, rm):
    bn = qa.shape[0]
    whole = pl.BlockSpec((None, TOK, NA_WIDTH), lambda b, t: (b, 0, 0))
    return pl.pallas_call(
        _na_kernel,
        grid=(bn, N_STEPS),
        in_specs=[
            _token_tile(NA_WIDTH), whole, whole,
            _layer_block(l, tb.shape[1:], single_buffer=True),
            pl.BlockSpec(rm.shape, lambda b, t: (0, 0, 0)),
        ],
        out_specs=_token_tile(NA_WIDTH),
        out_shape=jax.ShapeDtypeStruct((bn, TOK, NA_WIDTH), BF16),
        compiler_params=_params(),
        name="na",
    )(qa, ka, va, tb, rm)


def _na_static_tables():
    col = np.arange(GRID_W)
    win_c0 = np.clip(col - NA_WIN_W // 2, 0, GRID_W - NA_WIN_W)
    col_ok = (col[None, :] >= win_c0[:, None]) & (col[None, :] < win_c0[:, None] + NA_WIN_W)
    dx = np.clip(col[None, :] - col[:, None], -(NA_WIN_W - 1), NA_WIN_W - 1) + NA_WIN_W - 1
    sel_x = np.eye(2 * NA_WIN_W - 1, dtype=np.float32)[dx]
    row_mask = np.zeros((len(NA_PATTERN_GROUPS), NA_QROWS, NA_KPAIRS, 2, GRID_W), np.float32)
    for p, grp in enumerate(NA_PATTERN_GROUPS):
        first_row = int(np.clip(grp * NA_QROWS - NA_WIN_H // 2, 0, GRID_H - NA_KROWS))
        q_row = grp * NA_QROWS + np.arange(NA_QROWS)
        q_r0 = np.clip(q_row - NA_WIN_H // 2, 0, GRID_H - NA_WIN_H)
        k_row = first_row + np.arange(NA_KROWS)
        row_ok = (k_row[None, :] >= q_r0[:, None]) & (k_row[None, :] < q_r0[:, None] + NA_WIN_H)
        row_mask[p] = np.where(row_ok, 0.0, NEG_INF).reshape(NA_QROWS, NA_KPAIRS, 2, 1)
    row_mask = row_mask.reshape(len(NA_PATTERN_GROUPS), NA_QROWS * NA_KPAIRS, 2 * GRID_W)
    return sel_x, col_ok, row_mask


_NA_SEL_X, _NA_COL_OK, _NA_ROW_MASK = _na_static_tables()


def _na_bias_table(rpb):
    t = jnp.einsum('lhab,xcb->lhaxc', rpb, _NA_SEL_X, precision=lax.Precision.HIGHEST)
    t = jnp.where(_NA_COL_OK, t, NEG_INF)
    t = jnp.pad(t, ((0, 0), (0, 0), (NA_DY_PAD, NA_DY_PAD), (0, 0), (0, 0)))
    return jnp.concatenate([t[:, :, :-1], t[:, :, 1:]], axis=-1)


def _shift_rows(a, d):
    n = a.shape[0]
    row = lax.broadcasted_iota(jnp.int32, a.shape, 0)
    rolled = pltpu.roll(a, d % n, axis=0)
    ok = (row >= d) if d > 0 else (row < n + d)
    return jnp.where(ok, rolled, 0.0)


def _pool_segment(u, w_bd, scale):
    n = u.shape[0]
    row = lax.broadcasted_iota(jnp.int32, u.shape, 0)
    grp = lax.broadcasted_iota(jnp.int32, u.shape, 1) // (POOL_WIDTH // len(POOL_WINDOWS))
    trail, lead = u, u
    total = jnp.zeros_like(u)
    count = jnp.ones_like(u)
    k = 1
    for i, w in enumerate(POOL_WINDOWS):
        while k < w // 2:
            trail = trail + _shift_rows(trail, k)
            lead = lead + _shift_rows(lead, -k)
            k *= 2
        win = _shift_rows(trail, 1) + lead
        cnt = (jnp.minimum(row + w // 2, n) - jnp.maximum(row - w // 2, 0)).astype(F32)
        total = jnp.where(grp == i, win, total)
        count = jnp.where(grp == i, cnt, count)
    y = (total / count - u).astype(BF16)
    return (_dot(y, w_bd) * scale).astype(BF16)


def _pool_kernel(u_ref, w_ref, s_ref, o_ref):
    w_bd = w_ref[...]
    scale = s_ref[...]
    o_ref[:SEQ, :] = _pool_segment(u_ref[:SEQ, :], w_bd, scale)
    o_ref[SEQ:, :] = _pool_segment(u_ref[SEQ:, :], w_bd, scale)


def _pool_call(u, l, w_bd, scale):
    bn = u.shape[0]
    whole = pl.BlockSpec((None, TOK, POOL_WIDTH), lambda b: (b, 0, 0))
    return pl.pallas_call(
        _pool_kernel,
        grid=(bn,),
        in_specs=[whole, _layer_block(l, (POOL_WIDTH, POOL_WIDTH)), _layer_block(l, (1, POOL_WIDTH))],
        out_specs=whole,
        out_shape=jax.ShapeDtypeStruct((bn, TOK, POOL_WIDTH), BF16),
        compiler_params=_params(1),
        name="pool",
    )(u, w_bd, scale)


def _gqa_subtiles(q_ref, o_ref, subtiles, k_ref, vt_ref, key_chunks):
    half_of_lane = lax.broadcasted_iota(jnp.int32, (1, GQA_KV_WIDTH), 1) // HEAD_DIM
    stages = [(rows, j, lo, hi) for rows in subtiles for j in range(GQA_GROUP) for lo, hi in key_chunks]

    def scores_t(stage):
        rows, j, lo, hi = stage
        qb = q_ref[rows, j * GQA_KV_WIDTH:(j + 1) * GQA_KV_WIDTH]
        q2 = jnp.concatenate([jnp.where(half_of_lane == kv, qb, jnp.zeros_like(qb))
                              for kv in range(GQA_KV_HEADS)], axis=0)
        return _dot_nt(k_ref[lo:hi, :], q2)

    def merge(parts):
        m_all = functools.reduce(jnp.maximum, [m for m, _, _ in parts])
        weights = [jnp.exp2(m - m_all) for m, _, _ in parts]
        den = sum(w * d for w, (_, d, _) in zip(weights, parts))
        o_t = jnp.concatenate([
            sum(w[:, kv * TM:(kv + 1) * TM] * o[kv] for w, (_, _, o) in zip(weights, parts))
            / den[:, kv * TM:(kv + 1) * TM]
            for kv in range(GQA_KV_HEADS)], axis=0)
        return o_t.T.astype(BF16)

    parts = []
    s_next = scores_t(stages[0])
    for i, (rows, j, lo, hi) in enumerate(stages):
        s_t = s_next
        if i + 1 < len(stages):
            s_next = scores_t(stages[i + 1])
        m = s_t.max(axis=0, keepdims=True)
        e = jnp.exp2(s_t - m)
        den = e.sum(axis=0, keepdims=True)
        e = e.astype(BF16)
        o_t = [_dot(vt_ref[kv * HEAD_DIM:(kv + 1) * HEAD_DIM, lo:hi], e[:, kv * TM:(kv + 1) * TM])
               for kv in range(GQA_KV_HEADS)]
        parts.append((m, den, o_t))
        if len(parts) == len(key_chunks):
            o_ref[rows, j * GQA_KV_WIDTH:(j + 1) * GQA_KV_WIDTH] = merge(parts)
            parts = []


def _gqa_kernel(q_ref, k_ref, vt_ref, o_ref):
    t = pl.program_id(1)
    subtiles = [pl.ds(s * TM, TM) for s in range(STEP // TM)]

    @pl.when(t < LAT_STEPS)
    def _latent():
        chunks = [(lo, lo + GQA_KEY_CHUNK) for lo in range(0, TOK, GQA_KEY_CHUNK)]
        _gqa_subtiles(q_ref, o_ref, subtiles, k_ref, vt_ref, chunks)

    @pl.when(t == LAT_STEPS)
    def _context():
        _gqa_subtiles(q_ref, o_ref, subtiles[:1], k_ref, vt_ref, [(SEQ, TOK)])


def _gqa_call(qc, kc, vc_t):
    bn = qc.shape[0]
    return pl.pallas_call(
        _gqa_kernel,
        grid=(bn, N_STEPS),
        in_specs=[_token_tile(GQA_Q_WIDTH),
                  pl.BlockSpec((None, TOK, GQA_KV_WIDTH), lambda b, t: (b, 0, 0)),
                  pl.BlockSpec((None, GQA_KV_WIDTH, TOK), lambda b, t: (b, 0, 0))],
        out_specs=_token_tile(GQA_Q_WIDTH),
        out_shape=jax.ShapeDtypeStruct((bn, TOK, GQA_Q_WIDTH), BF16),
        compiler_params=_params(),
        name="gqa",
    )(qc, kc, vc_t)


def _rope_tables():
    pos = np.arange(SEQ)
    inv_freq = ROPE_THETA ** (-np.arange(0, HEAD_DIM // 2, 2, dtype=np.float32) / (HEAD_DIM // 2))
    ang_row = (pos // GRID_W).astype(np.float32)[:, None] * inv_freq[None, :].astype(np.float32)
    ang_col = (pos % GRID_W).astype(np.float32)[:, None] * inv_freq[None, :].astype(np.float32)
    ang = np.concatenate([ang_row, ang_row, ang_col, ang_col], axis=-1).astype(np.float32)
    sign = np.tile(np.repeat(np.array([-1.0, 1.0], np.float32), HEAD_DIM // 4), 2)
    return ang, sign


_ROPE_ANG, _ROPE_SIGN = _rope_tables()


def _group_major(a, axis):
    shape = a.shape
    a = a.reshape(shape[:axis] + (GQA_KV_HEADS, GQA_GROUP, HEAD_DIM) + shape[axis + 1:])
    return jnp.swapaxes(a, axis, axis + 1).reshape(shape)


def kernel(x, c, ctx, c_ctx, w_ada, b_ada, norm_g, ffn1_up, ffn1_down, ffn2_up, ffn2_down,
           w_in, w_out, na_rpb, pool_w, pool_scale, q_norm_g, k_norm_g, final_g):
    bn = x.shape[0]
    depth = w_ada.shape[0]
    assert x.shape == (bn, SEQ, D_MODEL) and ctx.shape == (bn, CTX_LEN, D_MODEL)
    assert bn < ADA_ROWS

    cc = jnp.zeros((ADA_ROWS, D_MODEL), F32).at[:bn].set(c).at[bn].set(c_ctx)
    mods = _ada_call(cc, w_ada, b_ada).reshape(depth, ADA_ROWS, N_MOD, D_MODEL)
    gains = norm_g.reshape(depth, 3, 1, D_MODEL)

    ang = jnp.asarray(_ROPE_ANG)
    cs_lat = jnp.tile(jnp.cos(ang), (1, 2))
    sn_lat = jnp.tile(jnp.sin(ang) * jnp.asarray(_ROPE_SIGN)[None, :], (1, 2))
    rope_cs = jnp.concatenate([cs_lat, jnp.ones((CTX_LEN, 2 * HEAD_DIM), F32)], axis=0)
    rope_sn = jnp.concatenate([sn_lat, jnp.zeros((CTX_LEN, 2 * HEAD_DIM), F32)], axis=0)
    ones_bd = jnp.asarray(
        np.kron(np.eye(GQA_Q_HEADS, dtype=np.float32), np.full((HEAD_DIM, HEAD_DIM), 1.0 / HEAD_DIM,
                                                               np.float32))).astype(BF16)

    w_in_p = jnp.concatenate([w_in[:, :, :OFF_C_Q], _group_major(w_in[:, :, OFF_C_Q:OFF_C_K], 2),
                              w_in[:, :, OFF_C_K:]], axis=-1).astype(BF16)
    w_out_p = jnp.concatenate([w_out[:, :NA_WIDTH + POOL_WIDTH],
                               _group_major(w_out[:, NA_WIDTH + POOL_WIDTH:], 1)], axis=1).astype(BF16)
    up1, dn1 = ffn1_up.astype(BF16), ffn1_down.astype(BF16)
    up2, dn2 = ffn2_up.astype(BF16), ffn2_down.astype(BF16)
    pool_bd = jnp.einsum('gh,lgcd->lgchd', jnp.eye(len(POOL_WINDOWS), dtype=F32), pool_w)
    pool_bd = pool_bd.reshape(depth, POOL_WIDTH, POOL_WIDTH).astype(BF16)
    pool_sc = pool_scale.reshape(depth, 1, POOL_WIDTH)
    qg = jnp.tile(q_norm_g, (1, GQA_Q_HEADS)).reshape(depth, 1, GQA_Q_WIDTH)
    kg = jnp.tile(k_norm_g, (1, GQA_KV_HEADS)).reshape(depth, 1, GQA_KV_WIDTH)
    na_tb = _na_bias_table(na_rpb)
    na_rm = jnp.asarray(_NA_ROW_MASK)

    h = jnp.concatenate([x, ctx], axis=1)
    for l in range(depth):
        last = l == depth - 1
        h = _ffn_call(h, l, mods, gains, 0, up1, dn1, mod_base=0)
        qa, ka, va, u, qc, kc, vc_t = _proj_call(h, l, mods, gains, w_in_p, rope_cs, rope_sn, qg, kg, ones_bd)
        oa = _na_call(qa, ka, va, l, na_tb, na_rm)
        ob = _pool_call(u, l, pool_bd, pool_sc)
        oc = _gqa_call(qc, kc, vc_t)
        h = _ffn_call(h, l, mods, gains, 2, up2, dn2, mod_base=6,
                      mix=(oa, ob, oc, w_out_p), final_g=final_g if last else None)
    return h
```

```python
import functools

import numpy as np
import jax
import jax.numpy as jnp
from jax import lax
from jax.experimental import pallas as pl
from jax.experimental.pallas import tpu as pltpu

D_MODEL = 1024
SEQ = 2048
CTX_LEN = 256
TOK = SEQ + CTX_LEN
GRID_W = 64
GRID_H = SEQ // GRID_W
HEAD_DIM = 64
N_MOD = 9
D_FF = 2816
EPS = 1e-6
NEG_INF = -1e30
NA_HEADS = 4
NA_WIN_H = 8
NA_WIN_W = 16
NA_WIDTH = NA_HEADS * HEAD_DIM
POOL_WINDOWS = (2, 4, 8, 16)
POOL_WIDTH = 256
GQA_Q_HEADS = 8
GQA_KV_HEADS = 2
GQA_GROUP = GQA_Q_HEADS // GQA_KV_HEADS
GQA_Q_WIDTH = GQA_Q_HEADS * HEAD_DIM
GQA_KV_WIDTH = GQA_KV_HEADS * HEAD_DIM
ROPE_THETA = 10000.0
OFF_A_Q = 0
OFF_A_K = OFF_A_Q + NA_WIDTH
OFF_A_V = OFF_A_K + NA_WIDTH
OFF_B_U = OFF_A_V + NA_WIDTH
OFF_C_Q = OFF_B_U + POOL_WIDTH
OFF_C_K = OFF_C_Q + GQA_Q_WIDTH
OFF_C_V = OFF_C_K + GQA_KV_WIDTH
D_IN = OFF_C_V + GQA_KV_WIDTH
D_MIX = NA_WIDTH + POOL_WIDTH + GQA_Q_WIDTH
QK_SCALE = HEAD_DIM ** -0.5
LOG2_E = float(np.log2(np.e))

TM = 256
LAT_TILES = SEQ // TM
STEP = 2 * TM
LAT_STEPS = SEQ // STEP
N_STEPS = LAT_STEPS + 1
NA_QROWS = TM // GRID_W
NA_KROWS = 12
NA_KEYS = NA_KROWS * GRID_W
NA_KPAIRS = NA_KROWS // 2
NA_DY = 2 * NA_WIN_H - 1
NA_DY_PAD = 4
NA_DY_SLOTS = NA_DY + 2 * NA_DY_PAD - 1
NA_PATTERN_GROUPS = (0, 1, 2, LAT_TILES - 1)
GQA_KEY_CHUNK = 768
FF_CHUNK = 256
ADA_TN = 1152
ADA_ROWS = 16
VMEM_LIMIT = 56 * 1024 * 1024

BF16 = jnp.bfloat16
F32 = jnp.float32


def _dot(a, b):
    return jnp.dot(a, b, preferred_element_type=F32)


def _dot_nt(a, b):
    return lax.dot_general(a, b, (((1,), (1,)), ((), ())), preferred_element_type=F32)


def _rms(x, g):
    ms = jnp.mean(x * x, axis=-1, keepdims=True)
    return x * lax.rsqrt(ms + EPS) * g


def _layer_block(l, shape, *, single_buffer=False):
    nd = len(shape)
    mode = dict(pipeline_mode=pl.Buffered(1)) if single_buffer else {}
    return pl.BlockSpec((None,) + tuple(shape), lambda *_: (l,) + (0,) * nd, **mode)


def _mod_block(l, bn):
    return pl.BlockSpec((None, None, N_MOD, D_MODEL),
                        lambda b, t: (l, b + (t // LAT_STEPS) * (bn - b), 0, 0))


def _gain_block(l, which):
    return pl.BlockSpec((None, None, 1, D_MODEL), lambda b, t: (l, which, 0, 0))


def _token_tile(width):
    return pl.BlockSpec((None, STEP, width), lambda b, t: (b, t, 0))


def _for_each_subtile(body, *, with_context=True):
    t = pl.program_id(1)
    subtiles = [pl.ds(s * TM, TM) for s in range(STEP // TM)]
    if not with_context:
        for rows in subtiles:
            body(rows, None)
        return

    @pl.when(t < LAT_STEPS)
    def _latent():
        for s, rows in enumerate(subtiles):
            body(rows, t * (STEP // TM) + s)

    @pl.when(t == LAT_STEPS)
    def _context():
        body(subtiles[0], None)


def _params(n_axes=2):
    return pltpu.CompilerParams(
        dimension_semantics=("arbitrary",) * n_axes, vmem_limit_bytes=VMEM_LIMIT)


def _ada_kernel(c_ref, w_ref, b_ref, o_ref):
    c = c_ref[...]
    s = c * jax.nn.sigmoid(c)
    o_ref[...] = jnp.dot(s, w_ref[...], preferred_element_type=F32,
                         precision=lax.Precision.HIGHEST) + b_ref[...]


def _ada_call(cc, w_ada, b_ada):
    depth = w_ada.shape[0]
    n = w_ada.shape[2]
    return pl.pallas_call(
        _ada_kernel,
        grid=(depth, n // ADA_TN),
        in_specs=[
            pl.BlockSpec((ADA_ROWS, D_MODEL), lambda l, j: (0, 0)),
            pl.BlockSpec((None, D_MODEL, ADA_TN), lambda l, j: (l, 0, j)),
            pl.BlockSpec((None, 1, ADA_TN), lambda l, j: (l, 0, j)),
        ],
        out_specs=pl.BlockSpec((None, ADA_ROWS, ADA_TN), lambda l, j: (l, 0, j)),
        out_shape=jax.ShapeDtypeStruct((depth, ADA_ROWS, n), F32),
        compiler_params=_params(),
        name="ada",
    )(cc, w_ada, b_ada.reshape(depth, 1, n))


def _ffn_kernel(*refs, mod_base, split_input, with_mix, with_proj, final):
    it = iter(refs)
    h_ref = next(it)
    if split_input:
        ctx_ref = next(it)
    if with_mix:
        oa_ref, ob_ref, oc_ref = next(it), next(it), next(it)
    mod_ref, g_ref = next(it), next(it)
    if with_mix:
        wout_ref = next(it)
    wup_ref, wdn_ref = next(it), next(it)
    if final:
        fg_ref = next(it)
    if with_proj:
        proj_in = [next(it) for _ in range(7)]
    o_ref = next(it)
    if with_proj:
        proj_out = [next(it) for _ in range(7)]
    hid_ref = next(it)

    mod = mod_ref[...]

    def body(rows, lat_tile):
        if split_input and lat_tile is None:
            x = ctx_ref[...]
        else:
            x = h_ref[rows, :]
        if with_mix:
            mixed = jnp.concatenate([oa_ref[rows, :], ob_ref[rows, :], oc_ref[rows, :]], axis=-1)
            x = x + mod[5:6] * _dot(mixed, wout_ref[...])
        y = _rms(x, g_ref[...])
        xn = (y * (1.0 + mod[mod_base + 1:mod_base + 2]) + mod[mod_base:mod_base + 1]).astype(BF16)
        for c in range(D_FF // FF_CHUNK):
            lo = c * FF_CHUNK
            a = _dot(xn, wup_ref[:, lo:lo + FF_CHUNK])
            b = _dot(xn, wup_ref[:, D_FF + lo:D_FF + lo + FF_CHUNK])
            hid_ref[rows, lo:lo + FF_CHUNK] = (a * jax.nn.sigmoid(a) * b).astype(BF16)
        out = x + (0.5 * mod[mod_base + 2:mod_base + 3]) * _dot(hid_ref[rows, :], wdn_ref[...])
        if final:
            out = _rms(out, fg_ref[...])
        o_ref[rows, :] = out
        if with_proj:
            _proj_rows(out, rows, mod, *proj_in, *proj_out)

    _for_each_subtile(body, with_context=not final)


def _ffn_call(h, l, mods, gains, which_gain, w_up, w_dn, *, mod_base, mix=None, proj=None,
              final_g=None):
    split_input = isinstance(h, tuple)
    bn = (h[0] if split_input else h).shape[0]
    with_mix = mix is not None
    with_proj = proj is not None
    final = final_g is not None
    n_steps = LAT_STEPS if final else N_STEPS
    if split_input:
        args = list(h)
        specs = [pl.BlockSpec((None, STEP, D_MODEL), lambda b, t: (b, jnp.minimum(t, LAT_STEPS - 1), 0)),
                 pl.BlockSpec((None, CTX_LEN, D_MODEL), lambda b, t: (b, 0, 0))]
    else:
        args, specs = [h], [_token_tile(D_MODEL)]
    if with_mix:
        oa, ob, oc, w_out = mix
        args += [oa, ob, oc]
        specs += [_token_tile(NA_WIDTH), _token_tile(POOL_WIDTH), _token_tile(GQA_Q_WIDTH)]
    args += [mods, gains]
    specs += [_mod_block(l, bn), _gain_block(l, which_gain)]
    if with_mix:
        args.append(w_out)
        specs.append(_layer_block(l, (D_MIX, D_MODEL), single_buffer=True))
    args += [w_up, w_dn]
    specs += [_layer_block(l, (D_MODEL, 2 * D_FF), single_buffer=True),
              _layer_block(l, (D_FF, D_MODEL), single_buffer=True)]
    if final:
        args.append(final_g.reshape(1, D_MODEL))
        specs.append(pl.BlockSpec((1, D_MODEL), lambda b, t: (0, 0)))
    out_specs = [_token_tile(D_MODEL)]
    out_shape = [jax.ShapeDtypeStruct((bn, SEQ if final else TOK, D_MODEL), F32)]
    if with_proj:
        w_in, rope_cs, rope_sn, qg, kg, ones_bd = proj
        args += [gains, w_in, rope_cs, rope_sn, qg, kg, ones_bd]
        specs += [
            _gain_block(l, 1),
            _layer_block(l, (D_MODEL, D_IN), single_buffer=True),
            pl.BlockSpec((STEP, 2 * HEAD_DIM), lambda b, t: (t, 0)),
            pl.BlockSpec((STEP, 2 * HEAD_DIM), lambda b, t: (t, 0)),
            _layer_block(l, (1, GQA_Q_WIDTH)),
            _layer_block(l, (1, GQA_KV_WIDTH)),
            pl.BlockSpec((GQA_Q_WIDTH, GQA_Q_WIDTH), lambda b, t: (0, 0)),
        ]
        rows_of = lambda w, dt: jax.ShapeDtypeStruct((bn, TOK, w), dt)
        out_specs += [_token_tile(NA_WIDTH), _token_tile(NA_WIDTH), _token_tile(NA_WIDTH),
                      _token_tile(POOL_WIDTH), _token_tile(GQA_Q_WIDTH), _token_tile(GQA_KV_WIDTH),
                      pl.BlockSpec((None, GQA_KV_WIDTH, STEP), lambda b, t: (b, 0, t))]
        out_shape += [rows_of(NA_WIDTH, BF16), rows_of(NA_WIDTH, BF16), rows_of(NA_WIDTH, BF16),
                      rows_of(POOL_WIDTH, F32), rows_of(GQA_Q_WIDTH, BF16), rows_of(GQA_KV_WIDTH, BF16),
                      jax.ShapeDtypeStruct((bn, GQA_KV_WIDTH, TOK), BF16)]
    outs = pl.pallas_call(
        functools.partial(_ffn_kernel, mod_base=mod_base, split_input=split_input, with_mix=with_mix,
                          with_proj=with_proj, final=final),
        grid=(bn, n_steps),
        in_specs=specs,
        out_specs=out_specs,
        out_shape=out_shape,
        scratch_shapes=[pltpu.VMEM((STEP, D_FF), BF16)],
        compiler_params=_params(),
        name="ffn_mix" if with_mix else "ffn_proj" if with_proj else "ffn",
    )(*args)
    return outs if with_proj else outs[0]


def _head_norm_rope(z, gain, cs, sn, ones_bd):
    width = z.shape[-1]
    zz = z * z
    hi = zz.astype(BF16)
    lo = (zz - hi.astype(F32)).astype(BF16)
    ms = _dot(hi, ones_bd) + _dot(lo, ones_bd)
    zn = z * lax.rsqrt(ms + EPS) * gain
    lane = lax.broadcasted_iota(jnp.int32, zn.shape, 1)
    partner = jnp.where((lane & 16) != 0,
                        pltpu.roll(zn, 16, axis=1), pltpu.roll(zn, width - 16, axis=1))
    reps = width // cs.shape[-1]
    if reps > 1:
        cs = jnp.concatenate([cs] * reps, axis=-1)
        sn = jnp.concatenate([sn] * reps, axis=-1)
    return zn * cs + partner * sn


def _proj_rows(x, rows, mod, g_ref, win_ref, cs_ref, sn_ref, qg_ref, kg_ref, bd_ref,
               qa_ref, ka_ref, va_ref, u_ref, qc_ref, kc_ref, vct_ref):
    bd = bd_ref[...]
    y = _rms(x, g_ref[...])
    a = (y * (1.0 + mod[4:5]) + mod[3:4]).astype(BF16)
    p = _dot(a, win_ref[...])
    qa_ref[rows, :] = (p[:, OFF_A_Q:OFF_A_K] * QK_SCALE).astype(BF16)
    ka_ref[rows, :] = p[:, OFF_A_K:OFF_A_V].astype(BF16)
    va_ref[rows, :] = p[:, OFF_A_V:OFF_B_U].astype(BF16)
    u_ref[rows, :] = p[:, OFF_B_U:OFF_C_Q]
    cs, sn = cs_ref[rows, :], sn_ref[rows, :]
    qc = _head_norm_rope(p[:, OFF_C_Q:OFF_C_K], qg_ref[...], cs, sn, bd)
    qc_ref[rows, :] = (qc * (QK_SCALE * LOG2_E)).astype(BF16)
    kc = _head_norm_rope(p[:, OFF_C_K:OFF_C_V], kg_ref[...], cs, sn,
                         bd[:GQA_KV_WIDTH, :GQA_KV_WIDTH])
    kc_ref[rows, :] = kc.astype(BF16)
    vct_ref[:, rows] = p[:, OFF_C_V:D_IN].T.astype(BF16)


def _softmax_pv(scores, values):
    m = scores[0].max(axis=-1, keepdims=True)
    for s in scores[1:]:
        m = jnp.maximum(m, s.max(axis=-1, keepdims=True))
    den = None
    acc = None
    for s, v in zip(scores, values):
        e = jnp.exp(s - m)
        d = e.sum(axis=-1, keepdims=True)
        o = _dot(e.astype(BF16), v)
        den = d if den is None else den + d
        acc = o if acc is None else acc + o
    return acc / den


def _na_kernel(q_ref, k_ref, v_ref, tb_ref, rm_ref, o_ref):
    head_of_lane = lax.broadcasted_iota(jnp.int32, (1, NA_WIDTH), 1) // HEAD_DIM
    k_ctx = k_ref[SEQ:TOK, :]
    v_ctx = v_ref[SEQ:TOK, :]

    def context(rows):
        q = q_ref[rows, :]
        out = jnp.zeros((TM, NA_WIDTH), F32)
        for h in range(NA_HEADS):
            mine = head_of_lane == h
            qh = jnp.where(mine, q, jnp.zeros_like(q))
            out = jnp.where(mine, _softmax_pv([_dot_nt(qh, k_ctx)], [v_ctx]), out)
        o_ref[rows, :] = out.astype(BF16)

    def body(rows, g):
        if g is None:
            return context(rows)
        q = q_ref[rows, :]
        first_row = jnp.clip(g * NA_QROWS - NA_WIN_H // 2, 0, GRID_H - NA_KROWS)
        start = pl.multiple_of(first_row * GRID_W, GRID_W)
        k_lat = k_ref[pl.ds(start, NA_KEYS), :]
        v_lat = v_ref[pl.ds(start, NA_KEYS), :]
        pattern = jnp.minimum(g, 2) + (g == LAT_TILES - 1).astype(jnp.int32)
        slot0 = first_row - g * NA_QROWS + NA_WIN_H - 1 + NA_DY_PAD

        def scores(h):
            bias = jnp.concatenate([
                jnp.concatenate([
                    tb_ref[h, slot0 + 2 * m - qr]
                    + rm_ref[pattern, qr * NA_KPAIRS + m:qr * NA_KPAIRS + m + 1, :]
                    for m in range(NA_KPAIRS)], axis=-1)
                for qr in range(NA_QROWS)], axis=0)
            qh = jnp.where(head_of_lane == h, q, jnp.zeros_like(q))
            return [_dot_nt(qh, k_lat) + bias, _dot_nt(qh, k_ctx)]

        out = jnp.zeros((TM, NA_WIDTH), F32)
        s_next = scores(0)
        for h in range(NA_HEADS):
            s_cur = s_next
            if h + 1 < NA_HEADS:
                s_next = scores(h + 1)
            out = jnp.where(head_of_lane == h, _softmax_pv(s_cur, [v_lat, v_ctx]), out)
        o_ref[rows, :] = out.astype(BF16)

    _for_each_subtile(body)


def _na_call(qa, ka, va, l, tb, rm):
    bn = qa.shape[0]
    whole = pl.BlockSpec((None, TOK, NA_WIDTH), lambda b, t: (b, 0, 0))
    return pl.pallas_call(
        _na_kernel,
        grid=(bn, N_STEPS),
        in_specs=[
            _token_tile(NA_WIDTH), whole, whole,
            _layer_block(l, tb.shape[1:], single_buffer=True),
            pl.BlockSpec(rm.shape, lambda b, t: (0, 0, 0)),
        ],
        out_specs=_token_tile(NA_WIDTH),
        out_shape=jax.ShapeDtypeStruct((bn, TOK, NA_WIDTH), BF16),
        compiler_params=_params(),
        name="na",
    )(qa, ka, va, tb, rm)


def _na_static_tables():
    col = np.arange(GRID_W)
    win_c0 = np.clip(col - NA_WIN_W // 2, 0, GRID_W - NA_WIN_W)
    col_ok = (col[None, :] >= win_c0[:, None]) & (col[None, :] < win_c0[:, None] + NA_WIN_W)
    dx = np.clip(col[None, :] - col[:, None], -(NA_WIN_W - 1), NA_WIN_W - 1) + NA_WIN_W - 1
    sel_x = np.eye(2 * NA_WIN_W - 1, dtype=np.float32)[dx]
    row_mask = np.zeros((len(NA_PATTERN_GROUPS), NA_QROWS, NA_KPAIRS, 2, GRID_W), np.float32)
    for p, grp in enumerate(NA_PATTERN_GROUPS):
        first_row = int(np.clip(grp * NA_QROWS - NA_WIN_H // 2, 0, GRID_H - NA_KROWS))
        q_row = grp * NA_QROWS + np.arange(NA_QROWS)
        q_r0 = np.clip(q_row - NA_WIN_H // 2, 0, GRID_H - NA_WIN_H)
        k_row = first_row + np.arange(NA_KROWS)
        row_ok = (k_row[None, :] >= q_r0[:, None]) & (k_row[None, :] < q_r0[:, None] + NA_WIN_H)
        row_mask[p] = np.where(row_ok, 0.0, NEG_INF).reshape(NA_QROWS, NA_KPAIRS, 2, 1)
    row_mask = row_mask.reshape(len(NA_PATTERN_GROUPS), NA_QROWS * NA_KPAIRS, 2 * GRID_W)
    return sel_x, col_ok, row_mask


_NA_SEL_X, _NA_COL_OK, _NA_ROW_MASK = _na_static_tables()


def _na_bias_table(rpb):
    t = jnp.einsum('lhab,xcb->lhaxc', rpb, _NA_SEL_X, precision=lax.Precision.HIGHEST)
    t = jnp.where(_NA_COL_OK, t, NEG_INF)
    t = jnp.pad(t, ((0, 0), (0, 0), (NA_DY_PAD, NA_DY_PAD), (0, 0), (0, 0)))
    return jnp.concatenate([t[:, :, :-1], t[:, :, 1:]], axis=-1)


def _shift_rows(a, d):
    n = a.shape[0]
    row = lax.broadcasted_iota(jnp.int32, a.shape, 0)
    rolled = pltpu.roll(a, d % n, axis=0)
    ok = (row >= d) if d > 0 else (row < n + d)
    return jnp.where(ok, rolled, 0.0)


def _pool_segment(u, w_bd, scale):
    n = u.shape[0]
    row = lax.broadcasted_iota(jnp.int32, u.shape, 0)
    grp = lax.broadcasted_iota(jnp.int32, u.shape, 1) // (POOL_WIDTH // len(POOL_WINDOWS))
    trail, lead = u, u
    total = jnp.zeros_like(u)
    count = jnp.ones_like(u)
    k = 1
    for i, w in enumerate(POOL_WINDOWS):
        while k < w // 2:
            trail = trail + _shift_rows(trail, k)
            lead = lead + _shift_rows(lead, -k)
            k *= 2
        win = _shift_rows(trail, 1) + lead
        cnt = (jnp.minimum(row + w // 2, n) - jnp.maximum(row - w // 2, 0)).astype(F32)
        total = jnp.where(grp == i, win, total)
        count = jnp.where(grp == i, cnt, count)
    y = (total / count - u).astype(BF16)
    return (_dot(y, w_bd) * scale).astype(BF16)


def _pool_kernel(u_ref, w_ref, s_ref, o_ref):
    w_bd = w_ref[...]
    scale = s_ref[...]
    o_ref[:SEQ, :] = _pool_segment(u_ref[:SEQ, :], w_bd, scale)
    o_ref[SEQ:, :] = _pool_segment(u_ref[SEQ:, :], w_bd, scale)


def _pool_call(u, l, w_bd, scale):
    bn = u.shape[0]
    whole = pl.BlockSpec((None, TOK, POOL_WIDTH), lambda b: (b, 0, 0))
    return pl.pallas_call(
        _pool_kernel,
        grid=(bn,),
        in_specs=[whole, _layer_block(l, (POOL_WIDTH, POOL_WIDTH)), _layer_block(l, (1, POOL_WIDTH))],
        out_specs=whole,
        out_shape=jax.ShapeDtypeStruct((bn, TOK, POOL_WIDTH), BF16),
        compiler_params=_params(1),
        name="pool",
    )(u, w_bd, scale)


def _gqa_subtiles(q_ref, o_ref, subtiles, k_ref, vt_ref, key_chunks):
    half_of_lane = lax.broadcasted_iota(jnp.int32, (1, GQA_KV_WIDTH), 1) // HEAD_DIM
    stages = [(rows, j, lo, hi) for rows in subtiles for j in range(GQA_GROUP) for lo, hi in key_chunks]

    def scores_t(stage):
        rows, j, lo, hi = stage
        qb = q_ref[rows, j * GQA_KV_WIDTH:(j + 1) * GQA_KV_WIDTH]
        q2 = jnp.concatenate([jnp.where(half_of_lane == kv, qb, jnp.zeros_like(qb))
                              for kv in range(GQA_KV_HEADS)], axis=0)
        return _dot_nt(k_ref[lo:hi, :], q2)

    def merge(parts):
        m_all = functools.reduce(jnp.maximum, [m for m, _, _ in parts])
        weights = [jnp.exp2(m - m_all) for m, _, _ in parts]
        den = sum(w * d for w, (_, d, _) in zip(weights, parts))
        o_t = jnp.concatenate([
            sum(w[:, kv * TM:(kv + 1) * TM] * o[kv] for w, (_, _, o) in zip(weights, parts))
            / den[:, kv * TM:(kv + 1) * TM]
            for kv in range(GQA_KV_HEADS)], axis=0)
        return o_t.T.astype(BF16)

    parts = []
    s_next = scores_t(stages[0])
    for i, (rows, j, lo, hi) in enumerate(stages):
        s_t = s_next
        if i + 1 < len(stages):
            s_next = scores_t(stages[i + 1])
        m = s_t.max(axis=0, keepdims=True)
        e = jnp.exp2(s_t - m)
        den = e.sum(axis=0, keepdims=True)
        e = e.astype(BF16)
        o_t = [_dot(vt_ref[kv * HEAD_DIM:(kv + 1) * HEAD_DIM, lo:hi], e[:, kv * TM:(kv + 1) * TM])
               for kv in range(GQA_KV_HEADS)]
        parts.append((m, den, o_t))
        if len(parts) == len(key_chunks):
            o_ref[rows, j * GQA_KV_WIDTH:(j + 1) * GQA_KV_WIDTH] = merge(parts)
            parts = []


def _gqa_kernel(q_ref, k_ref, vt_ref, o_ref):
    t = pl.program_id(1)
    subtiles = [pl.ds(0, TM)]

    @pl.when(t < LAT_TILES)
    def _latent():
        chunks = [(lo, lo + GQA_KEY_CHUNK) for lo in range(0, TOK, GQA_KEY_CHUNK)]
        _gqa_subtiles(q_ref, o_ref, subtiles, k_ref, vt_ref, chunks)

    @pl.when(t == LAT_TILES)
    def _context():
        _gqa_subtiles(q_ref, o_ref, subtiles, k_ref, vt_ref, [(SEQ, TOK)])


def _gqa_call(qc, kc, vc_t):
    bn = qc.shape[0]
    subtile = pl.BlockSpec((None, TM, GQA_Q_WIDTH), lambda b, t: (b, t, 0))
    return pl.pallas_call(
        _gqa_kernel,
        grid=(bn, TOK // TM),
        in_specs=[subtile,
                  pl.BlockSpec((None, TOK, GQA_KV_WIDTH), lambda b, t: (b, 0, 0)),
                  pl.BlockSpec((None, GQA_KV_WIDTH, TOK), lambda b, t: (b, 0, 0))],
        out_specs=subtile,
        out_shape=jax.ShapeDtypeStruct((bn, TOK, GQA_Q_WIDTH), BF16),
        compiler_params=_params(),
        name="gqa",
    )(qc, kc, vc_t)


def _rope_tables():
    pos = np.arange(SEQ)
    inv_freq = ROPE_THETA ** (-np.arange(0, HEAD_DIM // 2, 2, dtype=np.float32) / (HEAD_DIM // 2))
    ang_row = (pos // GRID_W).astype(np.float32)[:, None] * inv_freq[None, :].astype(np.float32)
    ang_col = (pos % GRID_W).astype(np.float32)[:, None] * inv_freq[None, :].astype(np.float32)
    ang = np.concatenate([ang_row, ang_row, ang_col, ang_col], axis=-1).astype(np.float32)
    sign = np.tile(np.repeat(np.array([-1.0, 1.0], np.float32), HEAD_DIM // 4), 2)
    return ang, sign


_ROPE_ANG, _ROPE_SIGN = _rope_tables()


def _group_major(a, axis):
    shape = a.shape
    a = a.reshape(shape[:axis] + (GQA_KV_HEADS, GQA_GROUP, HEAD_DIM) + shape[axis + 1:])
    return jnp.swapaxes(a, axis, axis + 1).reshape(shape)


def kernel(x, c, ctx, c_ctx, w_ada, b_ada, norm_g, ffn1_up, ffn1_down, ffn2_up, ffn2_down,
           w_in, w_out, na_rpb, pool_w, pool_scale, q_norm_g, k_norm_g, final_g):
    bn = x.shape[0]
    depth = w_ada.shape[0]
    assert x.shape == (bn, SEQ, D_MODEL) and ctx.shape == (bn, CTX_LEN, D_MODEL)
    assert bn < ADA_ROWS

    cc = jnp.zeros((ADA_ROWS, D_MODEL), F32).at[:bn].set(c).at[bn].set(c_ctx)
    mods = _ada_call(cc, w_ada, b_ada).reshape(depth, ADA_ROWS, N_MOD, D_MODEL)
    gains = norm_g.reshape(depth, 3, 1, D_MODEL)

    ang = jnp.asarray(_ROPE_ANG)
    cs_lat = jnp.tile(jnp.cos(ang), (1, 2))
    sn_lat = jnp.tile(jnp.sin(ang) * jnp.asarray(_ROPE_SIGN)[None, :], (1, 2))
    rope_cs = jnp.concatenate([cs_lat, jnp.ones((CTX_LEN, 2 * HEAD_DIM), F32)], axis=0)
    rope_sn = jnp.concatenate([sn_lat, jnp.zeros((CTX_LEN, 2 * HEAD_DIM), F32)], axis=0)
    ones_bd = jnp.asarray(
        np.kron(np.eye(GQA_Q_HEADS, dtype=np.float32), np.full((HEAD_DIM, HEAD_DIM), 1.0 / HEAD_DIM,
                                                               np.float32))).astype(BF16)

    w_in_p = jnp.concatenate([w_in[:, :, :OFF_C_Q], _group_major(w_in[:, :, OFF_C_Q:OFF_C_K], 2),
                              w_in[:, :, OFF_C_K:]], axis=-1).astype(BF16)
    w_out_p = jnp.concatenate([w_out[:, :NA_WIDTH + POOL_WIDTH],
                               _group_major(w_out[:, NA_WIDTH + POOL_WIDTH:], 1)], axis=1).astype(BF16)
    up1, dn1 = ffn1_up.astype(BF16), ffn1_down.astype(BF16)
    up2, dn2 = ffn2_up.astype(BF16), ffn2_down.astype(BF16)
    pool_bd = jnp.einsum('gh,lgcd->lgchd', jnp.eye(len(POOL_WINDOWS), dtype=F32), pool_w)
    pool_bd = pool_bd.reshape(depth, POOL_WIDTH, POOL_WIDTH).astype(BF16)
    pool_sc = pool_scale.reshape(depth, 1, POOL_WIDTH)
    qg = jnp.tile(q_norm_g, (1, GQA_Q_HEADS)).reshape(depth, 1, GQA_Q_WIDTH)
    kg = jnp.tile(k_norm_g, (1, GQA_KV_HEADS)).reshape(depth, 1, GQA_KV_WIDTH)
    na_tb = _na_bias_table(na_rpb)
    na_rm = jnp.asarray(_NA_ROW_MASK)

    h = (x, ctx)
    for l in range(depth):
        last = l == depth - 1
        h, qa, ka, va, u, qc, kc, vc_t = _ffn_call(
            h, l, mods, gains, 0, up1, dn1, mod_base=0,
            proj=(w_in_p, rope_cs, rope_sn, qg, kg, ones_bd))
        oa = _na_call(qa, ka, va, l, na_tb, na_rm)
        ob = _pool_call(u, l, pool_bd, pool_sc)
        oc = _gqa_call(qc, kc, vc_t)
        h = _ffn_call(h, l, mods, gains, 2, up2, dn2, mod_base=6,
                      mix=(oa, ob, oc, w_out_p), final_g=final_g if last else None)
    return h
```

```python
import functools

import numpy as np
import jax
import jax.numpy as jnp
from jax import lax
from jax.experimental import pallas as pl
from jax.experimental.pallas import tpu as pltpu

D_MODEL = 1024
SEQ = 2048
CTX_LEN = 256
TOK = SEQ + CTX_LEN
GRID_W = 64
GRID_H = SEQ // GRID_W
HEAD_DIM = 64
N_MOD = 9
D_FF = 2816
EPS = 1e-6
NEG_INF = -1e30
NA_HEADS = 4
NA_WIN_H = 8
NA_WIN_W = 16
NA_WIDTH = NA_HEADS * HEAD_DIM
POOL_WINDOWS = (2, 4, 8, 16)
POOL_WIDTH = 256
GQA_Q_HEADS = 8
GQA_KV_HEADS = 2
GQA_GROUP = GQA_Q_HEADS // GQA_KV_HEADS
GQA_Q_WIDTH = GQA_Q_HEADS * HEAD_DIM
GQA_KV_WIDTH = GQA_KV_HEADS * HEAD_DIM
ROPE_THETA = 10000.0
OFF_A_Q = 0
OFF_A_K = OFF_A_Q + NA_WIDTH
OFF_A_V = OFF_A_K + NA_WIDTH
OFF_B_U = OFF_A_V + NA_WIDTH
OFF_C_Q = OFF_B_U + POOL_WIDTH
OFF_C_K = OFF_C_Q + GQA_Q_WIDTH
OFF_C_V = OFF_C_K + GQA_KV_WIDTH
D_IN = OFF_C_V + GQA_KV_WIDTH
D_MIX = NA_WIDTH + POOL_WIDTH + GQA_Q_WIDTH
QK_SCALE = HEAD_DIM ** -0.5
LOG2_E = float(np.log2(np.e))

TM = 256
LAT_TILES = SEQ // TM
STEP = 2 * TM
LAT_STEPS = SEQ // STEP
N_STEPS = LAT_STEPS + 1
NA_QROWS = TM // GRID_W
NA_KROWS = 12
NA_KEYS = NA_KROWS * GRID_W
NA_KPAIRS = NA_KROWS // 2
NA_DY = 2 * NA_WIN_H - 1
NA_DY_PAD = 4
NA_DY_SLOTS = NA_DY + 2 * NA_DY_PAD - 1
NA_PATTERN_GROUPS = (0, 1, 2, LAT_TILES - 1)
GQA_KEY_CHUNK = 768
FF_CHUNK = 256
ADA_TN = 1152
ADA_ROWS = 16
VMEM_LIMIT = 56 * 1024 * 1024

BF16 = jnp.bfloat16
F32 = jnp.float32


def _dot(a, b):
    return jnp.dot(a, b, preferred_element_type=F32)


def _dot_nt(a, b):
    return lax.dot_general(a, b, (((1,), (1,)), ((), ())), preferred_element_type=F32)


def _rms(x, g):
    ms = jnp.mean(x * x, axis=-1, keepdims=True)
    return x * lax.rsqrt(ms + EPS) * g


def _layer_block(l, shape, *, single_buffer=False):
    nd = len(shape)
    mode = dict(pipeline_mode=pl.Buffered(1)) if single_buffer else {}
    return pl.BlockSpec((None,) + tuple(shape), lambda *_: (l,) + (0,) * nd, **mode)


def _mod_block(l, bn):
    return pl.BlockSpec((None, None, N_MOD, D_MODEL),
                        lambda b, t: (l, b + (t // LAT_STEPS) * (bn - b), 0, 0))


def _gain_block(l, which):
    return pl.BlockSpec((None, None, 1, D_MODEL), lambda b, t: (l, which, 0, 0))


def _token_tile(width):
    return pl.BlockSpec((None, STEP, width), lambda b, t: (b, t, 0))


def _for_each_subtile(body, *, with_context=True):
    t = pl.program_id(1)
    subtiles = [pl.ds(s * TM, TM) for s in range(STEP // TM)]
    if not with_context:
        for rows in subtiles:
            body(rows, None)
        return

    @pl.when(t < LAT_STEPS)
    def _latent():
        for s, rows in enumerate(subtiles):
            body(rows, t * (STEP // TM) + s)

    @pl.when(t == LAT_STEPS)
    def _context():
        body(subtiles[0], None)


def _params(n_axes=2):
    return pltpu.CompilerParams(
        dimension_semantics=("arbitrary",) * n_axes, vmem_limit_bytes=VMEM_LIMIT)


def _ada_kernel(c_ref, w_ref, b_ref, o_ref):
    c = c_ref[...]
    s = c * jax.nn.sigmoid(c)
    o_ref[...] = jnp.dot(s, w_ref[...], preferred_element_type=F32,
                         precision=lax.Precision.HIGHEST) + b_ref[...]


def _ada_call(cc, w_ada, b_ada):
    depth = w_ada.shape[0]
    n = w_ada.shape[2]
    return pl.pallas_call(
        _ada_kernel,
        grid=(depth, n // ADA_TN),
        in_specs=[
            pl.BlockSpec((ADA_ROWS, D_MODEL), lambda l, j: (0, 0)),
            pl.BlockSpec((None, D_MODEL, ADA_TN), lambda l, j: (l, 0, j)),
            pl.BlockSpec((None, 1, ADA_TN), lambda l, j: (l, 0, j)),
        ],
        out_specs=pl.BlockSpec((None, ADA_ROWS, ADA_TN), lambda l, j: (l, 0, j)),
        out_shape=jax.ShapeDtypeStruct((depth, ADA_ROWS, n), F32),
        compiler_params=_params(),
        name="ada",
    )(cc, w_ada, b_ada.reshape(depth, 1, n))


def _ffn_kernel(*refs, mod_base, split_input, with_mix, with_proj, final):
    it = iter(refs)
    h_ref = next(it)
    if split_input:
        ctx_ref = next(it)
    if with_mix:
        oa_ref, ob_ref, oc_ref = next(it), next(it), next(it)
    mod_ref, g_ref = next(it), next(it)
    if with_mix:
        wout_ref = next(it)
    wup_ref, wdn_ref = next(it), next(it)
    if final:
        fg_ref = next(it)
    if with_proj:
        proj_in = [next(it) for _ in range(7)]
    o_ref = next(it)
    if with_proj:
        proj_out = [next(it) for _ in range(7)]
    hid_ref = next(it)

    mod = mod_ref[...]

    def body(rows, lat_tile):
        if split_input and lat_tile is None:
            x = ctx_ref[...]
        else:
            x = h_ref[rows, :]
        if with_mix:
            mixed = jnp.concatenate([oa_ref[rows, :], ob_ref[rows, :], oc_ref[rows, :]], axis=-1)
            x = x + mod[5:6] * _dot(mixed, wout_ref[...])
        y = _rms(x, g_ref[...])
        xn = (y * (1.0 + mod[mod_base + 1:mod_base + 2]) + mod[mod_base:mod_base + 1]).astype(BF16)
        for c in range(D_FF // FF_CHUNK):
            lo = c * FF_CHUNK
            a = _dot(xn, wup_ref[:, lo:lo + FF_CHUNK])
            b = _dot(xn, wup_ref[:, D_FF + lo:D_FF + lo + FF_CHUNK])
            hid_ref[rows, lo:lo + FF_CHUNK] = (a * jax.nn.sigmoid(a) * b).astype(BF16)
        out = x + (0.5 * mod[mod_base + 2:mod_base + 3]) * _dot(hid_ref[rows, :], wdn_ref[...])
        if final:
            out = _rms(out, fg_ref[...])
        o_ref[rows, :] = out
        if with_proj:
            _proj_rows(out, rows, mod, *proj_in, *proj_out)

    _for_each_subtile(body, with_context=not final)


def _ffn_call(h, l, mods, gains, which_gain, w_up, w_dn, *, mod_base, mix=None, proj=None,
              final_g=None):
    split_input = isinstance(h, tuple)
    bn = (h[0] if split_input else h).shape[0]
    with_mix = mix is not None
    with_proj = proj is not None
    final = final_g is not None
    n_steps = LAT_STEPS if final else N_STEPS
    if split_input:
        args = list(h)
        specs = [pl.BlockSpec((None, STEP, D_MODEL), lambda b, t: (b, jnp.minimum(t, LAT_STEPS - 1), 0)),
                 pl.BlockSpec((None, CTX_LEN, D_MODEL), lambda b, t: (b, 0, 0))]
    else:
        args, specs = [h], [_token_tile(D_MODEL)]
    if with_mix:
        oa, ob, oc, w_out = mix
        args += [oa, ob, oc]
        specs += [_token_tile(NA_WIDTH), _token_tile(POOL_WIDTH), _token_tile(GQA_Q_WIDTH)]
    args += [mods, gains]
    specs += [_mod_block(l, bn), _gain_block(l, which_gain)]
    if with_mix:
        args.append(w_out)
        specs.append(_layer_block(l, (D_MIX, D_MODEL), single_buffer=True))
    args += [w_up, w_dn]
    specs += [_layer_block(l, (D_MODEL, 2 * D_FF), single_buffer=True),
              _layer_block(l, (D_FF, D_MODEL), single_buffer=True)]
    if final:
        args.append(final_g.reshape(1, D_MODEL))
        specs.append(pl.BlockSpec((1, D_MODEL), lambda b, t: (0, 0)))
    out_specs = [_token_tile(D_MODEL)]
    out_shape = [jax.ShapeDtypeStruct((bn, SEQ if final else TOK, D_MODEL), F32)]
    if with_proj:
        w_in, rope_cs, rope_sn, qg, kg, ones_bd = proj
        args += [gains, w_in, rope_cs, rope_sn, qg, kg, ones_bd]
        specs += [
            _gain_block(l, 1),
            _layer_block(l, (D_MODEL, D_IN), single_buffer=True),
            pl.BlockSpec((STEP, 2 * HEAD_DIM), lambda b, t: (t, 0)),
            pl.BlockSpec((STEP, 2 * HEAD_DIM), lambda b, t: (t, 0)),
            _layer_block(l, (1, GQA_Q_WIDTH)),
            _layer_block(l, (1, GQA_KV_WIDTH)),
            pl.BlockSpec((GQA_Q_WIDTH, GQA_Q_WIDTH), lambda b, t: (0, 0)),
        ]
        rows_of = lambda w, dt: jax.ShapeDtypeStruct((bn, TOK, w), dt)
        out_specs += [_token_tile(NA_WIDTH), _token_tile(NA_WIDTH), _token_tile(NA_WIDTH),
                      _token_tile(POOL_WIDTH), _token_tile(GQA_Q_WIDTH), _token_tile(GQA_KV_WIDTH),
                      pl.BlockSpec((None, GQA_KV_WIDTH, STEP), lambda b, t: (b, 0, t))]
        out_shape += [rows_of(NA_WIDTH, BF16), rows_of(NA_WIDTH, BF16), rows_of(NA_WIDTH, BF16),
                      rows_of(POOL_WIDTH, F32), rows_of(GQA_Q_WIDTH, BF16), rows_of(GQA_KV_WIDTH, BF16),
                      jax.ShapeDtypeStruct((bn, GQA_KV_WIDTH, TOK), BF16)]
    outs = pl.pallas_call(
        functools.partial(_ffn_kernel, mod_base=mod_base, split_input=split_input, with_mix=with_mix,
                          with_proj=with_proj, final=final),
        grid=(bn, n_steps),
        in_specs=specs,
        out_specs=out_specs,
        out_shape=out_shape,
        scratch_shapes=[pltpu.VMEM((STEP, D_FF), BF16)],
        compiler_params=_params(),
        name="ffn_mix" if with_mix else "ffn_proj" if with_proj else "ffn",
    )(*args)
    return outs if with_proj else outs[0]


def _head_norm_rope(z, gain, cs, sn, ones_bd):
    width = z.shape[-1]
    zz = z * z
    hi = zz.astype(BF16)
    lo = (zz - hi.astype(F32)).astype(BF16)
    ms = _dot(hi, ones_bd) + _dot(lo, ones_bd)
    zn = z * lax.rsqrt(ms + EPS) * gain
    lane = lax.broadcasted_iota(jnp.int32, zn.shape, 1)
    partner = jnp.where((lane & 16) != 0,
                        pltpu.roll(zn, 16, axis=1), pltpu.roll(zn, width - 16, axis=1))
    reps = width // cs.shape[-1]
    if reps > 1:
        cs = jnp.concatenate([cs] * reps, axis=-1)
        sn = jnp.concatenate([sn] * reps, axis=-1)
    return zn * cs + partner * sn


def _proj_rows(x, rows, mod, g_ref, win_ref, cs_ref, sn_ref, qg_ref, kg_ref, bd_ref,
               qa_ref, ka_ref, va_ref, u_ref, qc_ref, kc_ref, vct_ref):
    bd = bd_ref[...]
    y = _rms(x, g_ref[...])
    a = (y * (1.0 + mod[4:5]) + mod[3:4]).astype(BF16)
    p = _dot(a, win_ref[...])
    qa_ref[rows, :] = (p[:, OFF_A_Q:OFF_A_K] * QK_SCALE).astype(BF16)
    ka_ref[rows, :] = p[:, OFF_A_K:OFF_A_V].astype(BF16)
    va_ref[rows, :] = p[:, OFF_A_V:OFF_B_U].astype(BF16)
    u_ref[rows, :] = p[:, OFF_B_U:OFF_C_Q]
    cs, sn = cs_ref[rows, :], sn_ref[rows, :]
    qc = _head_norm_rope(p[:, OFF_C_Q:OFF_C_K], qg_ref[...], cs, sn, bd)
    qc_ref[rows, :] = (qc * (QK_SCALE * LOG2_E)).astype(BF16)
    kc = _head_norm_rope(p[:, OFF_C_K:OFF_C_V], kg_ref[...], cs, sn,
                         bd[:GQA_KV_WIDTH, :GQA_KV_WIDTH])
    kc_ref[rows, :] = kc.astype(BF16)
    vct_ref[:, rows] = p[:, OFF_C_V:D_IN].T.astype(BF16)


def _softmax_pv(scores, values):
    m = scores[0].max(axis=-1, keepdims=True)
    for s in scores[1:]:
        m = jnp.maximum(m, s.max(axis=-1, keepdims=True))
    den = None
    acc = None
    for s, v in zip(scores, values):
        e = jnp.exp(s - m)
        d = e.sum(axis=-1, keepdims=True)
        o = _dot(e.astype(BF16), v)
        den = d if den is None else den + d
        acc = o if acc is None else acc + o
    return acc / den


def _na_kernel(q_ref, k_ref, v_ref, tb_ref, rm_ref, o_ref):
    head_of_lane = lax.broadcasted_iota(jnp.int32, (1, NA_WIDTH), 1) // HEAD_DIM
    k_ctx = k_ref[SEQ:TOK, :]
    v_ctx = v_ref[SEQ:TOK, :]

    def context(rows):
        q = q_ref[rows, :]
        out = jnp.zeros((TM, NA_WIDTH), F32)
        for h in range(NA_HEADS):
            mine = head_of_lane == h
            qh = jnp.where(mine, q, jnp.zeros_like(q))
            out = jnp.where(mine, _softmax_pv([_dot_nt(qh, k_ctx)], [v_ctx]), out)
        o_ref[rows, :] = out.astype(BF16)

    def body(rows, g):
        if g is None:
            return context(rows)
        q = q_ref[rows, :]
        first_row = jnp.clip(g * NA_QROWS - NA_WIN_H // 2, 0, GRID_H - NA_KROWS)
        start = pl.multiple_of(first_row * GRID_W, GRID_W)
        k_lat = k_ref[pl.ds(start, NA_KEYS), :]
        v_lat = v_ref[pl.ds(start, NA_KEYS), :]
        pattern = jnp.minimum(g, 2) + (g == LAT_TILES - 1).astype(jnp.int32)
        slot0 = first_row - g * NA_QROWS + NA_WIN_H - 1 + NA_DY_PAD

        def scores(h):
            bias = jnp.concatenate([
                jnp.concatenate([
                    tb_ref[h, slot0 + 2 * m - qr]
                    + rm_ref[pattern, qr * NA_KPAIRS + m:qr * NA_KPAIRS + m + 1, :]
                    for m in range(NA_KPAIRS)], axis=-1)
                for qr in range(NA_QROWS)], axis=0)
            qh = jnp.where(head_of_lane == h, q, jnp.zeros_like(q))
            return [_dot_nt(qh, k_lat) + bias, _dot_nt(qh, k_ctx)]

        out = jnp.zeros((TM, NA_WIDTH), F32)
        s_next = scores(0)
        for h in range(NA_HEADS):
            s_cur = s_next
            if h + 1 < NA_HEADS:
                s_next = scores(h + 1)
            out = jnp.where(head_of_lane == h, _softmax_pv(s_cur, [v_lat, v_ctx]), out)
        o_ref[rows, :] = out.astype(BF16)

    _for_each_subtile(body)


def _na_call(qa, ka, va, l, tb, rm):
    bn = qa.shape[0]
    whole = pl.BlockSpec((None, TOK, NA_WIDTH), lambda b, t: (b, 0, 0))
    return pl.pallas_call(
        _na_kernel,
        grid=(bn, N_STEPS),
        in_specs=[
            _token_tile(NA_WIDTH), whole, whole,
            _layer_block(l, tb.shape[1:], single_buffer=True),
            pl.BlockSpec(rm.shape, lambda b, t: (0, 0, 0)),
        ],
        out_specs=_token_tile(NA_WIDTH),
        out_shape=jax.ShapeDtypeStruct((bn, TOK, NA_WIDTH), BF16),
        compiler_params=_params(),
        name="na",
    )(qa, ka, va, tb, rm)


def _na_static_tables():
    col = np.arange(GRID_W)
    win_c0 = np.clip(col - NA_WIN_W // 2, 0, GRID_W - NA_WIN_W)
    col_ok = (col[None, :] >= win_c0[:, None]) & (col[None, :] < win_c0[:, None] + NA_WIN_W)
    dx = np.clip(col[None, :] - col[:, None], -(NA_WIN_W - 1), NA_WIN_W - 1) + NA_WIN_W - 1
    sel_x = np.eye(2 * NA_WIN_W - 1, dtype=np.float32)[dx]
    row_mask = np.zeros((len(NA_PATTERN_GROUPS), NA_QROWS, NA_KPAIRS, 2, GRID_W), np.float32)
    for p, grp in enumerate(NA_PATTERN_GROUPS):
        first_row = int(np.clip(grp * NA_QROWS - NA_WIN_H // 2, 0, GRID_H - NA_KROWS))
        q_row = grp * NA_QROWS + np.arange(NA_QROWS)
        q_r0 = np.clip(q_row - NA_WIN_H // 2, 0, GRID_H - NA_WIN_H)
        k_row = first_row + np.arange(NA_KROWS)
        row_ok = (k_row[None, :] >= q_r0[:, None]) & (k_row[None, :] < q_r0[:, None] + NA_WIN_H)
        row_mask[p] = np.where(row_ok, 0.0, NEG_INF).reshape(NA_QROWS, NA_KPAIRS, 2, 1)
    row_mask = row_mask.reshape(len(NA_PATTERN_GROUPS), NA_QROWS * NA_KPAIRS, 2 * GRID_W)
    return sel_x, col_ok, row_mask


_NA_SEL_X, _NA_COL_OK, _NA_ROW_MASK = _na_static_tables()


def _na_bias_table(rpb):
    t = jnp.einsum('lhab,xcb->lhaxc', rpb, _NA_SEL_X, precision=lax.Precision.HIGHEST)
    t = jnp.where(_NA_COL_OK, t, NEG_INF)
    t = jnp.pad(t, ((0, 0), (0, 0), (NA_DY_PAD, NA_DY_PAD), (0, 0), (0, 0)))
    return jnp.concatenate([t[:, :, :-1], t[:, :, 1:]], axis=-1)


def _shift_rows(a, d):
    n = a.shape[0]
    row = lax.broadcasted_iota(jnp.int32, a.shape, 0)
    rolled = pltpu.roll(a, d % n, axis=0)
    ok = (row >= d) if d > 0 else (row < n + d)
    return jnp.where(ok, rolled, 0.0)


def _pool_segment(u, w_bd, scale):
    n = u.shape[0]
    row = lax.broadcasted_iota(jnp.int32, u.shape, 0)
    grp = lax.broadcasted_iota(jnp.int32, u.shape, 1) // (POOL_WIDTH // len(POOL_WINDOWS))
    trail, lead = u, u
    total = jnp.zeros_like(u)
    count = jnp.ones_like(u)
    k = 1
    for i, w in enumerate(POOL_WINDOWS):
        while k < w // 2:
            trail = trail + _shift_rows(trail, k)
            lead = lead + _shift_rows(lead, -k)
            k *= 2
        win = _shift_rows(trail, 1) + lead
        cnt = (jnp.minimum(row + w // 2, n) - jnp.maximum(row - w // 2, 0)).astype(F32)
        total = jnp.where(grp == i, win, total)
        count = jnp.where(grp == i, cnt, count)
    y = (total / count - u).astype(BF16)
    return (_dot(y, w_bd) * scale).astype(BF16)


def _pool_kernel(u_ref, w_ref, s_ref, o_ref):
    w_bd = w_ref[...]
    scale = s_ref[...]
    o_ref[:SEQ, :] = _pool_segment(u_ref[:SEQ, :], w_bd, scale)
    o_ref[SEQ:, :] = _pool_segment(u_ref[SEQ:, :], w_bd, scale)


def _pool_call(u, l, w_bd, scale):
    bn = u.shape[0]
    whole = pl.BlockSpec((None, TOK, POOL_WIDTH), lambda b: (b, 0, 0))
    return pl.pallas_call(
        _pool_kernel,
        grid=(bn,),
        in_specs=[whole, _layer_block(l, (POOL_WIDTH, POOL_WIDTH)), _layer_block(l, (1, POOL_WIDTH))],
        out_specs=whole,
        out_shape=jax.ShapeDtypeStruct((bn, TOK, POOL_WIDTH), BF16),
        compiler_params=_params(1),
        name="pool",
    )(u, w_bd, scale)


def _gqa_tile(q, k_ref, vt_ref, key_chunks):
    half_of_lane = lax.broadcasted_iota(jnp.int32, (1, GQA_KV_WIDTH), 1) // HEAD_DIM
    stages = [(j, lo, hi) for j in range(GQA_GROUP) for lo, hi in key_chunks]

    def scores_t(stage):
        j, lo, hi = stage
        qb = q[:, j * GQA_KV_WIDTH:(j + 1) * GQA_KV_WIDTH]
        q2 = jnp.concatenate([jnp.where(half_of_lane == kv, qb, jnp.zeros_like(qb))
                              for kv in range(GQA_KV_HEADS)], axis=0)
        return _dot_nt(k_ref[lo:hi, :], q2)

    partial = {j: [] for j in range(GQA_GROUP)}
    s_next = scores_t(stages[0])
    for i, (j, lo, hi) in enumerate(stages):
        s_t = s_next
        if i + 1 < len(stages):
            s_next = scores_t(stages[i + 1])
        m = s_t.max(axis=0, keepdims=True)
        e = jnp.exp2(s_t - m)
        den = e.sum(axis=0, keepdims=True)
        e = e.astype(BF16)
        o_t = [_dot(vt_ref[kv * HEAD_DIM:(kv + 1) * HEAD_DIM, lo:hi], e[:, kv * TM:(kv + 1) * TM])
               for kv in range(GQA_KV_HEADS)]
        partial[j].append((m, den, o_t))

    blocks = []
    for j in range(GQA_GROUP):
        m_all = functools.reduce(jnp.maximum, [m for m, _, _ in partial[j]])
        weights = [jnp.exp2(m - m_all) for m, _, _ in partial[j]]
        den = sum(w * d for w, (_, d, _) in zip(weights, partial[j]))
        o_t = jnp.concatenate([
            sum(w[:, kv * TM:(kv + 1) * TM] * o[kv] for w, (_, _, o) in zip(weights, partial[j]))
            / den[:, kv * TM:(kv + 1) * TM]
            for kv in range(GQA_KV_HEADS)], axis=0)
        blocks.append(o_t.T)
    return jnp.concatenate(blocks, axis=-1).astype(BF16)


def _gqa_kernel(q_ref, k_ref, vt_ref, o_ref):
    t = pl.program_id(1)

    @pl.when(t < LAT_TILES)
    def _latent():
        chunks = [(lo, lo + GQA_KEY_CHUNK) for lo in range(0, TOK, GQA_KEY_CHUNK)]
        o_ref[...] = _gqa_tile(q_ref[...], k_ref, vt_ref, chunks)

    @pl.when(t == LAT_TILES)
    def _context():
        o_ref[...] = _gqa_tile(q_ref[...], k_ref, vt_ref, [(SEQ, TOK)])


def _gqa_call(qc, kc, vc_t):
    bn = qc.shape[0]
    subtile = pl.BlockSpec((None, TM, GQA_Q_WIDTH), lambda b, t: (b, t, 0))
    return pl.pallas_call(
        _gqa_kernel,
        grid=(bn, TOK // TM),
        in_specs=[subtile,
                  pl.BlockSpec((None, TOK, GQA_KV_WIDTH), lambda b, t: (b, 0, 0)),
                  pl.BlockSpec((None, GQA_KV_WIDTH, TOK), lambda b, t: (b, 0, 0))],
        out_specs=subtile,
        out_shape=jax.ShapeDtypeStruct((bn, TOK, GQA_Q_WIDTH), BF16),
        compiler_params=_params(),
        name="gqa",
    )(qc, kc, vc_t)


def _rope_tables():
    pos = np.arange(SEQ)
    inv_freq = ROPE_THETA ** (-np.arange(0, HEAD_DIM // 2, 2, dtype=np.float32) / (HEAD_DIM // 2))
    ang_row = (pos // GRID_W).astype(np.float32)[:, None] * inv_freq[None, :].astype(np.float32)
    ang_col = (pos % GRID_W).astype(np.float32)[:, None] * inv_freq[None, :].astype(np.float32)
    ang = np.concatenate([ang_row, ang_row, ang_col, ang_col], axis=-1).astype(np.float32)
    sign = np.tile(np.repeat(np.array([-1.0, 1.0], np.float32), HEAD_DIM // 4), 2)
    return ang, sign


_ROPE_ANG, _ROPE_SIGN = _rope_tables()


def _group_major(a, axis):
    shape = a.shape
    a = a.reshape(shape[:axis] + (GQA_KV_HEADS, GQA_GROUP, HEAD_DIM) + shape[axis + 1:])
    return jnp.swapaxes(a, axis, axis + 1).reshape(shape)


def kernel(x, c, ctx, c_ctx, w_ada, b_ada, norm_g, ffn1_up, ffn1_down, ffn2_up, ffn2_down,
           w_in, w_out, na_rpb, pool_w, pool_scale, q_norm_g, k_norm_g, final_g):
    bn = x.shape[0]
    depth = w_ada.shape[0]
    assert x.shape == (bn, SEQ, D_MODEL) and ctx.shape == (bn, CTX_LEN, D_MODEL)
    assert bn < ADA_ROWS

    cc = jnp.zeros((ADA_ROWS, D_MODEL), F32).at[:bn].set(c).at[bn].set(c_ctx)
    mods = _ada_call(cc, w_ada, b_ada).reshape(depth, ADA_ROWS, N_MOD, D_MODEL)
    gains = norm_g.reshape(depth, 3, 1, D_MODEL)

    ang = jnp.asarray(_ROPE_ANG)
    cs_lat = jnp.tile(jnp.cos(ang), (1, 2))
    sn_lat = jnp.tile(jnp.sin(ang) * jnp.asarray(_ROPE_SIGN)[None, :], (1, 2))
    rope_cs = jnp.concatenate([cs_lat, jnp.ones((CTX_LEN, 2 * HEAD_DIM), F32)], axis=0)
    rope_sn = jnp.concatenate([sn_lat, jnp.zeros((CTX_LEN, 2 * HEAD_DIM), F32)], axis=0)
    ones_bd = jnp.asarray(
        np.kron(np.eye(GQA_Q_HEADS, dtype=np.float32), np.full((HEAD_DIM, HEAD_DIM), 1.0 / HEAD_DIM,
                                                               np.float32))).astype(BF16)

    w_in_p = jnp.concatenate([w_in[:, :, :OFF_C_Q], _group_major(w_in[:, :, OFF_C_Q:OFF_C_K], 2),
                              w_in[:, :, OFF_C_K:]], axis=-1).astype(BF16)
    w_out_p = jnp.concatenate([w_out[:, :NA_WIDTH + POOL_WIDTH],
                               _group_major(w_out[:, NA_WIDTH + POOL_WIDTH:], 1)], axis=1).astype(BF16)
    up1, dn1 = ffn1_up.astype(BF16), ffn1_down.astype(BF16)
    up2, dn2 = ffn2_up.astype(BF16), ffn2_down.astype(BF16)
    pool_bd = jnp.einsum('gh,lgcd->lgchd', jnp.eye(len(POOL_WINDOWS), dtype=F32), pool_w)
    pool_bd = pool_bd.reshape(depth, POOL_WIDTH, POOL_WIDTH).astype(BF16)
    pool_sc = pool_scale.reshape(depth, 1, POOL_WIDTH)
    qg = jnp.tile(q_norm_g, (1, GQA_Q_HEADS)).reshape(depth, 1, GQA_Q_WIDTH)
    kg = jnp.tile(k_norm_g, (1, GQA_KV_HEADS)).reshape(depth, 1, GQA_KV_WIDTH)
    na_tb = _na_bias_table(na_rpb)
    na_rm = jnp.asarray(_NA_ROW_MASK)

    h = (x, ctx)
    for l in range(depth):
        last = l == depth - 1
        h, qa, ka, va, u, qc, kc, vc_t = _ffn_call(
            h, l, mods, gains, 0, up1, dn1, mod_base=0,
            proj=(w_in_p, rope_cs, rope_sn, qg, kg, ones_bd))
        oa = _na_call(qa, ka, va, l, na_tb, na_rm)
        ob = _pool_call(u, l, pool_bd, pool_sc)
        oc = _gqa_call(qc, kc, vc_t)
        h = _ffn_call(h, l, mods, gains, 2, up2, dn2, mod_base=6,
                      mix=(oa, ob, oc, w_out_p), final_g=final_g if last else None)
    return h
```

```python
import functools

import numpy as np
import jax
import jax.numpy as jnp
from jax import lax
from jax.experimental import pallas as pl
from jax.experimental.pallas import tpu as pltpu

D_MODEL = 1024
SEQ = 2048
CTX_LEN = 256
TOK = SEQ + CTX_LEN
GRID_W = 64
GRID_H = SEQ // GRID_W
HEAD_DIM = 64
N_MOD = 9
D_FF = 2816
EPS = 1e-6
NEG_INF = -1e30
NA_HEADS = 4
NA_WIN_H = 8
NA_WIN_W = 16
NA_WIDTH = NA_HEADS * HEAD_DIM
POOL_WINDOWS = (2, 4, 8, 16)
POOL_WIDTH = 256
GQA_Q_HEADS = 8
GQA_KV_HEADS = 2
GQA_GROUP = GQA_Q_HEADS // GQA_KV_HEADS
GQA_Q_WIDTH = GQA_Q_HEADS * HEAD_DIM
GQA_KV_WIDTH = GQA_KV_HEADS * HEAD_DIM
ROPE_THETA = 10000.0
OFF_A_Q = 0
OFF_A_K = OFF_A_Q + NA_WIDTH
OFF_A_V = OFF_A_K + NA_WIDTH
OFF_B_U = OFF_A_V + NA_WIDTH
OFF_C_Q = OFF_B_U + POOL_WIDTH
OFF_C_K = OFF_C_Q + GQA_Q_WIDTH
OFF_C_V = OFF_C_K + GQA_KV_WIDTH
D_IN = OFF_C_V + GQA_KV_WIDTH
D_MIX = NA_WIDTH + POOL_WIDTH + GQA_Q_WIDTH
QK_SCALE = HEAD_DIM ** -0.5
LOG2_E = float(np.log2(np.e))

TM = 256
LAT_TILES = SEQ // TM
STEP = 2 * TM
LAT_STEPS = SEQ // STEP
N_STEPS = LAT_STEPS + 1
NA_QROWS = TM // GRID_W
NA_KROWS = 12
NA_KEYS = NA_KROWS * GRID_W
NA_KPAIRS = NA_KROWS // 2
NA_DY = 2 * NA_WIN_H - 1
NA_DY_PAD = 4
NA_DY_SLOTS = NA_DY + 2 * NA_DY_PAD - 1
NA_PATTERN_GROUPS = (0, 1, 2, LAT_TILES - 1)
GQA_KEY_CHUNK = 768
GQA_SAFE_LOGIT = 64.0
FF_CHUNK = 256
ADA_TN = 1152
ADA_ROWS = 16
VMEM_LIMIT = 56 * 1024 * 1024

BF16 = jnp.bfloat16
F32 = jnp.float32


def _dot(a, b):
    return jnp.dot(a, b, preferred_element_type=F32)


def _dot_nt(a, b):
    return lax.dot_general(a, b, (((1,), (1,)), ((), ())), preferred_element_type=F32)


def _rms(x, g):
    ms = jnp.mean(x * x, axis=-1, keepdims=True)
    return x * lax.rsqrt(ms + EPS) * g


def _layer_block(l, shape, *, single_buffer=False):
    nd = len(shape)
    mode = dict(pipeline_mode=pl.Buffered(1)) if single_buffer else {}
    return pl.BlockSpec((None,) + tuple(shape), lambda *_: (l,) + (0,) * nd, **mode)


def _mod_block(l, bn):
    return pl.BlockSpec((None, None, N_MOD, D_MODEL),
                        lambda b, t: (l, b + (t // LAT_STEPS) * (bn - b), 0, 0))


def _gain_block(l, which):
    return pl.BlockSpec((None, None, 1, D_MODEL), lambda b, t: (l, which, 0, 0))


def _token_tile(width):
    return pl.BlockSpec((None, STEP, width), lambda b, t: (b, t, 0))


def _for_each_subtile(body, *, with_context=True):
    t = pl.program_id(1)
    subtiles = [pl.ds(s * TM, TM) for s in range(STEP // TM)]
    if not with_context:
        for rows in subtiles:
            body(rows, None)
        return

    @pl.when(t < LAT_STEPS)
    def _latent():
        for s, rows in enumerate(subtiles):
            body(rows, t * (STEP // TM) + s)

    @pl.when(t == LAT_STEPS)
    def _context():
        body(subtiles[0], None)


def _params(n_axes=2):
    return pltpu.CompilerParams(
        dimension_semantics=("arbitrary",) * n_axes, vmem_limit_bytes=VMEM_LIMIT)


def _ada_kernel(c_ref, w_ref, b_ref, o_ref):
    c = c_ref[...]
    s = c * jax.nn.sigmoid(c)
    o_ref[...] = jnp.dot(s, w_ref[...], preferred_element_type=F32,
                         precision=lax.Precision.HIGHEST) + b_ref[...]


def _ada_call(cc, w_ada, b_ada):
    depth = w_ada.shape[0]
    n = w_ada.shape[2]
    return pl.pallas_call(
        _ada_kernel,
        grid=(depth, n // ADA_TN),
        in_specs=[
            pl.BlockSpec((ADA_ROWS, D_MODEL), lambda l, j: (0, 0)),
            pl.BlockSpec((None, D_MODEL, ADA_TN), lambda l, j: (l, 0, j)),
            pl.BlockSpec((None, 1, ADA_TN), lambda l, j: (l, 0, j)),
        ],
        out_specs=pl.BlockSpec((None, ADA_ROWS, ADA_TN), lambda l, j: (l, 0, j)),
        out_shape=jax.ShapeDtypeStruct((depth, ADA_ROWS, n), F32),
        compiler_params=_params(),
        name="ada",
    )(cc, w_ada, b_ada.reshape(depth, 1, n))


def _ffn_kernel(*refs, mod_base, split_input, with_mix, with_proj, final):
    it = iter(refs)
    h_ref = next(it)
    if split_input:
        ctx_ref = next(it)
    if with_mix:
        oa_ref, ob_ref, oc_ref = next(it), next(it), next(it)
    mod_ref, g_ref = next(it), next(it)
    if with_mix:
        wout_ref = next(it)
    wup_ref, wdn_ref = next(it), next(it)
    if final:
        fg_ref = next(it)
    if with_proj:
        proj_in = [next(it) for _ in range(7)]
    o_ref = next(it)
    if with_proj:
        proj_out = [next(it) for _ in range(7)]
    hid_ref = next(it)

    mod = mod_ref[...]

    def body(rows, lat_tile):
        if split_input and lat_tile is None:
            x = ctx_ref[...]
        else:
            x = h_ref[rows, :]
        if with_mix:
            mixed = jnp.concatenate([oa_ref[rows, :], ob_ref[rows, :], oc_ref[rows, :]], axis=-1)
            x = x + mod[5:6] * _dot(mixed, wout_ref[...])
        y = _rms(x, g_ref[...])
        xn = (y * (1.0 + mod[mod_base + 1:mod_base + 2]) + mod[mod_base:mod_base + 1]).astype(BF16)
        for c in range(D_FF // FF_CHUNK):
            lo = c * FF_CHUNK
            a = _dot(xn, wup_ref[:, lo:lo + FF_CHUNK])
            b = _dot(xn, wup_ref[:, D_FF + lo:D_FF + lo + FF_CHUNK])
            hid_ref[rows, lo:lo + FF_CHUNK] = (a * jax.nn.sigmoid(a) * b).astype(BF16)
        out = x + (0.5 * mod[mod_base + 2:mod_base + 3]) * _dot(hid_ref[rows, :], wdn_ref[...])
        if final:
            out = _rms(out, fg_ref[...])
        o_ref[rows, :] = out
        if with_proj:
            _proj_rows(out, rows, mod, *proj_in, *proj_out)

    _for_each_subtile(body, with_context=not final)


def _ffn_call(h, l, mods, gains, which_gain, w_up, w_dn, *, mod_base, mix=None, proj=None,
              final_g=None):
    split_input = isinstance(h, tuple)
    bn = (h[0] if split_input else h).shape[0]
    with_mix = mix is not None
    with_proj = proj is not None
    final = final_g is not None
    n_steps = LAT_STEPS if final else N_STEPS
    if split_input:
        args = list(h)
        specs = [pl.BlockSpec((None, STEP, D_MODEL), lambda b, t: (b, jnp.minimum(t, LAT_STEPS - 1), 0)),
                 pl.BlockSpec((None, CTX_LEN, D_MODEL), lambda b, t: (b, 0, 0))]
    else:
        args, specs = [h], [_token_tile(D_MODEL)]
    if with_mix:
        oa, ob, oc, w_out = mix
        args += [oa, ob, oc]
        specs += [_token_tile(NA_WIDTH), _token_tile(POOL_WIDTH), _token_tile(GQA_Q_WIDTH)]
    args += [mods, gains]
    specs += [_mod_block(l, bn), _gain_block(l, which_gain)]
    if with_mix:
        args.append(w_out)
        specs.append(_layer_block(l, (D_MIX, D_MODEL), single_buffer=True))
    args += [w_up, w_dn]
    specs += [_layer_block(l, (D_MODEL, 2 * D_FF), single_buffer=True),
              _layer_block(l, (D_FF, D_MODEL), single_buffer=True)]
    if final:
        args.append(final_g.reshape(1, D_MODEL))
        specs.append(pl.BlockSpec((1, D_MODEL), lambda b, t: (0, 0)))
    out_specs = [_token_tile(D_MODEL)]
    out_shape = [jax.ShapeDtypeStruct((bn, SEQ if final else TOK, D_MODEL), F32)]
    if with_proj:
        w_in, rope_cs, rope_sn, qg, kg, ones_bd = proj
        args += [gains, w_in, rope_cs, rope_sn, qg, kg, ones_bd]
        specs += [
            _gain_block(l, 1),
            _layer_block(l, (D_MODEL, D_IN), single_buffer=True),
            pl.BlockSpec((STEP, 2 * HEAD_DIM), lambda b, t: (t, 0)),
            pl.BlockSpec((STEP, 2 * HEAD_DIM), lambda b, t: (t, 0)),
            _layer_block(l, (1, GQA_Q_WIDTH)),
            _layer_block(l, (1, GQA_KV_WIDTH)),
            pl.BlockSpec((GQA_Q_WIDTH, GQA_Q_WIDTH), lambda b, t: (0, 0)),
        ]
        rows_of = lambda w, dt: jax.ShapeDtypeStruct((bn, TOK, w), dt)
        out_specs += [_token_tile(NA_WIDTH), _token_tile(NA_WIDTH), _token_tile(NA_WIDTH),
                      _token_tile(POOL_WIDTH), _token_tile(GQA_Q_WIDTH), _token_tile(GQA_KV_WIDTH),
                      pl.BlockSpec((None, GQA_KV_WIDTH, STEP), lambda b, t: (b, 0, t))]
        out_shape += [rows_of(NA_WIDTH, BF16), rows_of(NA_WIDTH, BF16), rows_of(NA_WIDTH, BF16),
                      rows_of(POOL_WIDTH, F32), rows_of(GQA_Q_WIDTH, BF16), rows_of(GQA_KV_WIDTH, BF16),
                      jax.ShapeDtypeStruct((bn, GQA_KV_WIDTH, TOK), BF16)]
    outs = pl.pallas_call(
        functools.partial(_ffn_kernel, mod_base=mod_base, split_input=split_input, with_mix=with_mix,
                          with_proj=with_proj, final=final),
        grid=(bn, n_steps),
        in_specs=specs,
        out_specs=out_specs,
        out_shape=out_shape,
        scratch_shapes=[pltpu.VMEM((STEP, D_FF), BF16)],
        compiler_params=_params(),
        name="ffn_mix" if with_mix else "ffn_proj" if with_proj else "ffn",
    )(*args)
    return outs if with_proj else outs[0]


def _head_norm_rope(z, gain, cs, sn, ones_bd):
    width = z.shape[-1]
    zz = z * z
    hi = zz.astype(BF16)
    lo = (zz - hi.astype(F32)).astype(BF16)
    ms = _dot(hi, ones_bd) + _dot(lo, ones_bd)
    zn = z * lax.rsqrt(ms + EPS) * gain
    lane = lax.broadcasted_iota(jnp.int32, zn.shape, 1)
    partner = jnp.where((lane & 16) != 0,
                        pltpu.roll(zn, 16, axis=1), pltpu.roll(zn, width - 16, axis=1))
    reps = width // cs.shape[-1]
    if reps > 1:
        cs = jnp.concatenate([cs] * reps, axis=-1)
        sn = jnp.concatenate([sn] * reps, axis=-1)
    return zn * cs + partner * sn


def _proj_rows(x, rows, mod, g_ref, win_ref, cs_ref, sn_ref, qg_ref, kg_ref, bd_ref,
               qa_ref, ka_ref, va_ref, u_ref, qc_ref, kc_ref, vct_ref):
    bd = bd_ref[...]
    y = _rms(x, g_ref[...])
    a = (y * (1.0 + mod[4:5]) + mod[3:4]).astype(BF16)
    p = _dot(a, win_ref[...])
    qa_ref[rows, :] = (p[:, OFF_A_Q:OFF_A_K] * QK_SCALE).astype(BF16)
    ka_ref[rows, :] = p[:, OFF_A_K:OFF_A_V].astype(BF16)
    va_ref[rows, :] = p[:, OFF_A_V:OFF_B_U].astype(BF16)
    u_ref[rows, :] = p[:, OFF_B_U:OFF_C_Q]
    cs, sn = cs_ref[rows, :], sn_ref[rows, :]
    qc = _head_norm_rope(p[:, OFF_C_Q:OFF_C_K], qg_ref[...], cs, sn, bd)
    qc_ref[rows, :] = (qc * (QK_SCALE * LOG2_E)).astype(BF16)
    kc = _head_norm_rope(p[:, OFF_C_K:OFF_C_V], kg_ref[...], cs, sn,
                         bd[:GQA_KV_WIDTH, :GQA_KV_WIDTH])
    kc_ref[rows, :] = kc.astype(BF16)
    vct_ref[:, rows] = p[:, OFF_C_V:D_IN].T.astype(BF16)


def _softmax_pv(scores, values):
    m = scores[0].max(axis=-1, keepdims=True)
    for s in scores[1:]:
        m = jnp.maximum(m, s.max(axis=-1, keepdims=True))
    den = None
    acc = None
    for s, v in zip(scores, values):
        e = jnp.exp(s - m)
        d = e.sum(axis=-1, keepdims=True)
        o = _dot(e.astype(BF16), v)
        den = d if den is None else den + d
        acc = o if acc is None else acc + o
    return acc / den


def _na_kernel(q_ref, k_ref, v_ref, tb_ref, rm_ref, o_ref):
    head_of_lane = lax.broadcasted_iota(jnp.int32, (1, NA_WIDTH), 1) // HEAD_DIM
    k_ctx = k_ref[SEQ:TOK, :]
    v_ctx = v_ref[SEQ:TOK, :]

    def context(rows):
        q = q_ref[rows, :]
        out = jnp.zeros((TM, NA_WIDTH), F32)
        for h in range(NA_HEADS):
            mine = head_of_lane == h
            qh = jnp.where(mine, q, jnp.zeros_like(q))
            out = jnp.where(mine, _softmax_pv([_dot_nt(qh, k_ctx)], [v_ctx]), out)
        o_ref[rows, :] = out.astype(BF16)

    def body(rows, g):
        if g is None:
            return context(rows)
        q = q_ref[rows, :]
        first_row = jnp.clip(g * NA_QROWS - NA_WIN_H // 2, 0, GRID_H - NA_KROWS)
        start = pl.multiple_of(first_row * GRID_W, GRID_W)
        k_lat = k_ref[pl.ds(start, NA_KEYS), :]
        v_lat = v_ref[pl.ds(start, NA_KEYS), :]
        pattern = jnp.minimum(g, 2) + (g == LAT_TILES - 1).astype(jnp.int32)
        slot0 = first_row - g * NA_QROWS + NA_WIN_H - 1 + NA_DY_PAD

        def scores(h):
            bias = jnp.concatenate([
                jnp.concatenate([
                    tb_ref[h, slot0 + 2 * m - qr]
                    + rm_ref[pattern, qr * NA_KPAIRS + m:qr * NA_KPAIRS + m + 1, :]
                    for m in range(NA_KPAIRS)], axis=-1)
                for qr in range(NA_QROWS)], axis=0)
            qh = jnp.where(head_of_lane == h, q, jnp.zeros_like(q))
            return [_dot_nt(qh, k_lat) + bias, _dot_nt(qh, k_ctx)]

        out = jnp.zeros((TM, NA_WIDTH), F32)
        s_next = scores(0)
        for h in range(NA_HEADS):
            s_cur = s_next
            if h + 1 < NA_HEADS:
                s_next = scores(h + 1)
            out = jnp.where(head_of_lane == h, _softmax_pv(s_cur, [v_lat, v_ctx]), out)
        o_ref[rows, :] = out.astype(BF16)

    _for_each_subtile(body)


def _na_call(qa, ka, va, l, tb, rm):
    bn = qa.shape[0]
    whole = pl.BlockSpec((None, TOK, NA_WIDTH), lambda b, t: (b, 0, 0))
    return pl.pallas_call(
        _na_kernel,
        grid=(bn, N_STEPS),
        in_specs=[
            _token_tile(NA_WIDTH), whole, whole,
            _layer_block(l, tb.shape[1:], single_buffer=True),
            pl.BlockSpec(rm.shape, lambda b, t: (0, 0, 0)),
        ],
        out_specs=_token_tile(NA_WIDTH),
        out_shape=jax.ShapeDtypeStruct((bn, TOK, NA_WIDTH), BF16),
        compiler_params=_params(),
        name="na",
    )(qa, ka, va, tb, rm)


def _na_static_tables():
    col = np.arange(GRID_W)
    win_c0 = np.clip(col - NA_WIN_W // 2, 0, GRID_W - NA_WIN_W)
    col_ok = (col[None, :] >= win_c0[:, None]) & (col[None, :] < win_c0[:, None] + NA_WIN_W)
    dx = np.clip(col[None, :] - col[:, None], -(NA_WIN_W - 1), NA_WIN_W - 1) + NA_WIN_W - 1
    sel_x = np.eye(2 * NA_WIN_W - 1, dtype=np.float32)[dx]
    row_mask = np.zeros((len(NA_PATTERN_GROUPS), NA_QROWS, NA_KPAIRS, 2, GRID_W), np.float32)
    for p, grp in enumerate(NA_PATTERN_GROUPS):
        first_row = int(np.clip(grp * NA_QROWS - NA_WIN_H // 2, 0, GRID_H - NA_KROWS))
        q_row = grp * NA_QROWS + np.arange(NA_QROWS)
        q_r0 = np.clip(q_row - NA_WIN_H // 2, 0, GRID_H - NA_WIN_H)
        k_row = first_row + np.arange(NA_KROWS)
        row_ok = (k_row[None, :] >= q_r0[:, None]) & (k_row[None, :] < q_r0[:, None] + NA_WIN_H)
        row_mask[p] = np.where(row_ok, 0.0, NEG_INF).reshape(NA_QROWS, NA_KPAIRS, 2, 1)
    row_mask = row_mask.reshape(len(NA_PATTERN_GROUPS), NA_QROWS * NA_KPAIRS, 2 * GRID_W)
    return sel_x, col_ok, row_mask


_NA_SEL_X, _NA_COL_OK, _NA_ROW_MASK = _na_static_tables()


def _na_bias_table(rpb):
    t = jnp.einsum('lhab,xcb->lhaxc', rpb, _NA_SEL_X, precision=lax.Precision.HIGHEST)
    t = jnp.where(_NA_COL_OK, t, NEG_INF)
    t = jnp.pad(t, ((0, 0), (0, 0), (NA_DY_PAD, NA_DY_PAD), (0, 0), (0, 0)))
    return jnp.concatenate([t[:, :, :-1], t[:, :, 1:]], axis=-1)


def _shift_rows(a, d):
    n = a.shape[0]
    row = lax.broadcasted_iota(jnp.int32, a.shape, 0)
    rolled = pltpu.roll(a, d % n, axis=0)
    ok = (row >= d) if d > 0 else (row < n + d)
    return jnp.where(ok, rolled, 0.0)


def _pool_segment(u, w_bd, scale):
    n = u.shape[0]
    row = lax.broadcasted_iota(jnp.int32, u.shape, 0)
    grp = lax.broadcasted_iota(jnp.int32, u.shape, 1) // (POOL_WIDTH // len(POOL_WINDOWS))
    trail, lead = u, u
    total = jnp.zeros_like(u)
    count = jnp.ones_like(u)
    k = 1
    for i, w in enumerate(POOL_WINDOWS):
        while k < w // 2:
            trail = trail + _shift_rows(trail, k)
            lead = lead + _shift_rows(lead, -k)
            k *= 2
        win = _shift_rows(trail, 1) + lead
        cnt = (jnp.minimum(row + w // 2, n) - jnp.maximum(row - w // 2, 0)).astype(F32)
        total = jnp.where(grp == i, win, total)
        count = jnp.where(grp == i, cnt, count)
    y = (total / count - u).astype(BF16)
    return (_dot(y, w_bd) * scale).astype(BF16)


def _pool_kernel(u_ref, w_ref, s_ref, o_ref):
    w_bd = w_ref[...]
    scale = s_ref[...]
    o_ref[:SEQ, :] = _pool_segment(u_ref[:SEQ, :], w_bd, scale)
    o_ref[SEQ:, :] = _pool_segment(u_ref[SEQ:, :], w_bd, scale)


def _pool_call(u, l, w_bd, scale):
    bn = u.shape[0]
    whole = pl.BlockSpec((None, TOK, POOL_WIDTH), lambda b: (b, 0, 0))
    return pl.pallas_call(
        _pool_kernel,
        grid=(bn,),
        in_specs=[whole, _layer_block(l, (POOL_WIDTH, POOL_WIDTH)), _layer_block(l, (1, POOL_WIDTH))],
        out_specs=whole,
        out_shape=jax.ShapeDtypeStruct((bn, TOK, POOL_WIDTH), BF16),
        compiler_params=_params(1),
        name="pool",
    )(u, w_bd, scale)


def _gqa_tile(q, k_ref, vt_ref, key_chunks, bounded):
    half_of_lane = lax.broadcasted_iota(jnp.int32, (1, GQA_KV_WIDTH), 1) // HEAD_DIM
    stages = [(j, lo, hi) for j in range(GQA_GROUP) for lo, hi in key_chunks]

    def scores_t(stage):
        j, lo, hi = stage
        qb = q[:, j * GQA_KV_WIDTH:(j + 1) * GQA_KV_WIDTH]
        q2 = jnp.concatenate([jnp.where(half_of_lane == kv, qb, jnp.zeros_like(qb))
                              for kv in range(GQA_KV_HEADS)], axis=0)
        return _dot_nt(k_ref[lo:hi, :], q2)

    partial = {j: [] for j in range(GQA_GROUP)}
    s_next = scores_t(stages[0])
    for i, (j, lo, hi) in enumerate(stages):
        s_t = s_next
        if i + 1 < len(stages):
            s_next = scores_t(stages[i + 1])
        if bounded:
            m = None
            e = jnp.exp2(s_t)
        else:
            m = s_t.max(axis=0, keepdims=True)
            e = jnp.exp2(s_t - m)
        den = e.sum(axis=0, keepdims=True)
        e = e.astype(BF16)
        o_t = [_dot(vt_ref[kv * HEAD_DIM:(kv + 1) * HEAD_DIM, lo:hi], e[:, kv * TM:(kv + 1) * TM])
               for kv in range(GQA_KV_HEADS)]
        partial[j].append((m, den, o_t))

    blocks = []
    for j in range(GQA_GROUP):
        if bounded:
            weights = [1.0] * len(partial[j])
        else:
            m_all = functools.reduce(jnp.maximum, [m for m, _, _ in partial[j]])
            weights = [jnp.exp2(m - m_all) for m, _, _ in partial[j]]
        den = sum(w * d for w, (_, d, _) in zip(weights, partial[j]))
        o_t = jnp.concatenate([
            sum((w if bounded else w[:, kv * TM:(kv + 1) * TM]) * o[kv]
                for w, (_, _, o) in zip(weights, partial[j]))
            / den[:, kv * TM:(kv + 1) * TM]
            for kv in range(GQA_KV_HEADS)], axis=0)
        blocks.append(o_t.T)
    return jnp.concatenate(blocks, axis=-1).astype(BF16)


def _gqa_kernel(q_ref, k_ref, vt_ref, o_ref, *, bounded):
    t = pl.program_id(1)

    @pl.when(t < LAT_TILES)
    def _latent():
        chunks = [(lo, lo + GQA_KEY_CHUNK) for lo in range(0, TOK, GQA_KEY_CHUNK)]
        o_ref[...] = _gqa_tile(q_ref[...], k_ref, vt_ref, chunks, bounded)

    @pl.when(t == LAT_TILES)
    def _context():
        o_ref[...] = _gqa_tile(q_ref[...], k_ref, vt_ref, [(SEQ, TOK)], bounded)


def _gqa_logit_bound(q_gain, k_gain):
    return (HEAD_DIM * jnp.max(jnp.abs(q_gain)) * jnp.max(jnp.abs(k_gain))
            * (QK_SCALE * LOG2_E) * (1.0 + 2.0 ** -7) ** 2)


def _gqa_call(qc, kc, vc_t, *, bounded):
    bn = qc.shape[0]
    subtile = pl.BlockSpec((None, TM, GQA_Q_WIDTH), lambda b, t: (b, t, 0))
    return pl.pallas_call(
        functools.partial(_gqa_kernel, bounded=bounded),
        grid=(bn, TOK // TM),
        in_specs=[subtile,
                  pl.BlockSpec((None, TOK, GQA_KV_WIDTH), lambda b, t: (b, 0, 0)),
                  pl.BlockSpec((None, GQA_KV_WIDTH, TOK), lambda b, t: (b, 0, 0))],
        out_specs=subtile,
        out_shape=jax.ShapeDtypeStruct((bn, TOK, GQA_Q_WIDTH), BF16),
        compiler_params=_params(),
        name="gqa_bounded" if bounded else "gqa",
    )(qc, kc, vc_t)


def _rope_tables():
    pos = np.arange(SEQ)
    inv_freq = ROPE_THETA ** (-np.arange(0, HEAD_DIM // 2, 2, dtype=np.float32) / (HEAD_DIM // 2))
    ang_row = (pos // GRID_W).astype(np.float32)[:, None] * inv_freq[None, :].astype(np.float32)
    ang_col = (pos % GRID_W).astype(np.float32)[:, None] * inv_freq[None, :].astype(np.float32)
    ang = np.concatenate([ang_row, ang_row, ang_col, ang_col], axis=-1).astype(np.float32)
    sign = np.tile(np.repeat(np.array([-1.0, 1.0], np.float32), HEAD_DIM // 4), 2)
    return ang, sign


_ROPE_ANG, _ROPE_SIGN = _rope_tables()


def _group_major(a, axis):
    shape = a.shape
    a = a.reshape(shape[:axis] + (GQA_KV_HEADS, GQA_GROUP, HEAD_DIM) + shape[axis + 1:])
    return jnp.swapaxes(a, axis, axis + 1).reshape(shape)


def kernel(x, c, ctx, c_ctx, w_ada, b_ada, norm_g, ffn1_up, ffn1_down, ffn2_up, ffn2_down,
           w_in, w_out, na_rpb, pool_w, pool_scale, q_norm_g, k_norm_g, final_g):
    bn = x.shape[0]
    depth = w_ada.shape[0]
    assert x.shape == (bn, SEQ, D_MODEL) and ctx.shape == (bn, CTX_LEN, D_MODEL)
    assert bn < ADA_ROWS

    cc = jnp.zeros((ADA_ROWS, D_MODEL), F32).at[:bn].set(c).at[bn].set(c_ctx)
    mods = _ada_call(cc, w_ada, b_ada).reshape(depth, ADA_ROWS, N_MOD, D_MODEL)
    gains = norm_g.reshape(depth, 3, 1, D_MODEL)

    ang = jnp.asarray(_ROPE_ANG)
    cs_lat = jnp.tile(jnp.cos(ang), (1, 2))
    sn_lat = jnp.tile(jnp.sin(ang) * jnp.asarray(_ROPE_SIGN)[None, :], (1, 2))
    rope_cs = jnp.concatenate([cs_lat, jnp.ones((CTX_LEN, 2 * HEAD_DIM), F32)], axis=0)
    rope_sn = jnp.concatenate([sn_lat, jnp.zeros((CTX_LEN, 2 * HEAD_DIM), F32)], axis=0)
    ones_bd = jnp.asarray(
        np.kron(np.eye(GQA_Q_HEADS, dtype=np.float32), np.full((HEAD_DIM, HEAD_DIM), 1.0 / HEAD_DIM,
                                                               np.float32))).astype(BF16)

    w_in_p = jnp.concatenate([w_in[:, :, :OFF_C_Q], _group_major(w_in[:, :, OFF_C_Q:OFF_C_K], 2),
                              w_in[:, :, OFF_C_K:]], axis=-1).astype(BF16)
    w_out_p = jnp.concatenate([w_out[:, :NA_WIDTH + POOL_WIDTH],
                               _group_major(w_out[:, NA_WIDTH + POOL_WIDTH:], 1)], axis=1).astype(BF16)
    up1, dn1 = ffn1_up.astype(BF16), ffn1_down.astype(BF16)
    up2, dn2 = ffn2_up.astype(BF16), ffn2_down.astype(BF16)
    pool_bd = jnp.einsum('gh,lgcd->lgchd', jnp.eye(len(POOL_WINDOWS), dtype=F32), pool_w)
    pool_bd = pool_bd.reshape(depth, POOL_WIDTH, POOL_WIDTH).astype(BF16)
    pool_sc = pool_scale.reshape(depth, 1, POOL_WIDTH)
    qg = jnp.tile(q_norm_g, (1, GQA_Q_HEADS)).reshape(depth, 1, GQA_Q_WIDTH)
    kg = jnp.tile(k_norm_g, (1, GQA_KV_HEADS)).reshape(depth, 1, GQA_KV_WIDTH)
    na_tb = _na_bias_table(na_rpb)
    na_rm = jnp.asarray(_NA_ROW_MASK)

    h = (x, ctx)
    for l in range(depth):
        last = l == depth - 1
        h, qa, ka, va, u, qc, kc, vc_t = _ffn_call(
            h, l, mods, gains, 0, up1, dn1, mod_base=0,
            proj=(w_in_p, rope_cs, rope_sn, qg, kg, ones_bd))
        oa = _na_call(qa, ka, va, l, na_tb, na_rm)
        ob = _pool_call(u, l, pool_bd, pool_sc)
        oc = lax.cond(_gqa_logit_bound(q_norm_g[l], k_norm_g[l]) <= GQA_SAFE_LOGIT,
                      functools.partial(_gqa_call, qc, kc, vc_t, bounded=True),
                      functools.partial(_gqa_call, qc, kc, vc_t, bounded=False))
        h = _ffn_call(h, l, mods, gains, 2, up2, dn2, mod_base=6,
                      mix=(oa, ob, oc, w_out_p), final_g=final_g if last else None)
    return h
```

```python
import functools

import numpy as np
import jax
import jax.numpy as jnp
from jax import lax
from jax.experimental import pallas as pl
from jax.experimental.pallas import tpu as pltpu

D_MODEL = 1024
SEQ = 2048
CTX_LEN = 256
TOK = SEQ + CTX_LEN
GRID_W = 64
GRID_H = SEQ // GRID_W
HEAD_DIM = 64
N_MOD = 9
D_FF = 2816
EPS = 1e-6
NEG_INF = -1e30
NA_HEADS = 4
NA_WIN_H = 8
NA_WIN_W = 16
NA_WIDTH = NA_HEADS * HEAD_DIM
POOL_WINDOWS = (2, 4, 8, 16)
POOL_WIDTH = 256
GQA_Q_HEADS = 8
GQA_KV_HEADS = 2
GQA_GROUP = GQA_Q_HEADS // GQA_KV_HEADS
GQA_Q_WIDTH = GQA_Q_HEADS * HEAD_DIM
GQA_KV_WIDTH = GQA_KV_HEADS * HEAD_DIM
ROPE_THETA = 10000.0
OFF_A_Q = 0
OFF_A_K = OFF_A_Q + NA_WIDTH
OFF_A_V = OFF_A_K + NA_WIDTH
OFF_B_U = OFF_A_V + NA_WIDTH
OFF_C_Q = OFF_B_U + POOL_WIDTH
OFF_C_K = OFF_C_Q + GQA_Q_WIDTH
OFF_C_V = OFF_C_K + GQA_KV_WIDTH
D_IN = OFF_C_V + GQA_KV_WIDTH
D_MIX = NA_WIDTH + POOL_WIDTH + GQA_Q_WIDTH
QK_SCALE = HEAD_DIM ** -0.5
LOG2_E = float(np.log2(np.e))

TM = 256
LAT_TILES = SEQ // TM
STEP = 2 * TM
LAT_STEPS = SEQ // STEP
N_STEPS = LAT_STEPS + 1
NA_QROWS = TM // GRID_W
NA_KROWS = 12
NA_KEYS = NA_KROWS * GRID_W
NA_KPAIRS = NA_KROWS // 2
NA_DY = 2 * NA_WIN_H - 1
NA_DY_PAD = 4
NA_DY_SLOTS = NA_DY + 2 * NA_DY_PAD - 1
NA_PATTERN_GROUPS = (0, 1, 2, LAT_TILES - 1)
GQA_KEY_CHUNK = 768
GQA_SAFE_LOGIT = 64.0
FF_CHUNK = 256
FFN_UP_CHUNK = 64
FFN_DN_CHUNK = D_FF // 8
NORM_BD = 256
ADA_TN = 1152
ADA_ROWS = 16
VMEM_LIMIT = 56 * 1024 * 1024

BF16 = jnp.bfloat16
F32 = jnp.float32


def _dot(a, b):
    return jnp.dot(a, b, preferred_element_type=F32)


def _dot_nt(a, b):
    return lax.dot_general(a, b, (((1,), (1,)), ((), ())), preferred_element_type=F32)


def _rms(x, g):
    ms = jnp.mean(x * x, axis=-1, keepdims=True)
    return x * lax.rsqrt(ms + EPS) * g


def _layer_block(l, shape, *, single_buffer=False):
    nd = len(shape)
    mode = dict(pipeline_mode=pl.Buffered(1)) if single_buffer else {}
    return pl.BlockSpec((None,) + tuple(shape), lambda *_: (l,) + (0,) * nd, **mode)


def _mod_block(l, bn):
    return pl.BlockSpec((None, None, N_MOD, D_MODEL),
                        lambda b, t: (l, b + (t // LAT_STEPS) * (bn - b), 0, 0))


def _gain_block(l, which):
    return pl.BlockSpec((None, None, 1, D_MODEL), lambda b, t: (l, which, 0, 0))


def _token_tile(width):
    return pl.BlockSpec((None, STEP, width), lambda b, t: (b, t, 0))


def _for_each_subtile(body, *, with_context=True):
    t = pl.program_id(1)
    subtiles = [pl.ds(s * TM, TM) for s in range(STEP // TM)]
    if not with_context:
        for rows in subtiles:
            body(rows, None)
        return

    @pl.when(t < LAT_STEPS)
    def _latent():
        for s, rows in enumerate(subtiles):
            body(rows, t * (STEP // TM) + s)

    @pl.when(t == LAT_STEPS)
    def _context():
        body(subtiles[0], None)


def _params(n_axes=2):
    return pltpu.CompilerParams(
        dimension_semantics=("arbitrary",) * n_axes, vmem_limit_bytes=VMEM_LIMIT)


def _ada_kernel(c_ref, w_ref, b_ref, o_ref):
    c = c_ref[...]
    s = c * jax.nn.sigmoid(c)
    o_ref[...] = jnp.dot(s, w_ref[...], preferred_element_type=F32,
                         precision=lax.Precision.HIGHEST) + b_ref[...]


def _ada_call(cc, w_ada, b_ada):
    depth = w_ada.shape[0]
    n = w_ada.shape[2]
    return pl.pallas_call(
        _ada_kernel,
        grid=(depth, n // ADA_TN),
        in_specs=[
            pl.BlockSpec((ADA_ROWS, D_MODEL), lambda l, j: (0, 0)),
            pl.BlockSpec((None, D_MODEL, ADA_TN), lambda l, j: (l, 0, j)),
            pl.BlockSpec((None, 1, ADA_TN), lambda l, j: (l, 0, j)),
        ],
        out_specs=pl.BlockSpec((None, ADA_ROWS, ADA_TN), lambda l, j: (l, 0, j)),
        out_shape=jax.ShapeDtypeStruct((depth, ADA_ROWS, n), F32),
        compiler_params=_params(),
        name="ada",
    )(cc, w_ada, b_ada.reshape(depth, 1, n))


def _stream_cast(src, dst_ref, stage_ref, sem, chunk):
    n_chunks = src.shape[0] // chunk

    def copy(i):
        return pltpu.make_async_copy(src.at[pl.ds(i * chunk, chunk), :], stage_ref.at[i % 2],
                                     sem.at[i % 2])

    copy(0).start()
    for i in range(n_chunks):
        if i + 1 < n_chunks:
            copy(i + 1).start()
        copy(i).wait()
        dst_ref[pl.ds(i * chunk, chunk), :] = stage_ref[i % 2].astype(BF16)


def _ffn_kernel(*refs, layer, mod_base, split_input, with_mix, with_proj, final):
    it = iter(refs)
    h_ref = next(it)
    if split_input:
        ctx_ref = next(it)
    if with_mix:
        oa_ref, ob_ref, oc_ref = next(it), next(it), next(it)
    mod_ref, g_ref = next(it), next(it)
    if with_mix:
        wout_ref = next(it)
    wup_hbm, wdn_hbm = next(it), next(it)
    if final:
        fg_ref = next(it)
    if with_proj:
        proj_in = [next(it) for _ in range(7)]
    o_ref = next(it)
    if with_proj:
        proj_out = [next(it) for _ in range(7)]
    hid_ref, wup_ref, wdn_ref, stage_up, stage_dn, sem_up, sem_dn = (next(it) for _ in range(7))

    @pl.when((pl.program_id(0) == 0) & (pl.program_id(1) == 0))
    def _load_weights():
        _stream_cast(wup_hbm.at[layer], wup_ref, stage_up, sem_up, FFN_UP_CHUNK)
        _stream_cast(wdn_hbm.at[layer], wdn_ref, stage_dn, sem_dn, FFN_DN_CHUNK)

    mod = mod_ref[...]

    def body(rows, lat_tile):
        if split_input and lat_tile is None:
            x = ctx_ref[...]
        else:
            x = h_ref[rows, :]
        if with_mix:
            mixed = jnp.concatenate([oa_ref[rows, :], ob_ref[rows, :], oc_ref[rows, :]], axis=-1)
            x = x + mod[5:6] * _dot(mixed, wout_ref[...])
        y = _rms(x, g_ref[...])
        xn = (y * (1.0 + mod[mod_base + 1:mod_base + 2]) + mod[mod_base:mod_base + 1]).astype(BF16)
        for c in range(D_FF // FF_CHUNK):
            lo = c * FF_CHUNK
            a = _dot(xn, wup_ref[:, lo:lo + FF_CHUNK])
            b = _dot(xn, wup_ref[:, D_FF + lo:D_FF + lo + FF_CHUNK])
            hid_ref[rows, lo:lo + FF_CHUNK] = (a * jax.nn.sigmoid(a) * b).astype(BF16)
        out = x + (0.5 * mod[mod_base + 2:mod_base + 3]) * _dot(hid_ref[rows, :], wdn_ref[...])
        if final:
            out = _rms(out, fg_ref[...])
        o_ref[rows, :] = out
        if with_proj:
            _proj_rows(out, rows, mod, *proj_in, *proj_out)

    _for_each_subtile(body, with_context=not final)


def _ffn_call(h, l, mods, gains, which_gain, w_up, w_dn, *, mod_base, mix=None, proj=None,
              final_g=None):
    split_input = isinstance(h, tuple)
    bn = (h[0] if split_input else h).shape[0]
    with_mix = mix is not None
    with_proj = proj is not None
    final = final_g is not None
    n_steps = LAT_STEPS if final else N_STEPS
    if split_input:
        args = list(h)
        specs = [pl.BlockSpec((None, STEP, D_MODEL), lambda b, t: (b, jnp.minimum(t, LAT_STEPS - 1), 0)),
                 pl.BlockSpec((None, CTX_LEN, D_MODEL), lambda b, t: (b, 0, 0))]
    else:
        args, specs = [h], [_token_tile(D_MODEL)]
    if with_mix:
        oa, ob, oc, w_out = mix
        args += [oa, ob, oc]
        specs += [_token_tile(NA_WIDTH), _token_tile(POOL_WIDTH), _token_tile(GQA_Q_WIDTH)]
    args += [mods, gains]
    specs += [_mod_block(l, bn), _gain_block(l, which_gain)]
    if with_mix:
        args.append(w_out)
        specs.append(_layer_block(l, (D_MIX, D_MODEL), single_buffer=True))
    args += [w_up, w_dn]
    specs += [pl.BlockSpec(memory_space=pl.ANY), pl.BlockSpec(memory_space=pl.ANY)]
    if final:
        args.append(final_g.reshape(1, D_MODEL))
        specs.append(pl.BlockSpec((1, D_MODEL), lambda b, t: (0, 0)))
    out_specs = [_token_tile(D_MODEL)]
    out_shape = [jax.ShapeDtypeStruct((bn, SEQ if final else TOK, D_MODEL), F32)]
    if with_proj:
        w_in, rope_cs, rope_sn, qg, kg, ones_bd = proj
        args += [gains, w_in, rope_cs, rope_sn, qg, kg, ones_bd]
        specs += [
            _gain_block(l, 1),
            _layer_block(l, (D_MODEL, D_IN), single_buffer=True),
            pl.BlockSpec((STEP, 2 * HEAD_DIM), lambda b, t: (t, 0)),
            pl.BlockSpec((STEP, 2 * HEAD_DIM), lambda b, t: (t, 0)),
            _layer_block(l, (1, GQA_Q_WIDTH)),
            _layer_block(l, (1, GQA_KV_WIDTH)),
            pl.BlockSpec((NORM_BD, NORM_BD), lambda b, t: (0, 0)),
        ]
        rows_of = lambda w, dt: jax.ShapeDtypeStruct((bn, TOK, w), dt)
        out_specs += [_token_tile(NA_WIDTH), _token_tile(NA_WIDTH), _token_tile(NA_WIDTH),
                      _token_tile(POOL_WIDTH), _token_tile(GQA_Q_WIDTH), _token_tile(GQA_KV_WIDTH),
                      pl.BlockSpec((None, GQA_KV_WIDTH, STEP), lambda b, t: (b, 0, t))]
        out_shape += [rows_of(NA_WIDTH, BF16), rows_of(NA_WIDTH, BF16), rows_of(NA_WIDTH, BF16),
                      rows_of(POOL_WIDTH, F32), rows_of(GQA_Q_WIDTH, BF16), rows_of(GQA_KV_WIDTH, BF16),
                      jax.ShapeDtypeStruct((bn, GQA_KV_WIDTH, TOK), BF16)]
    outs = pl.pallas_call(
        functools.partial(_ffn_kernel, layer=l, mod_base=mod_base, split_input=split_input,
                          with_mix=with_mix, with_proj=with_proj, final=final),
        grid=(bn, n_steps),
        in_specs=specs,
        out_specs=out_specs,
        out_shape=out_shape,
        scratch_shapes=[pltpu.VMEM((STEP, D_FF), BF16),
                        pltpu.VMEM((D_MODEL, 2 * D_FF), BF16),
                        pltpu.VMEM((D_FF, D_MODEL), BF16),
                        pltpu.VMEM((2, FFN_UP_CHUNK, 2 * D_FF), F32),
                        pltpu.VMEM((2, FFN_DN_CHUNK, D_MODEL), F32),
                        pltpu.SemaphoreType.DMA((2,)),
                        pltpu.SemaphoreType.DMA((2,))],
        compiler_params=_params(),
        name="ffn_mix" if with_mix else "ffn_proj" if with_proj else "ffn",
    )(*args)
    return outs if with_proj else outs[0]


def _head_norm_rope(z, gain, cs, sn, ones_bd):
    width = z.shape[-1]
    zz = z * z
    hi = zz.astype(BF16)
    lo = (zz - hi.astype(F32)).astype(BF16)
    span = min(width, ones_bd.shape[0])
    bd = ones_bd[:span, :span]
    ms = jnp.concatenate([_dot(hi[:, c:c + span], bd) + _dot(lo[:, c:c + span], bd)
                          for c in range(0, width, span)], axis=-1)
    zn = z * lax.rsqrt(ms + EPS) * gain
    lane = lax.broadcasted_iota(jnp.int32, zn.shape, 1)
    partner = jnp.where((lane & 16) != 0,
                        pltpu.roll(zn, 16, axis=1), pltpu.roll(zn, width - 16, axis=1))
    reps = width // cs.shape[-1]
    if reps > 1:
        cs = jnp.concatenate([cs] * reps, axis=-1)
        sn = jnp.concatenate([sn] * reps, axis=-1)
    return zn * cs + partner * sn


def _proj_rows(x, rows, mod, g_ref, win_ref, cs_ref, sn_ref, qg_ref, kg_ref, bd_ref,
               qa_ref, ka_ref, va_ref, u_ref, qc_ref, kc_ref, vct_ref):
    bd = bd_ref[...]
    y = _rms(x, g_ref[...])
    a = (y * (1.0 + mod[4:5]) + mod[3:4]).astype(BF16)
    p = _dot(a, win_ref[...])
    qa_ref[rows, :] = (p[:, OFF_A_Q:OFF_A_K] * (QK_SCALE * LOG2_E)).astype(BF16)
    ka_ref[rows, :] = p[:, OFF_A_K:OFF_A_V].astype(BF16)
    va_ref[rows, :] = p[:, OFF_A_V:OFF_B_U].astype(BF16)
    u_ref[rows, :] = p[:, OFF_B_U:OFF_C_Q]
    cs, sn = cs_ref[rows, :], sn_ref[rows, :]
    qc = _head_norm_rope(p[:, OFF_C_Q:OFF_C_K], qg_ref[...], cs, sn, bd)
    qc_ref[rows, :] = (qc * (QK_SCALE * LOG2_E)).astype(BF16)
    kc = _head_norm_rope(p[:, OFF_C_K:OFF_C_V], kg_ref[...], cs, sn, bd)
    kc_ref[rows, :] = kc.astype(BF16)
    vct_ref[:, rows] = p[:, OFF_C_V:D_IN].T.astype(BF16)


def _softmax_pv(scores, values):
    m = scores[0].max(axis=-1, keepdims=True)
    for s in scores[1:]:
        m = jnp.maximum(m, s.max(axis=-1, keepdims=True))
    den = None
    acc = None
    for s, v in zip(scores, values):
        e = jnp.exp2(s - m)
        d = e.sum(axis=-1, keepdims=True)
        o = _dot(e.astype(BF16), v)
        den = d if den is None else den + d
        acc = o if acc is None else acc + o
    return acc / den


def _na_kernel(q_ref, k_ref, v_ref, tb_ref, rm_ref, o_ref):
    head_of_lane = lax.broadcasted_iota(jnp.int32, (1, NA_WIDTH), 1) // HEAD_DIM
    k_ctx = k_ref[SEQ:TOK, :]
    v_ctx = v_ref[SEQ:TOK, :]

    def context(rows):
        q = q_ref[rows, :]
        out = jnp.zeros((TM, NA_WIDTH), F32)
        for h in range(NA_HEADS):
            mine = head_of_lane == h
            qh = jnp.where(mine, q, jnp.zeros_like(q))
            out = jnp.where(mine, _softmax_pv([_dot_nt(qh, k_ctx)], [v_ctx]), out)
        o_ref[rows, :] = out.astype(BF16)

    def body(rows, g):
        if g is None:
            return context(rows)
        q = q_ref[rows, :]
        first_row = jnp.clip(g * NA_QROWS - NA_WIN_H // 2, 0, GRID_H - NA_KROWS)
        start = pl.multiple_of(first_row * GRID_W, GRID_W)
        k_lat = k_ref[pl.ds(start, NA_KEYS), :]
        v_lat = v_ref[pl.ds(start, NA_KEYS), :]
        pattern = jnp.minimum(g, 2) + (g == LAT_TILES - 1).astype(jnp.int32)
        slot0 = first_row - g * NA_QROWS + NA_WIN_H - 1 + NA_DY_PAD

        def scores(h):
            bias = jnp.concatenate([
                jnp.concatenate([
                    tb_ref[h, slot0 + 2 * m - qr]
                    + rm_ref[pattern, qr * NA_KPAIRS + m:qr * NA_KPAIRS + m + 1, :]
                    for m in range(NA_KPAIRS)], axis=-1)
                for qr in range(NA_QROWS)], axis=0)
            qh = jnp.where(head_of_lane == h, q, jnp.zeros_like(q))
            return [_dot_nt(qh, k_lat) + bias, _dot_nt(qh, k_ctx)]

        out = jnp.zeros((TM, NA_WIDTH), F32)
        s_next = scores(0)
        for h in range(NA_HEADS):
            s_cur = s_next
            if h + 1 < NA_HEADS:
                s_next = scores(h + 1)
            out = jnp.where(head_of_lane == h, _softmax_pv(s_cur, [v_lat, v_ctx]), out)
        o_ref[rows, :] = out.astype(BF16)

    _for_each_subtile(body)


def _na_call(qa, ka, va, l, tb, rm):
    bn = qa.shape[0]
    whole = pl.BlockSpec((None, TOK, NA_WIDTH), lambda b, t: (b, 0, 0))
    return pl.pallas_call(
        _na_kernel,
        grid=(bn, N_STEPS),
        in_specs=[
            _token_tile(NA_WIDTH), whole, whole,
            _layer_block(l, tb.shape[1:], single_buffer=True),
            pl.BlockSpec(rm.shape, lambda b, t: (0, 0, 0)),
        ],
        out_specs=_token_tile(NA_WIDTH),
        out_shape=jax.ShapeDtypeStruct((bn, TOK, NA_WIDTH), BF16),
        compiler_params=_params(),
        name="na",
    )(qa, ka, va, tb, rm)


def _na_static_tables():
    col = np.arange(GRID_W)
    win_c0 = np.clip(col - NA_WIN_W // 2, 0, GRID_W - NA_WIN_W)
    col_ok = (col[None, :] >= win_c0[:, None]) & (col[None, :] < win_c0[:, None] + NA_WIN_W)
    dx = np.clip(col[None, :] - col[:, None], -(NA_WIN_W - 1), NA_WIN_W - 1) + NA_WIN_W - 1
    sel_x = np.eye(2 * NA_WIN_W - 1, dtype=np.float32)[dx]
    row_mask = np.zeros((len(NA_PATTERN_GROUPS), NA_QROWS, NA_KPAIRS, 2, GRID_W), np.float32)
    for p, grp in enumerate(NA_PATTERN_GROUPS):
        first_row = int(np.clip(grp * NA_QROWS - NA_WIN_H // 2, 0, GRID_H - NA_KROWS))
        q_row = grp * NA_QROWS + np.arange(NA_QROWS)
        q_r0 = np.clip(q_row - NA_WIN_H // 2, 0, GRID_H - NA_WIN_H)
        k_row = first_row + np.arange(NA_KROWS)
        row_ok = (k_row[None, :] >= q_r0[:, None]) & (k_row[None, :] < q_r0[:, None] + NA_WIN_H)
        row_mask[p] = np.where(row_ok, 0.0, NEG_INF).reshape(NA_QROWS, NA_KPAIRS, 2, 1)
    row_mask = row_mask.reshape(len(NA_PATTERN_GROUPS), NA_QROWS * NA_KPAIRS, 2 * GRID_W)
    return sel_x, col_ok, row_mask


_NA_SEL_X, _NA_COL_OK, _NA_ROW_MASK = _na_static_tables()


def _na_bias_table(rpb):
    t = jnp.einsum('lhab,xcb->lhaxc', rpb, _NA_SEL_X, precision=lax.Precision.HIGHEST) * LOG2_E
    t = jnp.where(_NA_COL_OK, t, NEG_INF)
    t = jnp.pad(t, ((0, 0), (0, 0), (NA_DY_PAD, NA_DY_PAD), (0, 0), (0, 0)))
    return jnp.concatenate([t[:, :, :-1], t[:, :, 1:]], axis=-1)


def _shift_rows(a, d):
    n = a.shape[0]
    row = lax.broadcasted_iota(jnp.int32, a.shape, 0)
    rolled = pltpu.roll(a, d % n, axis=0)
    ok = (row >= d) if d > 0 else (row < n + d)
    return jnp.where(ok, rolled, 0.0)


def _pool_segment(u, w_bd, scale):
    n = u.shape[0]
    row = lax.broadcasted_iota(jnp.int32, u.shape, 0)
    grp = lax.broadcasted_iota(jnp.int32, u.shape, 1) // (POOL_WIDTH // len(POOL_WINDOWS))
    trail, lead = u, u
    total = jnp.zeros_like(u)
    count = jnp.ones_like(u)
    k = 1
    for i, w in enumerate(POOL_WINDOWS):
        while k < w // 2:
            trail = trail + _shift_rows(trail, k)
            lead = lead + _shift_rows(lead, -k)
            k *= 2
        win = _shift_rows(trail, 1) + lead
        cnt = (jnp.minimum(row + w // 2, n) - jnp.maximum(row - w // 2, 0)).astype(F32)
        total = jnp.where(grp == i, win, total)
        count = jnp.where(grp == i, cnt, count)
    y = (total / count - u).astype(BF16)
    return (_dot(y, w_bd) * scale).astype(BF16)


def _pool_kernel(u_ref, w_ref, s_ref, o_ref):
    w_bd = w_ref[...]
    scale = s_ref[...]
    o_ref[:SEQ, :] = _pool_segment(u_ref[:SEQ, :], w_bd, scale)
    o_ref[SEQ:, :] = _pool_segment(u_ref[SEQ:, :], w_bd, scale)


def _pool_call(u, l, w_bd, scale):
    bn = u.shape[0]
    whole = pl.BlockSpec((None, TOK, POOL_WIDTH), lambda b: (b, 0, 0))
    return pl.pallas_call(
        _pool_kernel,
        grid=(bn,),
        in_specs=[whole, _layer_block(l, (POOL_WIDTH, POOL_WIDTH)), _layer_block(l, (1, POOL_WIDTH))],
        out_specs=whole,
        out_shape=jax.ShapeDtypeStruct((bn, TOK, POOL_WIDTH), BF16),
        compiler_params=_params(1),
        name="pool",
    )(u, w_bd, scale)


def _gqa_tile(q, k_ref, vt_ref, key_chunks, bounded):
    half_of_lane = lax.broadcasted_iota(jnp.int32, (1, GQA_KV_WIDTH), 1) // HEAD_DIM
    stages = [(j, lo, hi) for j in range(GQA_GROUP) for lo, hi in key_chunks]

    def scores_t(stage):
        j, lo, hi = stage
        qb = q[:, j * GQA_KV_WIDTH:(j + 1) * GQA_KV_WIDTH]
        q2 = jnp.concatenate([jnp.where(half_of_lane == kv, qb, jnp.zeros_like(qb))
                              for kv in range(GQA_KV_HEADS)], axis=0)
        return _dot_nt(k_ref[lo:hi, :], q2)

    partial = {j: [] for j in range(GQA_GROUP)}
    s_next = scores_t(stages[0])
    for i, (j, lo, hi) in enumerate(stages):
        s_t = s_next
        if i + 1 < len(stages):
            s_next = scores_t(stages[i + 1])
        if bounded:
            m = None
            e = jnp.exp2(s_t)
        else:
            m = s_t.max(axis=0, keepdims=True)
            e = jnp.exp2(s_t - m)
        den = e.sum(axis=0, keepdims=True)
        e = e.astype(BF16)
        o_t = [_dot(vt_ref[kv * HEAD_DIM:(kv + 1) * HEAD_DIM, lo:hi], e[:, kv * TM:(kv + 1) * TM])
               for kv in range(GQA_KV_HEADS)]
        partial[j].append((m, den, o_t))

    blocks = []
    for j in range(GQA_GROUP):
        if bounded:
            weights = [1.0] * len(partial[j])
        else:
            m_all = functools.reduce(jnp.maximum, [m for m, _, _ in partial[j]])
            weights = [jnp.exp2(m - m_all) for m, _, _ in partial[j]]
        den = sum(w * d for w, (_, d, _) in zip(weights, partial[j]))
        o_t = jnp.concatenate([
            sum((w if bounded else w[:, kv * TM:(kv + 1) * TM]) * o[kv]
                for w, (_, _, o) in zip(weights, partial[j]))
            / den[:, kv * TM:(kv + 1) * TM]
            for kv in range(GQA_KV_HEADS)], axis=0)
        blocks.append(o_t.T)
    return jnp.concatenate(blocks, axis=-1).astype(BF16)


def _gqa_kernel(q_ref, k_ref, vt_ref, o_ref, *, bounded):
    t = pl.program_id(1)

    @pl.when(t < LAT_TILES)
    def _latent():
        chunks = [(lo, lo + GQA_KEY_CHUNK) for lo in range(0, TOK, GQA_KEY_CHUNK)]
        o_ref[...] = _gqa_tile(q_ref[...], k_ref, vt_ref, chunks, bounded)

    @pl.when(t == LAT_TILES)
    def _context():
        o_ref[...] = _gqa_tile(q_ref[...], k_ref, vt_ref, [(SEQ, TOK)], bounded)


def _gqa_logit_bound(q_gain, k_gain):
    return (HEAD_DIM * jnp.max(jnp.abs(q_gain)) * jnp.max(jnp.abs(k_gain))
            * (QK_SCALE * LOG2_E) * (1.0 + 2.0 ** -7) ** 2)


def _gqa_call(qc, kc, vc_t, *, bounded):
    bn = qc.shape[0]
    subtile = pl.BlockSpec((None, TM, GQA_Q_WIDTH), lambda b, t: (b, t, 0))
    return pl.pallas_call(
        functools.partial(_gqa_kernel, bounded=bounded),
        grid=(bn, TOK // TM),
        in_specs=[subtile,
                  pl.BlockSpec((None, TOK, GQA_KV_WIDTH), lambda b, t: (b, 0, 0)),
                  pl.BlockSpec((None, GQA_KV_WIDTH, TOK), lambda b, t: (b, 0, 0))],
        out_specs=subtile,
        out_shape=jax.ShapeDtypeStruct((bn, TOK, GQA_Q_WIDTH), BF16),
        compiler_params=_params(),
        name="gqa_bounded" if bounded else "gqa",
    )(qc, kc, vc_t)


def _rope_tables():
    pos = np.arange(SEQ)
    inv_freq = ROPE_THETA ** (-np.arange(0, HEAD_DIM // 2, 2, dtype=np.float32) / (HEAD_DIM // 2))
    ang_row = (pos // GRID_W).astype(np.float32)[:, None] * inv_freq[None, :].astype(np.float32)
    ang_col = (pos % GRID_W).astype(np.float32)[:, None] * inv_freq[None, :].astype(np.float32)
    ang = np.concatenate([ang_row, ang_row, ang_col, ang_col], axis=-1).astype(np.float32)
    sign = np.tile(np.repeat(np.array([-1.0, 1.0], np.float32), HEAD_DIM // 4), 2)
    return ang, sign


_ROPE_ANG, _ROPE_SIGN = _rope_tables()


def _group_major(a, axis):
    shape = a.shape
    a = a.reshape(shape[:axis] + (GQA_KV_HEADS, GQA_GROUP, HEAD_DIM) + shape[axis + 1:])
    return jnp.swapaxes(a, axis, axis + 1).reshape(shape)


def kernel(x, c, ctx, c_ctx, w_ada, b_ada, norm_g, ffn1_up, ffn1_down, ffn2_up, ffn2_down,
           w_in, w_out, na_rpb, pool_w, pool_scale, q_norm_g, k_norm_g, final_g):
    bn = x.shape[0]
    depth = w_ada.shape[0]
    assert x.shape == (bn, SEQ, D_MODEL) and ctx.shape == (bn, CTX_LEN, D_MODEL)
    assert bn < ADA_ROWS

    cc = jnp.zeros((ADA_ROWS, D_MODEL), F32).at[:bn].set(c).at[bn].set(c_ctx)
    mods = _ada_call(cc, w_ada, b_ada).reshape(depth, ADA_ROWS, N_MOD, D_MODEL)
    gains = norm_g.reshape(depth, 3, 1, D_MODEL)

    ang = jnp.asarray(_ROPE_ANG)
    cs_lat = jnp.tile(jnp.cos(ang), (1, 2))
    sn_lat = jnp.tile(jnp.sin(ang) * jnp.asarray(_ROPE_SIGN)[None, :], (1, 2))
    rope_cs = jnp.concatenate([cs_lat, jnp.ones((CTX_LEN, 2 * HEAD_DIM), F32)], axis=0)
    rope_sn = jnp.concatenate([sn_lat, jnp.zeros((CTX_LEN, 2 * HEAD_DIM), F32)], axis=0)
    ones_bd = jnp.asarray(
        np.kron(np.eye(NORM_BD // HEAD_DIM, dtype=np.float32),
                np.full((HEAD_DIM, HEAD_DIM), 1.0 / HEAD_DIM, np.float32))).astype(BF16)

    w_in_p = jnp.concatenate([w_in[:, :, :OFF_C_Q], _group_major(w_in[:, :, OFF_C_Q:OFF_C_K], 2),
                              w_in[:, :, OFF_C_K:]], axis=-1).astype(BF16)
    w_out_p = jnp.concatenate([w_out[:, :NA_WIDTH + POOL_WIDTH],
                               _group_major(w_out[:, NA_WIDTH + POOL_WIDTH:], 1)], axis=1).astype(BF16)
    pool_bd = jnp.einsum('gh,lgcd->lgchd', jnp.eye(len(POOL_WINDOWS), dtype=F32), pool_w)
    pool_bd = pool_bd.reshape(depth, POOL_WIDTH, POOL_WIDTH).astype(BF16)
    pool_sc = pool_scale.reshape(depth, 1, POOL_WIDTH)
    qg = jnp.tile(q_norm_g, (1, GQA_Q_HEADS)).reshape(depth, 1, GQA_Q_WIDTH)
    kg = jnp.tile(k_norm_g, (1, GQA_KV_HEADS)).reshape(depth, 1, GQA_KV_WIDTH)
    na_tb = _na_bias_table(na_rpb)
    na_rm = jnp.asarray(_NA_ROW_MASK)

    h = (x, ctx)
    for l in range(depth):
        last = l == depth - 1
        h, qa, ka, va, u, qc, kc, vc_t = _ffn_call(
            h, l, mods, gains, 0, ffn1_up, ffn1_down, mod_base=0,
            proj=(w_in_p, rope_cs, rope_sn, qg, kg, ones_bd))
        oa = _na_call(qa, ka, va, l, na_tb, na_rm)
        ob = _pool_call(u, l, pool_bd, pool_sc)
        oc = lax.cond(_gqa_logit_bound(q_norm_g[l], k_norm_g[l]) <= GQA_SAFE_LOGIT,
                      functools.partial(_gqa_call, qc, kc, vc_t, bounded=True),
                      functools.partial(_gqa_call, qc, kc, vc_t, bounded=False))
        h = _ffn_call(h, l, mods, gains, 2, ffn2_up, ffn2_down, mod_base=6,
                      mix=(oa, ob, oc, w_out_p), final_g=final_g if last else None)
    return h
```

```python
import functools

import numpy as np
import jax
import jax.numpy as jnp
from jax import lax
from jax.experimental import pallas as pl
from jax.experimental.pallas import tpu as pltpu

D_MODEL = 1024
SEQ = 2048
CTX_LEN = 256
TOK = SEQ + CTX_LEN
GRID_W = 64
GRID_H = SEQ // GRID_W
HEAD_DIM = 64
N_MOD = 9
D_FF = 2816
EPS = 1e-6
NEG_INF = -1e30
NA_HEADS = 4
NA_WIN_H = 8
NA_WIN_W = 16
NA_WIDTH = NA_HEADS * HEAD_DIM
POOL_WINDOWS = (2, 4, 8, 16)
POOL_WIDTH = 256
POOL_GC = POOL_WIDTH // len(POOL_WINDOWS)
GQA_Q_HEADS = 8
GQA_KV_HEADS = 2
GQA_GROUP = GQA_Q_HEADS // GQA_KV_HEADS
GQA_Q_WIDTH = GQA_Q_HEADS * HEAD_DIM
GQA_KV_WIDTH = GQA_KV_HEADS * HEAD_DIM
ROPE_THETA = 10000.0
OFF_A_Q = 0
OFF_A_K = OFF_A_Q + NA_WIDTH
OFF_A_V = OFF_A_K + NA_WIDTH
OFF_B_U = OFF_A_V + NA_WIDTH
OFF_C_Q = OFF_B_U + POOL_WIDTH
OFF_C_K = OFF_C_Q + GQA_Q_WIDTH
OFF_C_V = OFF_C_K + GQA_KV_WIDTH
D_IN = OFF_C_V + GQA_KV_WIDTH
D_MIX = NA_WIDTH + POOL_WIDTH + GQA_Q_WIDTH
QK_SCALE = HEAD_DIM ** -0.5
LOG2_E = float(np.log2(np.e))

TM = 256
LAT_TILES = SEQ // TM
STEP = 2 * TM
LAT_STEPS = SEQ // STEP
N_STEPS = LAT_STEPS + 1
NA_QROWS = TM // GRID_W
NA_KROWS = 12
NA_KEYS = NA_KROWS * GRID_W
NA_KPAIRS = NA_KROWS // 2
NA_DY = 2 * NA_WIN_H - 1
NA_DY_PAD = 4
NA_DY_SLOTS = NA_DY + 2 * NA_DY_PAD - 1
NA_PATTERN_GROUPS = (0, 1, 2, LAT_TILES - 1)
GQA_KEY_CHUNK = 768
GQA_SAFE_LOGIT = 64.0
FF_CHUNK = 256
FFN_UP_CHUNK = 64
FFN_DN_CHUNK = D_FF // 8
NORM_BD = 256
ADA_TN = 1152
ADA_ROWS = 16
VMEM_LIMIT = 56 * 1024 * 1024

BF16 = jnp.bfloat16
F32 = jnp.float32


def _dot(a, b):
    return jnp.dot(a, b, preferred_element_type=F32)


def _dot_nt(a, b):
    return lax.dot_general(a, b, (((1,), (1,)), ((), ())), preferred_element_type=F32)


def _rms(x, g):
    ms = jnp.mean(x * x, axis=-1, keepdims=True)
    return x * lax.rsqrt(ms + EPS) * g


def _layer_block(l, shape, *, single_buffer=False):
    nd = len(shape)
    mode = dict(pipeline_mode=pl.Buffered(1)) if single_buffer else {}
    return pl.BlockSpec((None,) + tuple(shape), lambda *_: (l,) + (0,) * nd, **mode)


def _mod_block(l, bn):
    return pl.BlockSpec((None, None, N_MOD, D_MODEL),
                        lambda b, t: (l, b + (t // LAT_STEPS) * (bn - b), 0, 0))


def _gain_block(l, which):
    return pl.BlockSpec((None, None, 1, D_MODEL), lambda b, t: (l, which, 0, 0))


def _token_tile(width):
    return pl.BlockSpec((None, STEP, width), lambda b, t: (b, t, 0))


def _for_each_subtile(body, *, with_context=True):
    t = pl.program_id(1)
    subtiles = [pl.ds(s * TM, TM) for s in range(STEP // TM)]
    if not with_context:
        for rows in subtiles:
            body(rows, None)
        return

    @pl.when(t < LAT_STEPS)
    def _latent():
        for s, rows in enumerate(subtiles):
            body(rows, t * (STEP // TM) + s)

    @pl.when(t == LAT_STEPS)
    def _context():
        body(subtiles[0], None)


def _params(n_axes=2):
    return pltpu.CompilerParams(
        dimension_semantics=("arbitrary",) * n_axes, vmem_limit_bytes=VMEM_LIMIT)


def _ada_kernel(c_ref, w_ref, b_ref, o_ref):
    c = c_ref[...]
    s = c * jax.nn.sigmoid(c)
    o_ref[...] = jnp.dot(s, w_ref[...], preferred_element_type=F32,
                         precision=lax.Precision.HIGHEST) + b_ref[...]


def _ada_call(cc, w_ada, b_ada):
    depth = w_ada.shape[0]
    n = w_ada.shape[2]
    return pl.pallas_call(
        _ada_kernel,
        grid=(depth, n // ADA_TN),
        in_specs=[
            pl.BlockSpec((ADA_ROWS, D_MODEL), lambda l, j: (0, 0)),
            pl.BlockSpec((None, D_MODEL, ADA_TN), lambda l, j: (l, 0, j)),
            pl.BlockSpec((None, 1, ADA_TN), lambda l, j: (l, 0, j)),
        ],
        out_specs=pl.BlockSpec((None, ADA_ROWS, ADA_TN), lambda l, j: (l, 0, j)),
        out_shape=jax.ShapeDtypeStruct((depth, ADA_ROWS, n), F32),
        compiler_params=_params(),
        name="ada",
    )(cc, w_ada, b_ada.reshape(depth, 1, n))


def _stream_cast(src, dst_ref, stage_ref, sem, chunk):
    n_chunks = src.shape[0] // chunk

    def copy(i):
        return pltpu.make_async_copy(src.at[pl.ds(i * chunk, chunk), :], stage_ref.at[i % 2],
                                     sem.at[i % 2])

    copy(0).start()
    for i in range(n_chunks):
        if i + 1 < n_chunks:
            copy(i + 1).start()
        copy(i).wait()
        dst_ref[pl.ds(i * chunk, chunk), :] = stage_ref[i % 2].astype(BF16)


def _ffn_kernel(*refs, layer, mod_base, split_input, with_mix, with_proj, final):
    it = iter(refs)
    h_ref = next(it)
    if split_input:
        ctx_ref = next(it)
    if with_mix:
        oa_ref, ob_ref, oc_ref = next(it), next(it), next(it)
    mod_ref, g_ref = next(it), next(it)
    if with_mix:
        wout_ref = next(it)
    wup_hbm, wdn_hbm = next(it), next(it)
    if final:
        fg_ref = next(it)
    if with_proj:
        proj_in = [next(it) for _ in range(7)]
    o_ref = next(it)
    if with_proj:
        proj_out = [next(it) for _ in range(7)]
    hid_ref, wup_ref, wdn_ref, stage_up, stage_dn, sem_up, sem_dn = (next(it) for _ in range(7))

    @pl.when((pl.program_id(0) == 0) & (pl.program_id(1) == 0))
    def _load_weights():
        _stream_cast(wup_hbm.at[layer], wup_ref, stage_up, sem_up, FFN_UP_CHUNK)
        _stream_cast(wdn_hbm.at[layer], wdn_ref, stage_dn, sem_dn, FFN_DN_CHUNK)

    mod = mod_ref[...]

    def body(rows, lat_tile):
        if split_input and lat_tile is None:
            x = ctx_ref[...]
        else:
            x = h_ref[rows, :]
        if with_mix:
            mixed = jnp.concatenate([oa_ref[rows, :], ob_ref[rows, :], oc_ref[rows, :]], axis=-1)
            x = x + mod[5:6] * _dot(mixed, wout_ref[...])
        y = _rms(x, g_ref[...])
        xn = (y * (1.0 + mod[mod_base + 1:mod_base + 2]) + mod[mod_base:mod_base + 1]).astype(BF16)
        for c in range(D_FF // FF_CHUNK):
            lo = c * FF_CHUNK
            a = _dot(xn, wup_ref[:, lo:lo + FF_CHUNK])
            b = _dot(xn, wup_ref[:, D_FF + lo:D_FF + lo + FF_CHUNK])
            hid_ref[rows, lo:lo + FF_CHUNK] = (a * jax.nn.sigmoid(a) * b).astype(BF16)
        out = x + (0.5 * mod[mod_base + 2:mod_base + 3]) * _dot(hid_ref[rows, :], wdn_ref[...])
        if final:
            out = _rms(out, fg_ref[...])
        o_ref[rows, :] = out
        if with_proj:
            _proj_rows(out, rows, mod, *proj_in, *proj_out)

    _for_each_subtile(body, with_context=not final)


def _ffn_call(h, l, mods, gains, which_gain, w_up, w_dn, *, mod_base, mix=None, proj=None,
              final_g=None):
    split_input = isinstance(h, tuple)
    bn = (h[0] if split_input else h).shape[0]
    with_mix = mix is not None
    with_proj = proj is not None
    final = final_g is not None
    n_steps = LAT_STEPS if final else N_STEPS
    if split_input:
        args = list(h)
        specs = [pl.BlockSpec((None, STEP, D_MODEL), lambda b, t: (b, jnp.minimum(t, LAT_STEPS - 1), 0)),
                 pl.BlockSpec((None, CTX_LEN, D_MODEL), lambda b, t: (b, 0, 0))]
    else:
        args, specs = [h], [_token_tile(D_MODEL)]
    if with_mix:
        oa, ob, oc, w_out = mix
        args += [oa, ob, oc]
        specs += [_token_tile(NA_WIDTH), _token_tile(POOL_WIDTH), _token_tile(GQA_Q_WIDTH)]
    args += [mods, gains]
    specs += [_mod_block(l, bn), _gain_block(l, which_gain)]
    if with_mix:
        args.append(w_out)
        specs.append(_layer_block(l, (D_MIX, D_MODEL), single_buffer=True))
    args += [w_up, w_dn]
    specs += [pl.BlockSpec(memory_space=pl.ANY), pl.BlockSpec(memory_space=pl.ANY)]
    if final:
        args.append(final_g.reshape(1, D_MODEL))
        specs.append(pl.BlockSpec((1, D_MODEL), lambda b, t: (0, 0)))
    out_specs = [_token_tile(D_MODEL)]
    out_shape = [jax.ShapeDtypeStruct((bn, SEQ if final else TOK, D_MODEL), F32)]
    if with_proj:
        w_in, rope_cs, rope_sn, qg, kg, ones_bd = proj
        args += [gains, w_in, rope_cs, rope_sn, qg, kg, ones_bd]
        specs += [
            _gain_block(l, 1),
            _layer_block(l, (D_MODEL, D_IN), single_buffer=True),
            pl.BlockSpec((STEP, 2 * HEAD_DIM), lambda b, t: (t, 0)),
            pl.BlockSpec((STEP, 2 * HEAD_DIM), lambda b, t: (t, 0)),
            _layer_block(l, (1, GQA_Q_WIDTH)),
            _layer_block(l, (1, GQA_KV_WIDTH)),
            pl.BlockSpec((NORM_BD, NORM_BD), lambda b, t: (0, 0)),
        ]
        rows_of = lambda w, dt: jax.ShapeDtypeStruct((bn, TOK, w), dt)
        out_specs += [_token_tile(NA_WIDTH), _token_tile(NA_WIDTH), _token_tile(NA_WIDTH),
                      _token_tile(POOL_WIDTH), _token_tile(GQA_Q_WIDTH), _token_tile(GQA_KV_WIDTH),
                      pl.BlockSpec((None, GQA_KV_WIDTH, STEP), lambda b, t: (b, 0, t))]
        out_shape += [rows_of(NA_WIDTH, BF16), rows_of(NA_WIDTH, BF16), rows_of(NA_WIDTH, BF16),
                      rows_of(POOL_WIDTH, F32), rows_of(GQA_Q_WIDTH, BF16), rows_of(GQA_KV_WIDTH, BF16),
                      jax.ShapeDtypeStruct((bn, GQA_KV_WIDTH, TOK), BF16)]
    outs = pl.pallas_call(
        functools.partial(_ffn_kernel, layer=l, mod_base=mod_base, split_input=split_input,
                          with_mix=with_mix, with_proj=with_proj, final=final),
        grid=(bn, n_steps),
        in_specs=specs,
        out_specs=out_specs,
        out_shape=out_shape,
        scratch_shapes=[pltpu.VMEM((STEP, D_FF), BF16),
                        pltpu.VMEM((D_MODEL, 2 * D_FF), BF16),
                        pltpu.VMEM((D_FF, D_MODEL), BF16),
                        pltpu.VMEM((2, FFN_UP_CHUNK, 2 * D_FF), F32),
                        pltpu.VMEM((2, FFN_DN_CHUNK, D_MODEL), F32),
                        pltpu.SemaphoreType.DMA((2,)),
                        pltpu.SemaphoreType.DMA((2,))],
        compiler_params=_params(),
        name="ffn_mix" if with_mix else "ffn_proj" if with_proj else "ffn",
    )(*args)
    return outs if with_proj else outs[0]


def _head_norm_rope(z, gain, cs, sn, ones_bd):
    width = z.shape[-1]
    zz = z * z
    hi = zz.astype(BF16)
    lo = (zz - hi.astype(F32)).astype(BF16)
    span = min(width, ones_bd.shape[0])
    bd = ones_bd[:span, :span]
    ms = jnp.concatenate([_dot(hi[:, c:c + span], bd) + _dot(lo[:, c:c + span], bd)
                          for c in range(0, width, span)], axis=-1)
    zn = z * lax.rsqrt(ms + EPS) * gain
    lane = lax.broadcasted_iota(jnp.int32, zn.shape, 1)
    partner = jnp.where((lane & 16) != 0,
                        pltpu.roll(zn, 16, axis=1), pltpu.roll(zn, width - 16, axis=1))
    reps = width // cs.shape[-1]
    if reps > 1:
        cs = jnp.concatenate([cs] * reps, axis=-1)
        sn = jnp.concatenate([sn] * reps, axis=-1)
    return zn * cs + partner * sn


def _proj_rows(x, rows, mod, g_ref, win_ref, cs_ref, sn_ref, qg_ref, kg_ref, bd_ref,
               qa_ref, ka_ref, va_ref, u_ref, qc_ref, kc_ref, vct_ref):
    bd = bd_ref[...]
    y = _rms(x, g_ref[...])
    a = (y * (1.0 + mod[4:5]) + mod[3:4]).astype(BF16)
    p = _dot(a, win_ref[...])
    qa_ref[rows, :] = (p[:, OFF_A_Q:OFF_A_K] * (QK_SCALE * LOG2_E)).astype(BF16)
    ka_ref[rows, :] = p[:, OFF_A_K:OFF_A_V].astype(BF16)
    va_ref[rows, :] = p[:, OFF_A_V:OFF_B_U].astype(BF16)
    u_ref[rows, :] = p[:, OFF_B_U:OFF_C_Q]
    cs, sn = cs_ref[rows, :], sn_ref[rows, :]
    qc = _head_norm_rope(p[:, OFF_C_Q:OFF_C_K], qg_ref[...], cs, sn, bd)
    qc_ref[rows, :] = (qc * (QK_SCALE * LOG2_E)).astype(BF16)
    kc = _head_norm_rope(p[:, OFF_C_K:OFF_C_V], kg_ref[...], cs, sn, bd)
    kc_ref[rows, :] = kc.astype(BF16)
    vct_ref[:, rows] = p[:, OFF_C_V:D_IN].T.astype(BF16)


def _softmax_pv(scores, values):
    m = scores[0].max(axis=-1, keepdims=True)
    for s in scores[1:]:
        m = jnp.maximum(m, s.max(axis=-1, keepdims=True))
    den = None
    acc = None
    for s, v in zip(scores, values):
        e = jnp.exp2(s - m)
        d = e.sum(axis=-1, keepdims=True)
        o = _dot(e.astype(BF16), v)
        den = d if den is None else den + d
        acc = o if acc is None else acc + o
    return acc / den


def _na_kernel(q_ref, k_ref, v_ref, tb_ref, rm_ref, o_ref):
    head_of_lane = lax.broadcasted_iota(jnp.int32, (1, NA_WIDTH), 1) // HEAD_DIM
    k_ctx = k_ref[SEQ:TOK, :]
    v_ctx = v_ref[SEQ:TOK, :]

    def context(rows):
        q = q_ref[rows, :]
        out = jnp.zeros((TM, NA_WIDTH), F32)
        for h in range(NA_HEADS):
            mine = head_of_lane == h
            qh = jnp.where(mine, q, jnp.zeros_like(q))
            out = jnp.where(mine, _softmax_pv([_dot_nt(qh, k_ctx)], [v_ctx]), out)
        o_ref[rows, :] = out.astype(BF16)

    def body(rows, g):
        if g is None:
            return context(rows)
        q = q_ref[rows, :]
        first_row = jnp.clip(g * NA_QROWS - NA_WIN_H // 2, 0, GRID_H - NA_KROWS)
        start = pl.multiple_of(first_row * GRID_W, GRID_W)
        k_lat = k_ref[pl.ds(start, NA_KEYS), :]
        v_lat = v_ref[pl.ds(start, NA_KEYS), :]
        pattern = jnp.minimum(g, 2) + (g == LAT_TILES - 1).astype(jnp.int32)
        slot0 = first_row - g * NA_QROWS + NA_WIN_H - 1 + NA_DY_PAD

        def scores(h):
            bias = jnp.concatenate([
                jnp.concatenate([
                    tb_ref[h, slot0 + 2 * m - qr]
                    + rm_ref[pattern, qr * NA_KPAIRS + m:qr * NA_KPAIRS + m + 1, :]
                    for m in range(NA_KPAIRS)], axis=-1)
                for qr in range(NA_QROWS)], axis=0)
            qh = jnp.where(head_of_lane == h, q, jnp.zeros_like(q))
            return [_dot_nt(qh, k_lat) + bias, _dot_nt(qh, k_ctx)]

        out = jnp.zeros((TM, NA_WIDTH), F32)
        s_next = scores(0)
        for h in range(NA_HEADS):
            s_cur = s_next
            if h + 1 < NA_HEADS:
                s_next = scores(h + 1)
            out = jnp.where(head_of_lane == h, _softmax_pv(s_cur, [v_lat, v_ctx]), out)
        o_ref[rows, :] = out.astype(BF16)

    _for_each_subtile(body)


def _na_call(qa, ka, va, l, tb, rm):
    bn = qa.shape[0]
    whole = pl.BlockSpec((None, TOK, NA_WIDTH), lambda b, t: (b, 0, 0))
    return pl.pallas_call(
        _na_kernel,
        grid=(bn, N_STEPS),
        in_specs=[
            _token_tile(NA_WIDTH), whole, whole,
            _layer_block(l, tb.shape[1:], single_buffer=True),
            pl.BlockSpec(rm.shape, lambda b, t: (0, 0, 0)),
        ],
        out_specs=_token_tile(NA_WIDTH),
        out_shape=jax.ShapeDtypeStruct((bn, TOK, NA_WIDTH), BF16),
        compiler_params=_params(),
        name="na",
    )(qa, ka, va, tb, rm)


def _na_static_tables():
    col = np.arange(GRID_W)
    win_c0 = np.clip(col - NA_WIN_W // 2, 0, GRID_W - NA_WIN_W)
    col_ok = (col[None, :] >= win_c0[:, None]) & (col[None, :] < win_c0[:, None] + NA_WIN_W)
    dx = np.clip(col[None, :] - col[:, None], -(NA_WIN_W - 1), NA_WIN_W - 1) + NA_WIN_W - 1
    sel_x = np.eye(2 * NA_WIN_W - 1, dtype=np.float32)[dx]
    row_mask = np.zeros((len(NA_PATTERN_GROUPS), NA_QROWS, NA_KPAIRS, 2, GRID_W), np.float32)
    for p, grp in enumerate(NA_PATTERN_GROUPS):
        first_row = int(np.clip(grp * NA_QROWS - NA_WIN_H // 2, 0, GRID_H - NA_KROWS))
        q_row = grp * NA_QROWS + np.arange(NA_QROWS)
        q_r0 = np.clip(q_row - NA_WIN_H // 2, 0, GRID_H - NA_WIN_H)
        k_row = first_row + np.arange(NA_KROWS)
        row_ok = (k_row[None, :] >= q_r0[:, None]) & (k_row[None, :] < q_r0[:, None] + NA_WIN_H)
        row_mask[p] = np.where(row_ok, 0.0, NEG_INF).reshape(NA_QROWS, NA_KPAIRS, 2, 1)
    row_mask = row_mask.reshape(len(NA_PATTERN_GROUPS), NA_QROWS * NA_KPAIRS, 2 * GRID_W)
    return sel_x, col_ok, row_mask


_NA_SEL_X, _NA_COL_OK, _NA_ROW_MASK = _na_static_tables()


def _na_bias_table(rpb):
    t = jnp.einsum('lhab,xcb->lhaxc', rpb, _NA_SEL_X, precision=lax.Precision.HIGHEST) * LOG2_E
    t = jnp.where(_NA_COL_OK, t, NEG_INF)
    t = jnp.pad(t, ((0, 0), (0, 0), (NA_DY_PAD, NA_DY_PAD), (0, 0), (0, 0)))
    return jnp.concatenate([t[:, :, :-1], t[:, :, 1:]], axis=-1)


def _shift_rows(a, d):
    n = a.shape[0]
    row = lax.broadcasted_iota(jnp.int32, a.shape, 0)
    rolled = pltpu.roll(a, d % n, axis=0)
    ok = (row >= d) if d > 0 else (row < n + d)
    return jnp.where(ok, rolled, 0.0)


def _pool_lanes(u, windows):
    n = u.shape[0]
    row = lax.broadcasted_iota(jnp.int32, u.shape, 0)
    grp = lax.broadcasted_iota(jnp.int32, u.shape, 1) // POOL_GC
    trail, lead = u, u
    total = jnp.zeros_like(u)
    count = jnp.ones_like(u)
    k = 1
    for i, w in enumerate(windows):
        while k < w // 2:
            trail = trail + _shift_rows(trail, k)
            lead = lead + _shift_rows(lead, -k)
            k *= 2
        win = _shift_rows(trail, 1) + lead
        cnt = (jnp.minimum(row + w // 2, n) - jnp.maximum(row - w // 2, 0)).astype(F32)
        total = jnp.where(grp == i, win, total)
        count = jnp.where(grp == i, cnt, count)
    return total / count - u


def _pool_segment(u, w_bd, scale):
    half = 2 * POOL_GC
    y = jnp.concatenate([_pool_lanes(u[:, c:c + half], POOL_WINDOWS[c // POOL_GC:(c + half) // POOL_GC])
                         for c in range(0, POOL_WIDTH, half)], axis=-1).astype(BF16)
    return (_dot(y, w_bd) * scale).astype(BF16)


def _pool_kernel(u_ref, w_ref, s_ref, o_ref):
    w_bd = w_ref[...]
    scale = s_ref[...]
    o_ref[:SEQ, :] = _pool_segment(u_ref[:SEQ, :], w_bd, scale)
    o_ref[SEQ:, :] = _pool_segment(u_ref[SEQ:, :], w_bd, scale)


def _pool_call(u, l, w_bd, scale):
    bn = u.shape[0]
    whole = pl.BlockSpec((None, TOK, POOL_WIDTH), lambda b: (b, 0, 0))
    return pl.pallas_call(
        _pool_kernel,
        grid=(bn,),
        in_specs=[whole, _layer_block(l, (POOL_WIDTH, POOL_WIDTH)), _layer_block(l, (1, POOL_WIDTH))],
        out_specs=whole,
        out_shape=jax.ShapeDtypeStruct((bn, TOK, POOL_WIDTH), BF16),
        compiler_params=_params(1),
        name="pool",
    )(u, w_bd, scale)


def _gqa_tile(q, k_ref, vt_ref, key_chunks, bounded):
    half_of_lane = lax.broadcasted_iota(jnp.int32, (1, GQA_KV_WIDTH), 1) // HEAD_DIM
    stages = [(j, lo, hi) for j in range(GQA_GROUP) for lo, hi in key_chunks]

    def scores_t(stage):
        j, lo, hi = stage
        qb = q[:, j * GQA_KV_WIDTH:(j + 1) * GQA_KV_WIDTH]
        q2 = jnp.concatenate([jnp.where(half_of_lane == kv, qb, jnp.zeros_like(qb))
                              for kv in range(GQA_KV_HEADS)], axis=0)
        return _dot_nt(k_ref[lo:hi, :], q2)

    partial = {j: [] for j in range(GQA_GROUP)}
    s_next = scores_t(stages[0])
    for i, (j, lo, hi) in enumerate(stages):
        s_t = s_next
        if i + 1 < len(stages):
            s_next = scores_t(stages[i + 1])
        if bounded:
            m = None
            e = jnp.exp2(s_t)
        else:
            m = s_t.max(axis=0, keepdims=True)
            e = jnp.exp2(s_t - m)
        den = e.sum(axis=0, keepdims=True)
        e = e.astype(BF16)
        o_t = [_dot(vt_ref[kv * HEAD_DIM:(kv + 1) * HEAD_DIM, lo:hi], e[:, kv * TM:(kv + 1) * TM])
               for kv in range(GQA_KV_HEADS)]
        partial[j].append((m, den, o_t))

    blocks = []
    for j in range(GQA_GROUP):
        if bounded:
            weights = [1.0] * len(partial[j])
        else:
            m_all = functools.reduce(jnp.maximum, [m for m, _, _ in partial[j]])
            weights = [jnp.exp2(m - m_all) for m, _, _ in partial[j]]
        den = sum(w * d for w, (_, d, _) in zip(weights, partial[j]))
        o_t = jnp.concatenate([
            sum((w if bounded else w[:, kv * TM:(kv + 1) * TM]) * o[kv]
                for w, (_, _, o) in zip(weights, partial[j]))
            / den[:, kv * TM:(kv + 1) * TM]
            for kv in range(GQA_KV_HEADS)], axis=0)
        blocks.append(o_t.T)
    return jnp.concatenate(blocks, axis=-1).astype(BF16)


def _gqa_kernel(q_ref, k_ref, vt_ref, o_ref, *, bounded):
    def body(rows, lat_tile):
        if lat_tile is None:
            chunks = [(SEQ, TOK)]
        else:
            chunks = [(lo, lo + GQA_KEY_CHUNK) for lo in range(0, TOK, GQA_KEY_CHUNK)]
        o_ref[rows, :] = _gqa_tile(q_ref[rows, :], k_ref, vt_ref, chunks, bounded)

    _for_each_subtile(body)


def _gqa_logit_bound(q_gain, k_gain):
    return (HEAD_DIM * jnp.max(jnp.abs(q_gain)) * jnp.max(jnp.abs(k_gain))
            * (QK_SCALE * LOG2_E) * (1.0 + 2.0 ** -7) ** 2)


def _gqa_call(qc, kc, vc_t, *, bounded):
    bn = qc.shape[0]
    return pl.pallas_call(
        functools.partial(_gqa_kernel, bounded=bounded),
        grid=(bn, N_STEPS),
        in_specs=[_token_tile(GQA_Q_WIDTH),
                  pl.BlockSpec((None, TOK, GQA_KV_WIDTH), lambda b, t: (b, 0, 0)),
                  pl.BlockSpec((None, GQA_KV_WIDTH, TOK), lambda b, t: (b, 0, 0))],
        out_specs=_token_tile(GQA_Q_WIDTH),
        out_shape=jax.ShapeDtypeStruct((bn, TOK, GQA_Q_WIDTH), BF16),
        compiler_params=_params(),
        name="gqa_bounded" if bounded else "gqa",
    )(qc, kc, vc_t)


def _rope_tables():
    pos = np.arange(SEQ)
    inv_freq = ROPE_THETA ** (-np.arange(0, HEAD_DIM // 2, 2, dtype=np.float32) / (HEAD_DIM // 2))
    ang_row = (pos // GRID_W).astype(np.float32)[:, None] * inv_freq[None, :].astype(np.float32)
    ang_col = (pos % GRID_W).astype(np.float32)[:, None] * inv_freq[None, :].astype(np.float32)
    ang = np.concatenate([ang_row, ang_row, ang_col, ang_col], axis=-1).astype(np.float32)
    sign = np.tile(np.repeat(np.array([-1.0, 1.0], np.float32), HEAD_DIM // 4), 2)
    return ang, sign


_ROPE_ANG, _ROPE_SIGN = _rope_tables()


def _group_major(a, axis):
    shape = a.shape
    a = a.reshape(shape[:axis] + (GQA_KV_HEADS, GQA_GROUP, HEAD_DIM) + shape[axis + 1:])
    return jnp.swapaxes(a, axis, axis + 1).reshape(shape)


def kernel(x, c, ctx, c_ctx, w_ada, b_ada, norm_g, ffn1_up, ffn1_down, ffn2_up, ffn2_down,
           w_in, w_out, na_rpb, pool_w, pool_scale, q_norm_g, k_norm_g, final_g):
    bn = x.shape[0]
    depth = w_ada.shape[0]
    assert x.shape == (bn, SEQ, D_MODEL) and ctx.shape == (bn, CTX_LEN, D_MODEL)
    assert bn < ADA_ROWS

    cc = jnp.zeros((ADA_ROWS, D_MODEL), F32).at[:bn].set(c).at[bn].set(c_ctx)
    mods = _ada_call(cc, w_ada, b_ada).reshape(depth, ADA_ROWS, N_MOD, D_MODEL)
    gains = norm_g.reshape(depth, 3, 1, D_MODEL)

    ang = jnp.asarray(_ROPE_ANG)
    cs_lat = jnp.tile(jnp.cos(ang), (1, 2))
    sn_lat = jnp.tile(jnp.sin(ang) * jnp.asarray(_ROPE_SIGN)[None, :], (1, 2))
    rope_cs = jnp.concatenate([cs_lat, jnp.ones((CTX_LEN, 2 * HEAD_DIM), F32)], axis=0)
    rope_sn = jnp.concatenate([sn_lat, jnp.zeros((CTX_LEN, 2 * HEAD_DIM), F32)], axis=0)
    ones_bd = jnp.asarray(
        np.kron(np.eye(NORM_BD // HEAD_DIM, dtype=np.float32),
                np.full((HEAD_DIM, HEAD_DIM), 1.0 / HEAD_DIM, np.float32))).astype(BF16)

    w_in_p = jnp.concatenate([w_in[:, :, :OFF_C_Q], _group_major(w_in[:, :, OFF_C_Q:OFF_C_K], 2),
                              w_in[:, :, OFF_C_K:]], axis=-1).astype(BF16)
    w_out_p = jnp.concatenate([w_out[:, :NA_WIDTH + POOL_WIDTH],
                               _group_major(w_out[:, NA_WIDTH + POOL_WIDTH:], 1)], axis=1).astype(BF16)
    pool_bd = jnp.einsum('gh,lgcd->lgchd', jnp.eye(len(POOL_WINDOWS), dtype=F32), pool_w)
    pool_bd = pool_bd.reshape(depth, POOL_WIDTH, POOL_WIDTH).astype(BF16)
    pool_sc = pool_scale.reshape(depth, 1, POOL_WIDTH)
    qg = jnp.tile(q_norm_g, (1, GQA_Q_HEADS)).reshape(depth, 1, GQA_Q_WIDTH)
    kg = jnp.tile(k_norm_g, (1, GQA_KV_HEADS)).reshape(depth, 1, GQA_KV_WIDTH)
    na_tb = _na_bias_table(na_rpb)
    na_rm = jnp.asarray(_NA_ROW_MASK)

    h = (x, ctx)
    for l in range(depth):
        last = l == depth - 1
        h, qa, ka, va, u, qc, kc, vc_t = _ffn_call(
            h, l, mods, gains, 0, ffn1_up, ffn1_down, mod_base=0,
            proj=(w_in_p, rope_cs, rope_sn, qg, kg, ones_bd))
        oa = _na_call(qa, ka, va, l, na_tb, na_rm)
        ob = _pool_call(u, l, pool_bd, pool_sc)
        oc = lax.cond(_gqa_logit_bound(q_norm_g[l], k_norm_g[l]) <= GQA_SAFE_LOGIT,
                      functools.partial(_gqa_call, qc, kc, vc_t, bounded=True),
                      functools.partial(_gqa_call, qc, kc, vc_t, bounded=False))
        h = _ffn_call(h, l, mods, gains, 2, ffn2_up, ffn2_down, mod_base=6,
                      mix=(oa, ob, oc, w_out_p), final_g=final_g if last else None)
    return h
```

```python
import functools

import numpy as np
import jax
import jax.numpy as jnp
from jax import lax
from jax.experimental import pallas as pl
from jax.experimental.pallas import tpu as pltpu

D_MODEL = 1024
SEQ = 2048
CTX_LEN = 256
TOK = SEQ + CTX_LEN
GRID_W = 64
GRID_H = SEQ // GRID_W
HEAD_DIM = 64
N_MOD = 9
D_FF = 2816
EPS = 1e-6
NEG_INF = -1e30
NA_HEADS = 4
NA_WIN_H = 8
NA_WIN_W = 16
NA_WIDTH = NA_HEADS * HEAD_DIM
POOL_WINDOWS = (2, 4, 8, 16)
POOL_WIDTH = 256
POOL_GC = POOL_WIDTH // len(POOL_WINDOWS)
GQA_Q_HEADS = 8
GQA_KV_HEADS = 2
GQA_GROUP = GQA_Q_HEADS // GQA_KV_HEADS
GQA_Q_WIDTH = GQA_Q_HEADS * HEAD_DIM
GQA_KV_WIDTH = GQA_KV_HEADS * HEAD_DIM
ROPE_THETA = 10000.0
OFF_A_Q = 0
OFF_A_K = OFF_A_Q + NA_WIDTH
OFF_A_V = OFF_A_K + NA_WIDTH
OFF_B_U = OFF_A_V + NA_WIDTH
OFF_C_Q = OFF_B_U + POOL_WIDTH
OFF_C_K = OFF_C_Q + GQA_Q_WIDTH
OFF_C_V = OFF_C_K + GQA_KV_WIDTH
D_IN = OFF_C_V + GQA_KV_WIDTH
D_MIX = NA_WIDTH + POOL_WIDTH + GQA_Q_WIDTH
QK_SCALE = HEAD_DIM ** -0.5
LOG2_E = float(np.log2(np.e))

TM = 256
LAT_TILES = SEQ // TM
STEP = 2 * TM
LAT_STEPS = SEQ // STEP
N_STEPS = LAT_STEPS + 1
NA_QROWS = TM // GRID_W
NA_KROWS = 12
NA_KEYS = NA_KROWS * GRID_W
NA_KPAIRS = NA_KROWS // 2
NA_DY = 2 * NA_WIN_H - 1
NA_DY_PAD = 4
NA_DY_SLOTS = NA_DY + 2 * NA_DY_PAD - 1
NA_PATTERN_GROUPS = (0, 1, 2, LAT_TILES - 1)
GQA_KEY_CHUNK = 768
GQA_SAFE_LOGIT = 64.0
FF_CHUNK = 256
FFN_UP_CHUNK = 64
FFN_DN_CHUNK = D_FF // 8
NORM_BD = 256
ADA_TN = 1152
ADA_ROWS = 16
VMEM_LIMIT = 56 * 1024 * 1024

BF16 = jnp.bfloat16
F32 = jnp.float32


def _dot(a, b):
    return jnp.dot(a, b, preferred_element_type=F32)


def _dot_nt(a, b):
    return lax.dot_general(a, b, (((1,), (1,)), ((), ())), preferred_element_type=F32)


def _rms(x, g):
    ms = jnp.mean(x * x, axis=-1, keepdims=True)
    return x * lax.rsqrt(ms + EPS) * g


def _layer_block(l, shape, *, single_buffer=False):
    nd = len(shape)
    mode = dict(pipeline_mode=pl.Buffered(1)) if single_buffer else {}
    return pl.BlockSpec((None,) + tuple(shape), lambda *_: (l,) + (0,) * nd, **mode)


def _mod_block(l, bn):
    return pl.BlockSpec((None, None, N_MOD, D_MODEL),
                        lambda b, t: (l, b + (t // LAT_STEPS) * (bn - b), 0, 0))


def _gain_block(l, which):
    return pl.BlockSpec((None, None, 1, D_MODEL), lambda b, t: (l, which, 0, 0))


def _token_tile(width):
    return pl.BlockSpec((None, STEP, width), lambda b, t: (b, t, 0))


def _for_each_subtile(body, *, with_context=True):
    t = pl.program_id(1)
    subtiles = [pl.ds(s * TM, TM) for s in range(STEP // TM)]

    def latent():
        for s, rows in enumerate(subtiles):
            body(rows, t * (STEP // TM) + s)

    if not with_context:
        latent()
        return
    pl.when(t < LAT_STEPS)(latent)

    @pl.when(t == LAT_STEPS)
    def _context():
        body(subtiles[0], None)


def _params(n_axes=2):
    return pltpu.CompilerParams(
        dimension_semantics=("arbitrary",) * n_axes, vmem_limit_bytes=VMEM_LIMIT)


def _ada_kernel(c_ref, w_ref, b_ref, o_ref):
    c = c_ref[...]
    s = c * jax.nn.sigmoid(c)
    o_ref[...] = jnp.dot(s, w_ref[...], preferred_element_type=F32,
                         precision=lax.Precision.HIGHEST) + b_ref[...]


def _ada_call(cc, w_ada, b_ada):
    depth = w_ada.shape[0]
    n = w_ada.shape[2]
    return pl.pallas_call(
        _ada_kernel,
        grid=(depth, n // ADA_TN),
        in_specs=[
            pl.BlockSpec((ADA_ROWS, D_MODEL), lambda l, j: (0, 0)),
            pl.BlockSpec((None, D_MODEL, ADA_TN), lambda l, j: (l, 0, j)),
            pl.BlockSpec((None, 1, ADA_TN), lambda l, j: (l, 0, j)),
        ],
        out_specs=pl.BlockSpec((None, ADA_ROWS, ADA_TN), lambda l, j: (l, 0, j)),
        out_shape=jax.ShapeDtypeStruct((depth, ADA_ROWS, n), F32),
        compiler_params=_params(),
        name="ada",
    )(cc, w_ada, b_ada.reshape(depth, 1, n))


def _stream_cast(src, dst_ref, stage_ref, sem, chunk):
    n_chunks = src.shape[0] // chunk

    def copy(i):
        return pltpu.make_async_copy(src.at[pl.ds(i * chunk, chunk), :], stage_ref.at[i % 2],
                                     sem.at[i % 2])

    copy(0).start()
    for i in range(n_chunks):
        if i + 1 < n_chunks:
            copy(i + 1).start()
        copy(i).wait()
        dst_ref[pl.ds(i * chunk, chunk), :] = stage_ref[i % 2].astype(BF16)


def _ffn_kernel(*refs, layer, mod_base, split_input, with_mix, with_proj, final):
    it = iter(refs)
    h_ref = next(it)
    if split_input:
        ctx_ref = next(it)
    if with_mix:
        oa_ref, ob_ref, oc_ref = next(it), next(it), next(it)
    mod_ref, g_ref = next(it), next(it)
    if with_mix:
        wout_ref = next(it)
    wup_hbm, wdn_hbm = next(it), next(it)
    if final:
        fg_ref = next(it)
    if with_proj:
        proj_in = [next(it) for _ in range(7)]
    o_ref = next(it)
    if with_proj:
        proj_out = [next(it) for _ in range(7)]
    hid_ref, wup_ref, wdn_ref, stage_up, stage_dn, sem_up, sem_dn = (next(it) for _ in range(7))

    @pl.when((pl.program_id(0) == 0) & (pl.program_id(1) == 0))
    def _load_weights():
        _stream_cast(wup_hbm.at[layer], wup_ref, stage_up, sem_up, FFN_UP_CHUNK)
        _stream_cast(wdn_hbm.at[layer], wdn_ref, stage_dn, sem_dn, FFN_DN_CHUNK)

    mod = mod_ref[...]

    def body(rows, lat_tile):
        if split_input and lat_tile is None:
            x = ctx_ref[...]
        else:
            x = h_ref[rows, :]
        if with_mix:
            mixed = jnp.concatenate([oa_ref[rows, :], ob_ref[rows, :], oc_ref[rows, :]], axis=-1)
            x = x + mod[5:6] * _dot(mixed, wout_ref[...])
        y = _rms(x, g_ref[...])
        xn = (y * (1.0 + mod[mod_base + 1:mod_base + 2]) + mod[mod_base:mod_base + 1]).astype(BF16)
        for c in range(D_FF // FF_CHUNK):
            lo = c * FF_CHUNK
            a = _dot(xn, wup_ref[:, lo:lo + FF_CHUNK])
            b = _dot(xn, wup_ref[:, D_FF + lo:D_FF + lo + FF_CHUNK])
            hid_ref[rows, lo:lo + FF_CHUNK] = (a * jax.nn.sigmoid(a) * b).astype(BF16)
        out = x + (0.5 * mod[mod_base + 2:mod_base + 3]) * _dot(hid_ref[rows, :], wdn_ref[...])
        if final:
            out = _rms(out, fg_ref[...])
        o_ref[rows, :] = out
        if with_proj:
            _proj_rows(out, rows, mod, *proj_in, *proj_out)

    _for_each_subtile(body, with_context=not final)


def _ffn_call(h, l, mods, gains, which_gain, w_up, w_dn, *, mod_base, mix=None, proj=None,
              final_g=None):
    split_input = isinstance(h, tuple)
    bn = (h[0] if split_input else h).shape[0]
    with_mix = mix is not None
    with_proj = proj is not None
    final = final_g is not None
    n_steps = LAT_STEPS if final else N_STEPS
    if split_input:
        args = list(h)
        specs = [pl.BlockSpec((None, STEP, D_MODEL), lambda b, t: (b, jnp.minimum(t, LAT_STEPS - 1), 0)),
                 pl.BlockSpec((None, CTX_LEN, D_MODEL), lambda b, t: (b, 0, 0))]
    else:
        args, specs = [h], [_token_tile(D_MODEL)]
    if with_mix:
        oa, ob, oc, w_out = mix
        args += [oa, ob, oc]
        specs += [_token_tile(NA_WIDTH), _token_tile(POOL_WIDTH), _token_tile(GQA_Q_WIDTH)]
    args += [mods, gains]
    specs += [_mod_block(l, bn), _gain_block(l, which_gain)]
    if with_mix:
        args.append(w_out)
        specs.append(_layer_block(l, (D_MIX, D_MODEL), single_buffer=True))
    args += [w_up, w_dn]
    specs += [pl.BlockSpec(memory_space=pl.ANY), pl.BlockSpec(memory_space=pl.ANY)]
    if final:
        args.append(final_g.reshape(1, D_MODEL))
        specs.append(pl.BlockSpec((1, D_MODEL), lambda b, t: (0, 0)))
    out_specs = [_token_tile(D_MODEL)]
    out_shape = [jax.ShapeDtypeStruct((bn, SEQ if final else TOK, D_MODEL), F32)]
    if with_proj:
        w_in, rope_cs, rope_sn, qg, kg, ones_bd = proj
        args += [gains, w_in, rope_cs, rope_sn, qg, kg, ones_bd]
        specs += [
            _gain_block(l, 1),
            _layer_block(l, (D_MODEL, D_IN), single_buffer=True),
            pl.BlockSpec((STEP, 2 * HEAD_DIM), lambda b, t: (t, 0)),
            pl.BlockSpec((STEP, 2 * HEAD_DIM), lambda b, t: (t, 0)),
            _layer_block(l, (1, GQA_Q_WIDTH)),
            _layer_block(l, (1, GQA_KV_WIDTH)),
            pl.BlockSpec((NORM_BD, NORM_BD), lambda b, t: (0, 0)),
        ]
        rows_of = lambda w, dt: jax.ShapeDtypeStruct((bn, TOK, w), dt)
        out_specs += [_token_tile(NA_WIDTH), _token_tile(NA_WIDTH), _token_tile(NA_WIDTH),
                      _token_tile(POOL_WIDTH), _token_tile(GQA_Q_WIDTH), _token_tile(GQA_KV_WIDTH),
                      pl.BlockSpec((None, GQA_KV_WIDTH, STEP), lambda b, t: (b, 0, t))]
        out_shape += [rows_of(NA_WIDTH, BF16), rows_of(NA_WIDTH, BF16), rows_of(NA_WIDTH, BF16),
                      rows_of(POOL_WIDTH, F32), rows_of(GQA_Q_WIDTH, BF16), rows_of(GQA_KV_WIDTH, BF16),
                      jax.ShapeDtypeStruct((bn, GQA_KV_WIDTH, TOK), BF16)]
    outs = pl.pallas_call(
        functools.partial(_ffn_kernel, layer=l, mod_base=mod_base, split_input=split_input,
                          with_mix=with_mix, with_proj=with_proj, final=final),
        grid=(bn, n_steps),
        in_specs=specs,
        out_specs=out_specs,
        out_shape=out_shape,
        scratch_shapes=[pltpu.VMEM((STEP, D_FF), BF16),
                        pltpu.VMEM((D_MODEL, 2 * D_FF), BF16),
                        pltpu.VMEM((D_FF, D_MODEL), BF16),
                        pltpu.VMEM((2, FFN_UP_CHUNK, 2 * D_FF), F32),
                        pltpu.VMEM((2, FFN_DN_CHUNK, D_MODEL), F32),
                        pltpu.SemaphoreType.DMA((2,)),
                        pltpu.SemaphoreType.DMA((2,))],
        compiler_params=_params(),
        name="ffn_mix" if with_mix else "ffn_proj" if with_proj else "ffn",
    )(*args)
    return outs if with_proj else outs[0]


def _head_norm_rope(z, gain, cs, sn, ones_bd):
    width = z.shape[-1]
    zz = z * z
    hi = zz.astype(BF16)
    lo = (zz - hi.astype(F32)).astype(BF16)
    span = min(width, ones_bd.shape[0])
    bd = ones_bd[:span, :span]
    ms = jnp.concatenate([_dot(hi[:, c:c + span], bd) + _dot(lo[:, c:c + span], bd)
                          for c in range(0, width, span)], axis=-1)
    zn = z * lax.rsqrt(ms + EPS) * gain
    lane = lax.broadcasted_iota(jnp.int32, zn.shape, 1)
    partner = jnp.where((lane & 16) != 0,
                        pltpu.roll(zn, 16, axis=1), pltpu.roll(zn, width - 16, axis=1))
    reps = width // cs.shape[-1]
    if reps > 1:
        cs = jnp.concatenate([cs] * reps, axis=-1)
        sn = jnp.concatenate([sn] * reps, axis=-1)
    return zn * cs + partner * sn


def _proj_rows(x, rows, mod, g_ref, win_ref, cs_ref, sn_ref, qg_ref, kg_ref, bd_ref,
               qa_ref, ka_ref, va_ref, u_ref, qc_ref, kc_ref, vct_ref):
    bd = bd_ref[...]
    y = _rms(x, g_ref[...])
    a = (y * (1.0 + mod[4:5]) + mod[3:4]).astype(BF16)
    p = _dot(a, win_ref[...])
    qa_ref[rows, :] = (p[:, OFF_A_Q:OFF_A_K] * (QK_SCALE * LOG2_E)).astype(BF16)
    ka_ref[rows, :] = p[:, OFF_A_K:OFF_A_V].astype(BF16)
    va_ref[rows, :] = p[:, OFF_A_V:OFF_B_U].astype(BF16)
    u_ref[rows, :] = p[:, OFF_B_U:OFF_C_Q]
    cs, sn = cs_ref[rows, :], sn_ref[rows, :]
    qc = _head_norm_rope(p[:, OFF_C_Q:OFF_C_K], qg_ref[...], cs, sn, bd)
    qc_ref[rows, :] = (qc * (QK_SCALE * LOG2_E)).astype(BF16)
    kc = _head_norm_rope(p[:, OFF_C_K:OFF_C_V], kg_ref[...], cs, sn, bd)
    kc_ref[rows, :] = kc.astype(BF16)
    vct_ref[:, rows] = p[:, OFF_C_V:D_IN].T.astype(BF16)


def _softmax_pv(scores, values):
    m = scores[0].max(axis=-1, keepdims=True)
    for s in scores[1:]:
        m = jnp.maximum(m, s.max(axis=-1, keepdims=True))
    den = None
    acc = None
    for s, v in zip(scores, values):
        e = jnp.exp2(s - m)
        d = e.sum(axis=-1, keepdims=True)
        o = _dot(e.astype(BF16), v)
        den = d if den is None else den + d
        acc = o if acc is None else acc + o
    return acc / den


def _na_kernel(q_ref, k_ref, v_ref, tb_ref, rm_ref, o_ref, *, with_context):
    head_of_lane = lax.broadcasted_iota(jnp.int32, (1, NA_WIDTH), 1) // HEAD_DIM
    k_ctx = k_ref[SEQ:TOK, :]
    v_ctx = v_ref[SEQ:TOK, :]

    def context(rows):
        q = q_ref[rows, :]
        out = jnp.zeros((TM, NA_WIDTH), F32)
        for h in range(NA_HEADS):
            mine = head_of_lane == h
            qh = jnp.where(mine, q, jnp.zeros_like(q))
            out = jnp.where(mine, _softmax_pv([_dot_nt(qh, k_ctx)], [v_ctx]), out)
        o_ref[rows, :] = out.astype(BF16)

    def body(rows, g):
        if g is None:
            return context(rows)
        q = q_ref[rows, :]
        first_row = jnp.clip(g * NA_QROWS - NA_WIN_H // 2, 0, GRID_H - NA_KROWS)
        start = pl.multiple_of(first_row * GRID_W, GRID_W)
        k_lat = k_ref[pl.ds(start, NA_KEYS), :]
        v_lat = v_ref[pl.ds(start, NA_KEYS), :]
        pattern = jnp.minimum(g, 2) + (g == LAT_TILES - 1).astype(jnp.int32)
        slot0 = first_row - g * NA_QROWS + NA_WIN_H - 1 + NA_DY_PAD

        def scores(h):
            bias = jnp.concatenate([
                jnp.concatenate([
                    tb_ref[h, slot0 + 2 * m - qr]
                    + rm_ref[pattern, qr * NA_KPAIRS + m:qr * NA_KPAIRS + m + 1, :]
                    for m in range(NA_KPAIRS)], axis=-1)
                for qr in range(NA_QROWS)], axis=0)
            qh = jnp.where(head_of_lane == h, q, jnp.zeros_like(q))
            return [_dot_nt(qh, k_lat) + bias, _dot_nt(qh, k_ctx)]

        out = jnp.zeros((TM, NA_WIDTH), F32)
        s_next = scores(0)
        for h in range(NA_HEADS):
            s_cur = s_next
            if h + 1 < NA_HEADS:
                s_next = scores(h + 1)
            out = jnp.where(head_of_lane == h, _softmax_pv(s_cur, [v_lat, v_ctx]), out)
        o_ref[rows, :] = out.astype(BF16)

    _for_each_subtile(body, with_context=with_context)


def _na_call(qa, ka, va, l, tb, rm, *, with_context):
    bn = qa.shape[0]
    whole = pl.BlockSpec((None, TOK, NA_WIDTH), lambda b, t: (b, 0, 0))
    return pl.pallas_call(
        functools.partial(_na_kernel, with_context=with_context),
        grid=(bn, N_STEPS if with_context else LAT_STEPS),
        in_specs=[
            _token_tile(NA_WIDTH), whole, whole,
            _layer_block(l, tb.shape[1:], single_buffer=True),
            pl.BlockSpec(rm.shape, lambda b, t: (0, 0, 0)),
        ],
        out_specs=_token_tile(NA_WIDTH),
        out_shape=jax.ShapeDtypeStruct((bn, TOK if with_context else SEQ, NA_WIDTH), BF16),
        compiler_params=_params(),
        name="na",
    )(qa, ka, va, tb, rm)


def _na_static_tables():
    col = np.arange(GRID_W)
    win_c0 = np.clip(col - NA_WIN_W // 2, 0, GRID_W - NA_WIN_W)
    col_ok = (col[None, :] >= win_c0[:, None]) & (col[None, :] < win_c0[:, None] + NA_WIN_W)
    dx = np.clip(col[None, :] - col[:, None], -(NA_WIN_W - 1), NA_WIN_W - 1) + NA_WIN_W - 1
    sel_x = np.eye(2 * NA_WIN_W - 1, dtype=np.float32)[dx]
    row_mask = np.zeros((len(NA_PATTERN_GROUPS), NA_QROWS, NA_KPAIRS, 2, GRID_W), np.float32)
    for p, grp in enumerate(NA_PATTERN_GROUPS):
        first_row = int(np.clip(grp * NA_QROWS - NA_WIN_H // 2, 0, GRID_H - NA_KROWS))
        q_row = grp * NA_QROWS + np.arange(NA_QROWS)
        q_r0 = np.clip(q_row - NA_WIN_H // 2, 0, GRID_H - NA_WIN_H)
        k_row = first_row + np.arange(NA_KROWS)
        row_ok = (k_row[None, :] >= q_r0[:, None]) & (k_row[None, :] < q_r0[:, None] + NA_WIN_H)
        row_mask[p] = np.where(row_ok, 0.0, NEG_INF).reshape(NA_QROWS, NA_KPAIRS, 2, 1)
    row_mask = row_mask.reshape(len(NA_PATTERN_GROUPS), NA_QROWS * NA_KPAIRS, 2 * GRID_W)
    return sel_x, col_ok, row_mask


_NA_SEL_X, _NA_COL_OK, _NA_ROW_MASK = _na_static_tables()


def _na_bias_table(rpb):
    t = jnp.einsum('lhab,xcb->lhaxc', rpb, _NA_SEL_X, precision=lax.Precision.HIGHEST) * LOG2_E
    t = jnp.where(_NA_COL_OK, t, NEG_INF)
    t = jnp.pad(t, ((0, 0), (0, 0), (NA_DY_PAD, NA_DY_PAD), (0, 0), (0, 0)))
    return jnp.concatenate([t[:, :, :-1], t[:, :, 1:]], axis=-1)


def _shift_rows(a, d):
    n = a.shape[0]
    row = lax.broadcasted_iota(jnp.int32, a.shape, 0)
    rolled = pltpu.roll(a, d % n, axis=0)
    ok = (row >= d) if d > 0 else (row < n + d)
    return jnp.where(ok, rolled, 0.0)


def _pool_lanes(u, windows):
    n = u.shape[0]
    row = lax.broadcasted_iota(jnp.int32, u.shape, 0)
    grp = lax.broadcasted_iota(jnp.int32, u.shape, 1) // POOL_GC
    trail, lead = u, u
    total = jnp.zeros_like(u)
    count = jnp.ones_like(u)
    k = 1
    for i, w in enumerate(windows):
        while k < w // 2:
            trail = trail + _shift_rows(trail, k)
            lead = lead + _shift_rows(lead, -k)
            k *= 2
        win = _shift_rows(trail, 1) + lead
        cnt = (jnp.minimum(row + w // 2, n) - jnp.maximum(row - w // 2, 0)).astype(F32)
        total = jnp.where(grp == i, win, total)
        count = jnp.where(grp == i, cnt, count)
    return total / count - u


def _pool_segment(u, w_bd, scale):
    half = 2 * POOL_GC
    y = jnp.concatenate([_pool_lanes(u[:, c:c + half], POOL_WINDOWS[c // POOL_GC:(c + half) // POOL_GC])
                         for c in range(0, POOL_WIDTH, half)], axis=-1).astype(BF16)
    return (_dot(y, w_bd) * scale).astype(BF16)


def _pool_kernel(u_ref, w_ref, s_ref, o_ref, *, with_context):
    w_bd = w_ref[...]
    scale = s_ref[...]
    o_ref[:SEQ, :] = _pool_segment(u_ref[:SEQ, :], w_bd, scale)
    if with_context:
        o_ref[SEQ:, :] = _pool_segment(u_ref[SEQ:, :], w_bd, scale)


def _pool_call(u, l, w_bd, scale, *, with_context):
    bn = u.shape[0]
    rows = TOK if with_context else SEQ
    return pl.pallas_call(
        functools.partial(_pool_kernel, with_context=with_context),
        grid=(bn,),
        in_specs=[pl.BlockSpec((None, TOK, POOL_WIDTH), lambda b: (b, 0, 0)),
                  _layer_block(l, (POOL_WIDTH, POOL_WIDTH)), _layer_block(l, (1, POOL_WIDTH))],
        out_specs=pl.BlockSpec((None, rows, POOL_WIDTH), lambda b: (b, 0, 0)),
        out_shape=jax.ShapeDtypeStruct((bn, rows, POOL_WIDTH), BF16),
        compiler_params=_params(1),
        name="pool",
    )(u, w_bd, scale)


def _gqa_tile(q, k_ref, vt_ref, key_chunks, bounded):
    half_of_lane = lax.broadcasted_iota(jnp.int32, (1, GQA_KV_WIDTH), 1) // HEAD_DIM
    stages = [(j, lo, hi) for j in range(GQA_GROUP) for lo, hi in key_chunks]

    def scores_t(stage):
        j, lo, hi = stage
        qb = q[:, j * GQA_KV_WIDTH:(j + 1) * GQA_KV_WIDTH]
        q2 = jnp.concatenate([jnp.where(half_of_lane == kv, qb, jnp.zeros_like(qb))
                              for kv in range(GQA_KV_HEADS)], axis=0)
        return _dot_nt(k_ref[lo:hi, :], q2)

    partial = {j: [] for j in range(GQA_GROUP)}
    s_next = scores_t(stages[0])
    for i, (j, lo, hi) in enumerate(stages):
        s_t = s_next
        if i + 1 < len(stages):
            s_next = scores_t(stages[i + 1])
        if bounded:
            m = None
            e = jnp.exp2(s_t)
        else:
            m = s_t.max(axis=0, keepdims=True)
            e = jnp.exp2(s_t - m)
        den = e.sum(axis=0, keepdims=True)
        e = e.astype(BF16)
        o_t = [_dot(vt_ref[kv * HEAD_DIM:(kv + 1) * HEAD_DIM, lo:hi], e[:, kv * TM:(kv + 1) * TM])
               for kv in range(GQA_KV_HEADS)]
        partial[j].append((m, den, o_t))

    blocks = []
    for j in range(GQA_GROUP):
        if bounded:
            weights = [1.0] * len(partial[j])
        else:
            m_all = functools.reduce(jnp.maximum, [m for m, _, _ in partial[j]])
            weights = [jnp.exp2(m - m_all) for m, _, _ in partial[j]]
        den = sum(w * d for w, (_, d, _) in zip(weights, partial[j]))
        o_t = jnp.concatenate([
            sum((w if bounded else w[:, kv * TM:(kv + 1) * TM]) * o[kv]
                for w, (_, _, o) in zip(weights, partial[j]))
            / den[:, kv * TM:(kv + 1) * TM]
            for kv in range(GQA_KV_HEADS)], axis=0)
        blocks.append(o_t.T)
    return jnp.concatenate(blocks, axis=-1).astype(BF16)


def _gqa_kernel(q_ref, k_ref, vt_ref, o_ref, *, bounded, with_context):
    def body(rows, lat_tile):
        if lat_tile is None:
            chunks = [(SEQ, TOK)]
        else:
            chunks = [(lo, lo + GQA_KEY_CHUNK) for lo in range(0, TOK, GQA_KEY_CHUNK)]
        o_ref[rows, :] = _gqa_tile(q_ref[rows, :], k_ref, vt_ref, chunks, bounded)

    _for_each_subtile(body, with_context=with_context)


def _gqa_logit_bound(q_gain, k_gain):
    return (HEAD_DIM * jnp.max(jnp.abs(q_gain)) * jnp.max(jnp.abs(k_gain))
            * (QK_SCALE * LOG2_E) * (1.0 + 2.0 ** -7) ** 2)


def _gqa_call(qc, kc, vc_t, *, bounded, with_context):
    bn = qc.shape[0]
    return pl.pallas_call(
        functools.partial(_gqa_kernel, bounded=bounded, with_context=with_context),
        grid=(bn, N_STEPS if with_context else LAT_STEPS),
        in_specs=[_token_tile(GQA_Q_WIDTH),
                  pl.BlockSpec((None, TOK, GQA_KV_WIDTH), lambda b, t: (b, 0, 0)),
                  pl.BlockSpec((None, GQA_KV_WIDTH, TOK), lambda b, t: (b, 0, 0))],
        out_specs=_token_tile(GQA_Q_WIDTH),
        out_shape=jax.ShapeDtypeStruct((bn, TOK if with_context else SEQ, GQA_Q_WIDTH), BF16),
        compiler_params=_params(),
        name="gqa_bounded" if bounded else "gqa",
    )(qc, kc, vc_t)


def _rope_tables():
    pos = np.arange(SEQ)
    inv_freq = ROPE_THETA ** (-np.arange(0, HEAD_DIM // 2, 2, dtype=np.float32) / (HEAD_DIM // 2))
    ang_row = (pos // GRID_W).astype(np.float32)[:, None] * inv_freq[None, :].astype(np.float32)
    ang_col = (pos % GRID_W).astype(np.float32)[:, None] * inv_freq[None, :].astype(np.float32)
    ang = np.concatenate([ang_row, ang_row, ang_col, ang_col], axis=-1).astype(np.float32)
    sign = np.tile(np.repeat(np.array([-1.0, 1.0], np.float32), HEAD_DIM // 4), 2)
    return ang, sign


_ROPE_ANG, _ROPE_SIGN = _rope_tables()


def _group_major(a, axis):
    shape = a.shape
    a = a.reshape(shape[:axis] + (GQA_KV_HEADS, GQA_GROUP, HEAD_DIM) + shape[axis + 1:])
    return jnp.swapaxes(a, axis, axis + 1).reshape(shape)


def kernel(x, c, ctx, c_ctx, w_ada, b_ada, norm_g, ffn1_up, ffn1_down, ffn2_up, ffn2_down,
           w_in, w_out, na_rpb, pool_w, pool_scale, q_norm_g, k_norm_g, final_g):
    bn = x.shape[0]
    depth = w_ada.shape[0]
    assert x.shape == (bn, SEQ, D_MODEL) and ctx.shape == (bn, CTX_LEN, D_MODEL)
    assert bn < ADA_ROWS

    cc = jnp.zeros((ADA_ROWS, D_MODEL), F32).at[:bn].set(c).at[bn].set(c_ctx)
    mods = _ada_call(cc, w_ada, b_ada).reshape(depth, ADA_ROWS, N_MOD, D_MODEL)
    gains = norm_g.reshape(depth, 3, 1, D_MODEL)

    ang = jnp.asarray(_ROPE_ANG)
    cs_lat = jnp.tile(jnp.cos(ang), (1, 2))
    sn_lat = jnp.tile(jnp.sin(ang) * jnp.asarray(_ROPE_SIGN)[None, :], (1, 2))
    rope_cs = jnp.concatenate([cs_lat, jnp.ones((CTX_LEN, 2 * HEAD_DIM), F32)], axis=0)
    rope_sn = jnp.concatenate([sn_lat, jnp.zeros((CTX_LEN, 2 * HEAD_DIM), F32)], axis=0)
    ones_bd = jnp.asarray(
        np.kron(np.eye(NORM_BD // HEAD_DIM, dtype=np.float32),
                np.full((HEAD_DIM, HEAD_DIM), 1.0 / HEAD_DIM, np.float32))).astype(BF16)

    w_in_p = jnp.concatenate([w_in[:, :, :OFF_C_Q], _group_major(w_in[:, :, OFF_C_Q:OFF_C_K], 2),
                              w_in[:, :, OFF_C_K:]], axis=-1).astype(BF16)
    w_out_p = jnp.concatenate([w_out[:, :NA_WIDTH + POOL_WIDTH],
                               _group_major(w_out[:, NA_WIDTH + POOL_WIDTH:], 1)], axis=1).astype(BF16)
    pool_bd = jnp.einsum('gh,lgcd->lgchd', jnp.eye(len(POOL_WINDOWS), dtype=F32), pool_w)
    pool_bd = pool_bd.reshape(depth, POOL_WIDTH, POOL_WIDTH).astype(BF16)
    pool_sc = pool_scale.reshape(depth, 1, POOL_WIDTH)
    qg = jnp.tile(q_norm_g, (1, GQA_Q_HEADS)).reshape(depth, 1, GQA_Q_WIDTH)
    kg = jnp.tile(k_norm_g, (1, GQA_KV_HEADS)).reshape(depth, 1, GQA_KV_WIDTH)
    na_tb = _na_bias_table(na_rpb)
    na_rm = jnp.asarray(_NA_ROW_MASK)

    h = (x, ctx)
    for l in range(depth):
        last = l == depth - 1
        h, qa, ka, va, u, qc, kc, vc_t = _ffn_call(
            h, l, mods, gains, 0, ffn1_up, ffn1_down, mod_base=0,
            proj=(w_in_p, rope_cs, rope_sn, qg, kg, ones_bd))
        oa = _na_call(qa, ka, va, l, na_tb, na_rm, with_context=not last)
        ob = _pool_call(u, l, pool_bd, pool_sc, with_context=not last)
        oc = lax.cond(_gqa_logit_bound(q_norm_g[l], k_norm_g[l]) <= GQA_SAFE_LOGIT,
                      functools.partial(_gqa_call, qc, kc, vc_t, bounded=True, with_context=not last),
                      functools.partial(_gqa_call, qc, kc, vc_t, bounded=False, with_context=not last))
        h = _ffn_call(h, l, mods, gains, 2, ffn2_up, ffn2_down, mod_base=6,
                      mix=(oa, ob, oc, w_out_p), final_g=final_g if last else None)
    return h
```

```python
import functools

import numpy as np
import jax
import jax.numpy as jnp
from jax import lax
from jax.experimental import pallas as pl
from jax.experimental.pallas import tpu as pltpu

D_MODEL = 1024
SEQ = 2048
CTX_LEN = 256
TOK = SEQ + CTX_LEN
GRID_W = 64
GRID_H = SEQ // GRID_W
HEAD_DIM = 64
N_MOD = 9
D_FF = 2816
EPS = 1e-6
NEG_INF = -1e30
NA_HEADS = 4
NA_WIN_H = 8
NA_WIN_W = 16
NA_WIDTH = NA_HEADS * HEAD_DIM
POOL_WINDOWS = (2, 4, 8, 16)
POOL_WIDTH = 256
POOL_GC = POOL_WIDTH // len(POOL_WINDOWS)
GQA_Q_HEADS = 8
GQA_KV_HEADS = 2
GQA_GROUP = GQA_Q_HEADS // GQA_KV_HEADS
GQA_Q_WIDTH = GQA_Q_HEADS * HEAD_DIM
GQA_KV_WIDTH = GQA_KV_HEADS * HEAD_DIM
ROPE_THETA = 10000.0
OFF_A_Q = 0
OFF_A_K = OFF_A_Q + NA_WIDTH
OFF_A_V = OFF_A_K + NA_WIDTH
OFF_B_U = OFF_A_V + NA_WIDTH
OFF_C_Q = OFF_B_U + POOL_WIDTH
OFF_C_K = OFF_C_Q + GQA_Q_WIDTH
OFF_C_V = OFF_C_K + GQA_KV_WIDTH
D_IN = OFF_C_V + GQA_KV_WIDTH
D_MIX = NA_WIDTH + POOL_WIDTH + GQA_Q_WIDTH
QK_SCALE = HEAD_DIM ** -0.5
LOG2_E = float(np.log2(np.e))

TM = 256
LAT_TILES = SEQ // TM
STEP = 2 * TM
LAT_STEPS = SEQ // STEP
N_STEPS = LAT_STEPS + 1
NA_QROWS = TM // GRID_W
NA_KROWS = 12
NA_KEYS = NA_KROWS * GRID_W
NA_KPAIRS = NA_KROWS // 2
NA_DY = 2 * NA_WIN_H - 1
NA_DY_PAD = 4
NA_DY_SLOTS = NA_DY + 2 * NA_DY_PAD - 1
NA_PATTERN_GROUPS = (0, 1, 2, LAT_TILES - 1)
GQA_KEY_CHUNK = 768
GQA_SAFE_LOGIT = 64.0
FF_CHUNK = 256
FFN_UP_CHUNK = 64
FFN_DN_CHUNK = D_FF // 8
NORM_BD = 256
ADA_TN = 1152
ADA_ROWS = 16
VMEM_LIMIT = 56 * 1024 * 1024

BF16 = jnp.bfloat16
F32 = jnp.float32


def _dot(a, b):
    return jnp.dot(a, b, preferred_element_type=F32)


def _dot_nt(a, b):
    return lax.dot_general(a, b, (((1,), (1,)), ((), ())), preferred_element_type=F32)


def _rms(x, g):
    ms = jnp.mean(x * x, axis=-1, keepdims=True)
    return x * lax.rsqrt(ms + EPS) * g


def _layer_block(l, shape, *, single_buffer=False):
    nd = len(shape)
    mode = dict(pipeline_mode=pl.Buffered(1)) if single_buffer else {}
    return pl.BlockSpec((None,) + tuple(shape), lambda *_: (l,) + (0,) * nd, **mode)


def _mod_block(l, bn):
    return pl.BlockSpec((None, None, N_MOD, D_MODEL),
                        lambda b, t: (l, b + (t // LAT_STEPS) * (bn - b), 0, 0))


def _gain_block(l, which):
    return pl.BlockSpec((None, None, 1, D_MODEL), lambda b, t: (l, which, 0, 0))


def _token_tile(width):
    return pl.BlockSpec((None, STEP, width), lambda b, t: (b, t, 0))


def _for_each_subtile(body, *, with_context=True):
    t = pl.program_id(1)
    subtiles = [pl.ds(s * TM, TM) for s in range(STEP // TM)]

    def latent():
        for s, rows in enumerate(subtiles):
            body(rows, t * (STEP // TM) + s)

    if not with_context:
        latent()
        return
    pl.when(t < LAT_STEPS)(latent)

    @pl.when(t == LAT_STEPS)
    def _context():
        body(subtiles[0], None)


def _params(n_axes=2):
    return pltpu.CompilerParams(
        dimension_semantics=("arbitrary",) * n_axes, vmem_limit_bytes=VMEM_LIMIT)


def _split_bf16(a):
    hi = a.astype(BF16)
    return hi, (a - hi.astype(F32)).astype(BF16)


def _ada_kernel(c_ref, w_ref, b_ref, o_ref):
    c = c_ref[...]
    rows = c.shape[0]
    s_parts = jnp.concatenate(_split_bf16(c * jax.nn.sigmoid(c)), axis=0)
    acc = b_ref[...]
    for w_part in _split_bf16(w_ref[...]):
        partial = _dot(s_parts, w_part)
        acc = acc + partial[:rows] + partial[rows:]
    o_ref[...] = acc


def _ada_call(cc, w_ada, b_ada):
    depth = w_ada.shape[0]
    n = w_ada.shape[2]
    return pl.pallas_call(
        _ada_kernel,
        grid=(depth, n // ADA_TN),
        in_specs=[
            pl.BlockSpec((ADA_ROWS, D_MODEL), lambda l, j: (0, 0)),
            pl.BlockSpec((None, D_MODEL, ADA_TN), lambda l, j: (l, 0, j)),
            pl.BlockSpec((None, 1, ADA_TN), lambda l, j: (l, 0, j)),
        ],
        out_specs=pl.BlockSpec((None, ADA_ROWS, ADA_TN), lambda l, j: (l, 0, j)),
        out_shape=jax.ShapeDtypeStruct((depth, ADA_ROWS, n), F32),
        compiler_params=_params(),
        name="ada",
    )(cc, w_ada, b_ada.reshape(depth, 1, n))


def _stream_cast(src, dst_ref, stage_ref, sem, chunk):
    n_chunks = src.shape[0] // chunk

    def copy(i):
        return pltpu.make_async_copy(src.at[pl.ds(i * chunk, chunk), :], stage_ref.at[i % 2],
                                     sem.at[i % 2])

    copy(0).start()
    for i in range(n_chunks):
        if i + 1 < n_chunks:
            copy(i + 1).start()
        copy(i).wait()
        dst_ref[pl.ds(i * chunk, chunk), :] = stage_ref[i % 2].astype(BF16)


def _ffn_kernel(*refs, layer, mod_base, split_input, with_mix, with_proj, final):
    it = iter(refs)
    h_ref = next(it)
    if split_input:
        ctx_ref = next(it)
    if with_mix:
        oa_ref, ob_ref, oc_ref = next(it), next(it), next(it)
    mod_ref, g_ref = next(it), next(it)
    if with_mix:
        wout_ref = next(it)
    wup_hbm, wdn_hbm = next(it), next(it)
    if final:
        fg_ref = next(it)
    if with_proj:
        proj_in = [next(it) for _ in range(7)]
    o_ref = next(it)
    if with_proj:
        proj_out = [next(it) for _ in range(7)]
    hid_ref, wup_ref, wdn_ref, stage_up, stage_dn, sem_up, sem_dn = (next(it) for _ in range(7))

    @pl.when((pl.program_id(0) == 0) & (pl.program_id(1) == 0))
    def _load_weights():
        _stream_cast(wup_hbm.at[layer], wup_ref, stage_up, sem_up, FFN_UP_CHUNK)
        _stream_cast(wdn_hbm.at[layer], wdn_ref, stage_dn, sem_dn, FFN_DN_CHUNK)

    mod = mod_ref[...]

    def body(rows, lat_tile):
        if split_input and lat_tile is None:
            x = ctx_ref[...]
        else:
            x = h_ref[rows, :]
        if with_mix:
            mixed = jnp.concatenate([oa_ref[rows, :], ob_ref[rows, :], oc_ref[rows, :]], axis=-1)
            x = x + mod[5:6] * _dot(mixed, wout_ref[...])
        y = _rms(x, g_ref[...])
        xn = (y * (1.0 + mod[mod_base + 1:mod_base + 2]) + mod[mod_base:mod_base + 1]).astype(BF16)
        for c in range(D_FF // FF_CHUNK):
            lo = c * FF_CHUNK
            a = _dot(xn, wup_ref[:, lo:lo + FF_CHUNK])
            b = _dot(xn, wup_ref[:, D_FF + lo:D_FF + lo + FF_CHUNK])
            hid_ref[rows, lo:lo + FF_CHUNK] = (a * jax.nn.sigmoid(a) * b).astype(BF16)
        out = x + (0.5 * mod[mod_base + 2:mod_base + 3]) * _dot(hid_ref[rows, :], wdn_ref[...])
        if final:
            out = _rms(out, fg_ref[...])
        o_ref[rows, :] = out
        if with_proj:
            _proj_rows(out, rows, mod, *proj_in, *proj_out)

    _for_each_subtile(body, with_context=not final)


def _ffn_call(h, l, mods, gains, which_gain, w_up, w_dn, *, mod_base, mix=None, proj=None,
              final_g=None):
    split_input = isinstance(h, tuple)
    bn = (h[0] if split_input else h).shape[0]
    with_mix = mix is not None
    with_proj = proj is not None
    final = final_g is not None
    n_steps = LAT_STEPS if final else N_STEPS
    if split_input:
        args = list(h)
        specs = [pl.BlockSpec((None, STEP, D_MODEL), lambda b, t: (b, jnp.minimum(t, LAT_STEPS - 1), 0)),
                 pl.BlockSpec((None, CTX_LEN, D_MODEL), lambda b, t: (b, 0, 0))]
    else:
        args, specs = [h], [_token_tile(D_MODEL)]
    if with_mix:
        oa, ob, oc, w_out = mix
        args += [oa, ob, oc]
        specs += [_token_tile(NA_WIDTH), _token_tile(POOL_WIDTH), _token_tile(GQA_Q_WIDTH)]
    args += [mods, gains]
    specs += [_mod_block(l, bn), _gain_block(l, which_gain)]
    if with_mix:
        args.append(w_out)
        specs.append(_layer_block(l, (D_MIX, D_MODEL), single_buffer=True))
    args += [w_up, w_dn]
    specs += [pl.BlockSpec(memory_space=pl.ANY), pl.BlockSpec(memory_space=pl.ANY)]
    if final:
        args.append(final_g.reshape(1, D_MODEL))
        specs.append(pl.BlockSpec((1, D_MODEL), lambda b, t: (0, 0)))
    out_specs = [_token_tile(D_MODEL)]
    out_shape = [jax.ShapeDtypeStruct((bn, SEQ if final else TOK, D_MODEL), F32)]
    if with_proj:
        w_in, rope_cs, rope_sn, qg, kg, ones_bd = proj
        args += [gains, w_in, rope_cs, rope_sn, qg, kg, ones_bd]
        specs += [
            _gain_block(l, 1),
            _layer_block(l, (D_MODEL, D_IN), single_buffer=True),
            pl.BlockSpec((STEP, 2 * HEAD_DIM), lambda b, t: (t, 0)),
            pl.BlockSpec((STEP, 2 * HEAD_DIM), lambda b, t: (t, 0)),
            _layer_block(l, (1, GQA_Q_WIDTH)),
            _layer_block(l, (1, GQA_KV_WIDTH)),
            pl.BlockSpec((NORM_BD, NORM_BD), lambda b, t: (0, 0)),
        ]
        rows_of = lambda w, dt: jax.ShapeDtypeStruct((bn, TOK, w), dt)
        out_specs += [_token_tile(NA_WIDTH), _token_tile(NA_WIDTH), _token_tile(NA_WIDTH),
                      _token_tile(POOL_WIDTH), _token_tile(GQA_Q_WIDTH), _token_tile(GQA_KV_WIDTH),
                      pl.BlockSpec((None, GQA_KV_WIDTH, STEP), lambda b, t: (b, 0, t))]
        out_shape += [rows_of(NA_WIDTH, BF16), rows_of(NA_WIDTH, BF16), rows_of(NA_WIDTH, BF16),
                      rows_of(POOL_WIDTH, F32), rows_of(GQA_Q_WIDTH, BF16), rows_of(GQA_KV_WIDTH, BF16),
                      jax.ShapeDtypeStruct((bn, GQA_KV_WIDTH, TOK), BF16)]
    outs = pl.pallas_call(
        functools.partial(_ffn_kernel, layer=l, mod_base=mod_base, split_input=split_input,
                          with_mix=with_mix, with_proj=with_proj, final=final),
        grid=(bn, n_steps),
        in_specs=specs,
        out_specs=out_specs,
        out_shape=out_shape,
        scratch_shapes=[pltpu.VMEM((STEP, D_FF), BF16),
                        pltpu.VMEM((D_MODEL, 2 * D_FF), BF16),
                        pltpu.VMEM((D_FF, D_MODEL), BF16),
                        pltpu.VMEM((2, FFN_UP_CHUNK, 2 * D_FF), F32),
                        pltpu.VMEM((2, FFN_DN_CHUNK, D_MODEL), F32),
                        pltpu.SemaphoreType.DMA((2,)),
                        pltpu.SemaphoreType.DMA((2,))],
        compiler_params=_params(),
        name="ffn_mix" if with_mix else "ffn_proj" if with_proj else "ffn",
    )(*args)
    return outs if with_proj else outs[0]


def _head_norm_rope(z, gain, cs, sn, ones_bd):
    width = z.shape[-1]
    zz = z * z
    hi = zz.astype(BF16)
    lo = (zz - hi.astype(F32)).astype(BF16)
    span = min(width, ones_bd.shape[0])
    bd = ones_bd[:span, :span]
    ms = jnp.concatenate([_dot(hi[:, c:c + span], bd) + _dot(lo[:, c:c + span], bd)
                          for c in range(0, width, span)], axis=-1)
    zn = z * lax.rsqrt(ms + EPS) * gain
    lane = lax.broadcasted_iota(jnp.int32, zn.shape, 1)
    partner = jnp.where((lane & 16) != 0,
                        pltpu.roll(zn, 16, axis=1), pltpu.roll(zn, width - 16, axis=1))
    reps = width // cs.shape[-1]
    if reps > 1:
        cs = jnp.concatenate([cs] * reps, axis=-1)
        sn = jnp.concatenate([sn] * reps, axis=-1)
    return zn * cs + partner * sn


def _proj_rows(x, rows, mod, g_ref, win_ref, cs_ref, sn_ref, qg_ref, kg_ref, bd_ref,
               qa_ref, ka_ref, va_ref, u_ref, qc_ref, kc_ref, vct_ref):
    bd = bd_ref[...]
    y = _rms(x, g_ref[...])
    a = (y * (1.0 + mod[4:5]) + mod[3:4]).astype(BF16)
    p = _dot(a, win_ref[...])
    qa_ref[rows, :] = (p[:, OFF_A_Q:OFF_A_K] * (QK_SCALE * LOG2_E)).astype(BF16)
    ka_ref[rows, :] = p[:, OFF_A_K:OFF_A_V].astype(BF16)
    va_ref[rows, :] = p[:, OFF_A_V:OFF_B_U].astype(BF16)
    u_ref[rows, :] = p[:, OFF_B_U:OFF_C_Q]
    cs, sn = cs_ref[rows, :], sn_ref[rows, :]
    qc = _head_norm_rope(p[:, OFF_C_Q:OFF_C_K], qg_ref[...], cs, sn, bd)
    qc_ref[rows, :] = (qc * (QK_SCALE * LOG2_E)).astype(BF16)
    kc = _head_norm_rope(p[:, OFF_C_K:OFF_C_V], kg_ref[...], cs, sn, bd)
    kc_ref[rows, :] = kc.astype(BF16)
    vct_ref[:, rows] = p[:, OFF_C_V:D_IN].T.astype(BF16)


def _softmax_pv(scores, values):
    m = scores[0].max(axis=-1, keepdims=True)
    for s in scores[1:]:
        m = jnp.maximum(m, s.max(axis=-1, keepdims=True))
    den = None
    acc = None
    for s, v in zip(scores, values):
        e = jnp.exp2(s - m)
        d = e.sum(axis=-1, keepdims=True)
        o = _dot(e.astype(BF16), v)
        den = d if den is None else den + d
        acc = o if acc is None else acc + o
    return acc / den


def _na_kernel(q_ref, k_ref, v_ref, tb_ref, rm_ref, o_ref, *, with_context):
    head_of_lane = lax.broadcasted_iota(jnp.int32, (1, NA_WIDTH), 1) // HEAD_DIM
    k_ctx = k_ref[SEQ:TOK, :]
    v_ctx = v_ref[SEQ:TOK, :]

    def context(rows):
        q = q_ref[rows, :]
        out = jnp.zeros((TM, NA_WIDTH), F32)
        for h in range(NA_HEADS):
            mine = head_of_lane == h
            qh = jnp.where(mine, q, jnp.zeros_like(q))
            out = jnp.where(mine, _softmax_pv([_dot_nt(qh, k_ctx)], [v_ctx]), out)
        o_ref[rows, :] = out.astype(BF16)

    def body(rows, g):
        if g is None:
            return context(rows)
        q = q_ref[rows, :]
        first_row = jnp.clip(g * NA_QROWS - NA_WIN_H // 2, 0, GRID_H - NA_KROWS)
        start = pl.multiple_of(first_row * GRID_W, GRID_W)
        k_lat = k_ref[pl.ds(start, NA_KEYS), :]
        v_lat = v_ref[pl.ds(start, NA_KEYS), :]
        pattern = jnp.minimum(g, 2) + (g == LAT_TILES - 1).astype(jnp.int32)
        slot0 = first_row - g * NA_QROWS + NA_WIN_H - 1 + NA_DY_PAD

        def scores(h):
            bias = jnp.concatenate([
                jnp.concatenate([
                    tb_ref[h, slot0 + 2 * m - qr]
                    + rm_ref[pattern, qr * NA_KPAIRS + m:qr * NA_KPAIRS + m + 1, :]
                    for m in range(NA_KPAIRS)], axis=-1)
                for qr in range(NA_QROWS)], axis=0)
            qh = jnp.where(head_of_lane == h, q, jnp.zeros_like(q))
            return [_dot_nt(qh, k_lat) + bias, _dot_nt(qh, k_ctx)]

        out = jnp.zeros((TM, NA_WIDTH), F32)
        s_next = scores(0)
        for h in range(NA_HEADS):
            s_cur = s_next
            if h + 1 < NA_HEADS:
                s_next = scores(h + 1)
            out = jnp.where(head_of_lane == h, _softmax_pv(s_cur, [v_lat, v_ctx]), out)
        o_ref[rows, :] = out.astype(BF16)

    _for_each_subtile(body, with_context=with_context)


def _na_call(qa, ka, va, l, tb, rm, *, with_context):
    bn = qa.shape[0]
    whole = pl.BlockSpec((None, TOK, NA_WIDTH), lambda b, t: (b, 0, 0))
    return pl.pallas_call(
        functools.partial(_na_kernel, with_context=with_context),
        grid=(bn, N_STEPS if with_context else LAT_STEPS),
        in_specs=[
            _token_tile(NA_WIDTH), whole, whole,
            _layer_block(l, tb.shape[1:], single_buffer=True),
            pl.BlockSpec(rm.shape, lambda b, t: (0, 0, 0)),
        ],
        out_specs=_token_tile(NA_WIDTH),
        out_shape=jax.ShapeDtypeStruct((bn, TOK if with_context else SEQ, NA_WIDTH), BF16),
        compiler_params=_params(),
        name="na",
    )(qa, ka, va, tb, rm)


def _na_static_tables():
    col = np.arange(GRID_W)
    win_c0 = np.clip(col - NA_WIN_W // 2, 0, GRID_W - NA_WIN_W)
    col_ok = (col[None, :] >= win_c0[:, None]) & (col[None, :] < win_c0[:, None] + NA_WIN_W)
    dx = np.clip(col[None, :] - col[:, None], -(NA_WIN_W - 1), NA_WIN_W - 1) + NA_WIN_W - 1
    sel_x = np.eye(2 * NA_WIN_W - 1, dtype=np.float32)[dx]
    row_mask = np.zeros((len(NA_PATTERN_GROUPS), NA_QROWS, NA_KPAIRS, 2, GRID_W), np.float32)
    for p, grp in enumerate(NA_PATTERN_GROUPS):
        first_row = int(np.clip(grp * NA_QROWS - NA_WIN_H // 2, 0, GRID_H - NA_KROWS))
        q_row = grp * NA_QROWS + np.arange(NA_QROWS)
        q_r0 = np.clip(q_row - NA_WIN_H // 2, 0, GRID_H - NA_WIN_H)
        k_row = first_row + np.arange(NA_KROWS)
        row_ok = (k_row[None, :] >= q_r0[:, None]) & (k_row[None, :] < q_r0[:, None] + NA_WIN_H)
        row_mask[p] = np.where(row_ok, 0.0, NEG_INF).reshape(NA_QROWS, NA_KPAIRS, 2, 1)
    row_mask = row_mask.reshape(len(NA_PATTERN_GROUPS), NA_QROWS * NA_KPAIRS, 2 * GRID_W)
    return sel_x, col_ok, row_mask


_NA_SEL_X, _NA_COL_OK, _NA_ROW_MASK = _na_static_tables()


def _na_bias_table(rpb):
    t = jnp.einsum('lhab,xcb->lhaxc', rpb, _NA_SEL_X, precision=lax.Precision.HIGHEST) * LOG2_E
    t = jnp.where(_NA_COL_OK, t, NEG_INF)
    t = jnp.pad(t, ((0, 0), (0, 0), (NA_DY_PAD, NA_DY_PAD), (0, 0), (0, 0)))
    return jnp.concatenate([t[:, :, :-1], t[:, :, 1:]], axis=-1)


def _shift_rows(a, d):
    n = a.shape[0]
    row = lax.broadcasted_iota(jnp.int32, a.shape, 0)
    rolled = pltpu.roll(a, d % n, axis=0)
    ok = (row >= d) if d > 0 else (row < n + d)
    return jnp.where(ok, rolled, 0.0)


def _pool_lanes(u, windows):
    n = u.shape[0]
    row = lax.broadcasted_iota(jnp.int32, u.shape, 0)
    grp = lax.broadcasted_iota(jnp.int32, u.shape, 1) // POOL_GC
    trail, lead = u, u
    total = jnp.zeros_like(u)
    count = jnp.ones_like(u)
    k = 1
    for i, w in enumerate(windows):
        while k < w // 2:
            trail = trail + _shift_rows(trail, k)
            lead = lead + _shift_rows(lead, -k)
            k *= 2
        win = _shift_rows(trail, 1) + lead
        cnt = (jnp.minimum(row + w // 2, n) - jnp.maximum(row - w // 2, 0)).astype(F32)
        total = jnp.where(grp == i, win, total)
        count = jnp.where(grp == i, cnt, count)
    return total / count - u


def _pool_segment(u, w_bd, scale):
    half = 2 * POOL_GC
    y = jnp.concatenate([_pool_lanes(u[:, c:c + half], POOL_WINDOWS[c // POOL_GC:(c + half) // POOL_GC])
                         for c in range(0, POOL_WIDTH, half)], axis=-1).astype(BF16)
    return (_dot(y, w_bd) * scale).astype(BF16)


def _pool_kernel(u_ref, w_ref, s_ref, o_ref, *, with_context):
    w_bd = w_ref[...]
    scale = s_ref[...]
    o_ref[:SEQ, :] = _pool_segment(u_ref[:SEQ, :], w_bd, scale)
    if with_context:
        o_ref[SEQ:, :] = _pool_segment(u_ref[SEQ:, :], w_bd, scale)


def _pool_call(u, l, w_bd, scale, *, with_context):
    bn = u.shape[0]
    rows = TOK if with_context else SEQ
    return pl.pallas_call(
        functools.partial(_pool_kernel, with_context=with_context),
        grid=(bn,),
        in_specs=[pl.BlockSpec((None, TOK, POOL_WIDTH), lambda b: (b, 0, 0)),
                  _layer_block(l, (POOL_WIDTH, POOL_WIDTH)), _layer_block(l, (1, POOL_WIDTH))],
        out_specs=pl.BlockSpec((None, rows, POOL_WIDTH), lambda b: (b, 0, 0)),
        out_shape=jax.ShapeDtypeStruct((bn, rows, POOL_WIDTH), BF16),
        compiler_params=_params(1),
        name="pool",
    )(u, w_bd, scale)


def _gqa_tile(q, k_ref, vt_ref, key_chunks, bounded):
    half_of_lane = lax.broadcasted_iota(jnp.int32, (1, GQA_KV_WIDTH), 1) // HEAD_DIM
    stages = [(j, lo, hi) for j in range(GQA_GROUP) for lo, hi in key_chunks]

    def scores_t(stage):
        j, lo, hi = stage
        qb = q[:, j * GQA_KV_WIDTH:(j + 1) * GQA_KV_WIDTH]
        q2 = jnp.concatenate([jnp.where(half_of_lane == kv, qb, jnp.zeros_like(qb))
                              for kv in range(GQA_KV_HEADS)], axis=0)
        return _dot_nt(k_ref[lo:hi, :], q2)

    partial = {j: [] for j in range(GQA_GROUP)}
    s_next = scores_t(stages[0])
    for i, (j, lo, hi) in enumerate(stages):
        s_t = s_next
        if i + 1 < len(stages):
            s_next = scores_t(stages[i + 1])
        if bounded:
            m = None
            e = jnp.exp2(s_t)
        else:
            m = s_t.max(axis=0, keepdims=True)
            e = jnp.exp2(s_t - m)
        den = e.sum(axis=0, keepdims=True)
        e = e.astype(BF16)
        o_t = [_dot(vt_ref[kv * HEAD_DIM:(kv + 1) * HEAD_DIM, lo:hi], e[:, kv * TM:(kv + 1) * TM])
               for kv in range(GQA_KV_HEADS)]
        partial[j].append((m, den, o_t))

    blocks = []
    for j in range(GQA_GROUP):
        if bounded:
            weights = [1.0] * len(partial[j])
        else:
            m_all = functools.reduce(jnp.maximum, [m for m, _, _ in partial[j]])
            weights = [jnp.exp2(m - m_all) for m, _, _ in partial[j]]
        den = sum(w * d for w, (_, d, _) in zip(weights, partial[j]))
        o_t = jnp.concatenate([
            sum((w if bounded else w[:, kv * TM:(kv + 1) * TM]) * o[kv]
                for w, (_, _, o) in zip(weights, partial[j]))
            / den[:, kv * TM:(kv + 1) * TM]
            for kv in range(GQA_KV_HEADS)], axis=0)
        blocks.append(o_t.T)
    return jnp.concatenate(blocks, axis=-1).astype(BF16)


def _gqa_kernel(q_ref, k_ref, vt_ref, o_ref, *, bounded, with_context):
    def body(rows, lat_tile):
        if lat_tile is None:
            chunks = [(SEQ, TOK)]
        else:
            chunks = [(lo, lo + GQA_KEY_CHUNK) for lo in range(0, TOK, GQA_KEY_CHUNK)]
        o_ref[rows, :] = _gqa_tile(q_ref[rows, :], k_ref, vt_ref, chunks, bounded)

    _for_each_subtile(body, with_context=with_context)


def _gqa_logit_bound(q_gain, k_gain):
    return (HEAD_DIM * jnp.max(jnp.abs(q_gain)) * jnp.max(jnp.abs(k_gain))
            * (QK_SCALE * LOG2_E) * (1.0 + 2.0 ** -7) ** 2)


def _gqa_call(qc, kc, vc_t, *, bounded, with_context):
    bn = qc.shape[0]
    return pl.pallas_call(
        functools.partial(_gqa_kernel, bounded=bounded, with_context=with_context),
        grid=(bn, N_STEPS if with_context else LAT_STEPS),
        in_specs=[_token_tile(GQA_Q_WIDTH),
                  pl.BlockSpec((None, TOK, GQA_KV_WIDTH), lambda b, t: (b, 0, 0)),
                  pl.BlockSpec((None, GQA_KV_WIDTH, TOK), lambda b, t: (b, 0, 0))],
        out_specs=_token_tile(GQA_Q_WIDTH),
        out_shape=jax.ShapeDtypeStruct((bn, TOK if with_context else SEQ, GQA_Q_WIDTH), BF16),
        compiler_params=_params(),
        name="gqa_bounded" if bounded else "gqa",
    )(qc, kc, vc_t)


def _rope_tables():
    pos = np.arange(SEQ)
    inv_freq = ROPE_THETA ** (-np.arange(0, HEAD_DIM // 2, 2, dtype=np.float32) / (HEAD_DIM // 2))
    ang_row = (pos // GRID_W).astype(np.float32)[:, None] * inv_freq[None, :].astype(np.float32)
    ang_col = (pos % GRID_W).astype(np.float32)[:, None] * inv_freq[None, :].astype(np.float32)
    ang = np.concatenate([ang_row, ang_row, ang_col, ang_col], axis=-1).astype(np.float32)
    sign = np.tile(np.repeat(np.array([-1.0, 1.0], np.float32), HEAD_DIM // 4), 2)
    return ang, sign


_ROPE_ANG, _ROPE_SIGN = _rope_tables()


def _group_major(a, axis):
    shape = a.shape
    a = a.reshape(shape[:axis] + (GQA_KV_HEADS, GQA_GROUP, HEAD_DIM) + shape[axis + 1:])
    return jnp.swapaxes(a, axis, axis + 1).reshape(shape)


def kernel(x, c, ctx, c_ctx, w_ada, b_ada, norm_g, ffn1_up, ffn1_down, ffn2_up, ffn2_down,
           w_in, w_out, na_rpb, pool_w, pool_scale, q_norm_g, k_norm_g, final_g):
    bn = x.shape[0]
    depth = w_ada.shape[0]
    assert x.shape == (bn, SEQ, D_MODEL) and ctx.shape == (bn, CTX_LEN, D_MODEL)
    assert bn < ADA_ROWS

    cc = jnp.zeros((ADA_ROWS, D_MODEL), F32).at[:bn].set(c).at[bn].set(c_ctx)
    mods = _ada_call(cc, w_ada, b_ada).reshape(depth, ADA_ROWS, N_MOD, D_MODEL)
    gains = norm_g.reshape(depth, 3, 1, D_MODEL)

    ang = jnp.asarray(_ROPE_ANG)
    cs_lat = jnp.tile(jnp.cos(ang), (1, 2))
    sn_lat = jnp.tile(jnp.sin(ang) * jnp.asarray(_ROPE_SIGN)[None, :], (1, 2))
    rope_cs = jnp.concatenate([cs_lat, jnp.ones((CTX_LEN, 2 * HEAD_DIM), F32)], axis=0)
    rope_sn = jnp.concatenate([sn_lat, jnp.zeros((CTX_LEN, 2 * HEAD_DIM), F32)], axis=0)
    ones_bd = jnp.asarray(
        np.kron(np.eye(NORM_BD // HEAD_DIM, dtype=np.float32),
                np.full((HEAD_DIM, HEAD_DIM), 1.0 / HEAD_DIM, np.float32))).astype(BF16)

    w_in_p = jnp.concatenate([w_in[:, :, :OFF_C_Q], _group_major(w_in[:, :, OFF_C_Q:OFF_C_K], 2),
                              w_in[:, :, OFF_C_K:]], axis=-1).astype(BF16)
    w_out_p = jnp.concatenate([w_out[:, :NA_WIDTH + POOL_WIDTH],
                               _group_major(w_out[:, NA_WIDTH + POOL_WIDTH:], 1)], axis=1).astype(BF16)
    pool_bd = jnp.einsum('gh,lgcd->lgchd', jnp.eye(len(POOL_WINDOWS), dtype=F32), pool_w)
    pool_bd = pool_bd.reshape(depth, POOL_WIDTH, POOL_WIDTH).astype(BF16)
    pool_sc = pool_scale.reshape(depth, 1, POOL_WIDTH)
    qg = jnp.tile(q_norm_g, (1, GQA_Q_HEADS)).reshape(depth, 1, GQA_Q_WIDTH)
    kg = jnp.tile(k_norm_g, (1, GQA_KV_HEADS)).reshape(depth, 1, GQA_KV_WIDTH)
    na_tb = _na_bias_table(na_rpb)
    na_rm = jnp.asarray(_NA_ROW_MASK)

    h = (x, ctx)
    for l in range(depth):
        last = l == depth - 1
        h, qa, ka, va, u, qc, kc, vc_t = _ffn_call(
            h, l, mods, gains, 0, ffn1_up, ffn1_down, mod_base=0,
            proj=(w_in_p, rope_cs, rope_sn, qg, kg, ones_bd))
        oa = _na_call(qa, ka, va, l, na_tb, na_rm, with_context=not last)
        ob = _pool_call(u, l, pool_bd, pool_sc, with_context=not last)
        oc = lax.cond(_gqa_logit_bound(q_norm_g[l], k_norm_g[l]) <= GQA_SAFE_LOGIT,
                      functools.partial(_gqa_call, qc, kc, vc_t, bounded=True, with_context=not last),
                      functools.partial(_gqa_call, qc, kc, vc_t, bounded=False, with_context=not last))
        h = _ffn_call(h, l, mods, gains, 2, ffn2_up, ffn2_down, mod_base=6,
                      mix=(oa, ob, oc, w_out_p), final_g=final_g if last else None)
    return h
```

```python
import functools

import numpy as np
import jax
import jax.numpy as jnp
from jax import lax
from jax.experimental import pallas as pl
from jax.experimental.pallas import tpu as pltpu

D_MODEL = 1024
SEQ = 2048
CTX_LEN = 256
TOK = SEQ + CTX_LEN
GRID_W = 64
GRID_H = SEQ // GRID_W
HEAD_DIM = 64
N_MOD = 9
D_FF = 2816
EPS = 1e-6
NEG_INF = -1e30
NA_HEADS = 4
NA_WIN_H = 8
NA_WIN_W = 16
NA_WIDTH = NA_HEADS * HEAD_DIM
POOL_WINDOWS = (2, 4, 8, 16)
POOL_WIDTH = 256
POOL_GC = POOL_WIDTH // len(POOL_WINDOWS)
GQA_Q_HEADS = 8
GQA_KV_HEADS = 2
GQA_GROUP = GQA_Q_HEADS // GQA_KV_HEADS
GQA_Q_WIDTH = GQA_Q_HEADS * HEAD_DIM
GQA_KV_WIDTH = GQA_KV_HEADS * HEAD_DIM
ROPE_THETA = 10000.0
OFF_A_Q = 0
OFF_A_K = OFF_A_Q + NA_WIDTH
OFF_A_V = OFF_A_K + NA_WIDTH
OFF_B_U = OFF_A_V + NA_WIDTH
OFF_C_Q = OFF_B_U + POOL_WIDTH
OFF_C_K = OFF_C_Q + GQA_Q_WIDTH
OFF_C_V = OFF_C_K + GQA_KV_WIDTH
D_IN = OFF_C_V + GQA_KV_WIDTH
D_MIX = NA_WIDTH + POOL_WIDTH + GQA_Q_WIDTH
QK_SCALE = HEAD_DIM ** -0.5
LOG2_E = float(np.log2(np.e))

TM = 256
LAT_TILES = SEQ // TM
STEP = 2 * TM
LAT_STEPS = SEQ // STEP
N_STEPS = LAT_STEPS + 1
NA_QROWS = TM // GRID_W
NA_KROWS = 12
NA_KEYS = NA_KROWS * GRID_W
NA_KPAIRS = NA_KROWS // 2
NA_DY = 2 * NA_WIN_H - 1
NA_DY_PAD = 4
NA_DY_SLOTS = NA_DY + 2 * NA_DY_PAD - 1
NA_PATTERN_GROUPS = (0, 1, 2, LAT_TILES - 1)
GQA_KEY_CHUNK = 768
GQA_SAFE_LOGIT = 64.0
FF_CHUNK = 256
FFN_UP_CHUNK = 64
FFN_DN_CHUNK = D_FF // 8
NORM_BD = 256
ADA_TN = 1152
ADA_ROWS = 16
VMEM_LIMIT = 56 * 1024 * 1024

BF16 = jnp.bfloat16
F32 = jnp.float32


def _dot(a, b):
    return jnp.dot(a, b, preferred_element_type=F32)


def _dot_nt(a, b):
    return lax.dot_general(a, b, (((1,), (1,)), ((), ())), preferred_element_type=F32)


def _rms(x, g):
    ms = jnp.mean(x * x, axis=-1, keepdims=True)
    return x * lax.rsqrt(ms + EPS) * g


def _layer_block(l, shape, *, single_buffer=False):
    nd = len(shape)
    mode = dict(pipeline_mode=pl.Buffered(1)) if single_buffer else {}
    return pl.BlockSpec((None,) + tuple(shape), lambda *_: (l,) + (0,) * nd, **mode)


def _mod_block(l, bn):
    return pl.BlockSpec((None, None, N_MOD, D_MODEL),
                        lambda b, t: (l, b + (t // LAT_STEPS) * (bn - b), 0, 0))


def _gain_block(l, which):
    return pl.BlockSpec((None, None, 1, D_MODEL), lambda b, t: (l, which, 0, 0))


def _token_tile(width):
    return pl.BlockSpec((None, STEP, width), lambda b, t: (b, t, 0))


def _for_each_subtile(body, *, with_context=True, then=None):
    t = pl.program_id(1)
    subtiles = [pl.ds(s * TM, TM) for s in range(STEP // TM)]

    def latent():
        for s, rows in enumerate(subtiles):
            body(rows, t * (STEP // TM) + s)
        if then is not None:
            for rows in subtiles:
                then(rows)

    if not with_context:
        latent()
        return
    pl.when(t < LAT_STEPS)(latent)

    @pl.when(t == LAT_STEPS)
    def _context():
        body(subtiles[0], None)
        if then is not None:
            then(subtiles[0])


def _params(n_axes=2):
    return pltpu.CompilerParams(
        dimension_semantics=("arbitrary",) * n_axes, vmem_limit_bytes=VMEM_LIMIT)


def _split_bf16(a):
    hi = a.astype(BF16)
    return hi, (a - hi.astype(F32)).astype(BF16)


def _ada_kernel(c_ref, w_ref, b_ref, o_ref):
    c = c_ref[...]
    rows = c.shape[0]
    s_parts = jnp.concatenate(_split_bf16(c * jax.nn.sigmoid(c)), axis=0)
    acc = b_ref[...]
    for w_part in _split_bf16(w_ref[...]):
        partial = _dot(s_parts, w_part)
        acc = acc + partial[:rows] + partial[rows:]
    o_ref[...] = acc


def _ada_call(cc, w_ada, b_ada):
    depth = w_ada.shape[0]
    n = w_ada.shape[2]
    return pl.pallas_call(
        _ada_kernel,
        grid=(depth, n // ADA_TN),
        in_specs=[
            pl.BlockSpec((ADA_ROWS, D_MODEL), lambda l, j: (0, 0)),
            pl.BlockSpec((None, D_MODEL, ADA_TN), lambda l, j: (l, 0, j)),
            pl.BlockSpec((None, 1, ADA_TN), lambda l, j: (l, 0, j)),
        ],
        out_specs=pl.BlockSpec((None, ADA_ROWS, ADA_TN), lambda l, j: (l, 0, j)),
        out_shape=jax.ShapeDtypeStruct((depth, ADA_ROWS, n), F32),
        compiler_params=_params(),
        name="ada",
    )(cc, w_ada, b_ada.reshape(depth, 1, n))


def _stream_cast(src, dst_ref, stage_ref, sem, chunk):
    n_chunks = src.shape[0] // chunk

    def copy(i):
        return pltpu.make_async_copy(src.at[pl.ds(i * chunk, chunk), :], stage_ref.at[i % 2],
                                     sem.at[i % 2])

    copy(0).start()
    for i in range(n_chunks):
        if i + 1 < n_chunks:
            copy(i + 1).start()
        copy(i).wait()
        dst_ref[pl.ds(i * chunk, chunk), :] = stage_ref[i % 2].astype(BF16)


def _ffn_kernel(*refs, layer, mod_base, split_input, with_mix, with_proj, final):
    it = iter(refs)
    h_ref = next(it)
    if split_input:
        ctx_ref = next(it)
    if with_mix:
        oa_ref, ob_ref, oc_ref = next(it), next(it), next(it)
    mod_ref, g_ref = next(it), next(it)
    if with_mix:
        wout_ref = next(it)
    wup_hbm, wdn_hbm = next(it), next(it)
    if final:
        fg_ref = next(it)
    if with_proj:
        proj_in = [next(it) for _ in range(7)]
    o_ref = next(it)
    if with_proj:
        proj_out = [next(it) for _ in range(7)]
    hid_ref, wup_ref, wdn_ref, stage_up, stage_dn, sem_up, sem_dn = (next(it) for _ in range(7))

    @pl.when((pl.program_id(0) == 0) & (pl.program_id(1) == 0))
    def _load_weights():
        _stream_cast(wup_hbm.at[layer], wup_ref, stage_up, sem_up, FFN_UP_CHUNK)
        _stream_cast(wdn_hbm.at[layer], wdn_ref, stage_dn, sem_dn, FFN_DN_CHUNK)

    mod = mod_ref[...]

    def body(rows, lat_tile):
        if split_input and lat_tile is None:
            x = ctx_ref[...]
        else:
            x = h_ref[rows, :]
        if with_mix:
            mixed = jnp.concatenate([oa_ref[rows, :], ob_ref[rows, :], oc_ref[rows, :]], axis=-1)
            x = x + mod[5:6] * _dot(mixed, wout_ref[...])
        y = _rms(x, g_ref[...])
        xn = (y * (1.0 + mod[mod_base + 1:mod_base + 2]) + mod[mod_base:mod_base + 1]).astype(BF16)
        for c in range(D_FF // FF_CHUNK):
            lo = c * FF_CHUNK
            a = _dot(xn, wup_ref[:, lo:lo + FF_CHUNK])
            b = _dot(xn, wup_ref[:, D_FF + lo:D_FF + lo + FF_CHUNK])
            hid_ref[rows, lo:lo + FF_CHUNK] = (a * jax.nn.sigmoid(a) * b).astype(BF16)
        out = x + (0.5 * mod[mod_base + 2:mod_base + 3]) * _dot(hid_ref[rows, :], wdn_ref[...])
        if final:
            out = _rms(out, fg_ref[...])
        o_ref[rows, :] = out

    def project(rows):
        _proj_rows(o_ref[rows, :], rows, mod, *proj_in, *proj_out)

    _for_each_subtile(body, with_context=not final, then=project if with_proj else None)


def _ffn_call(h, l, mods, gains, which_gain, w_up, w_dn, *, mod_base, mix=None, proj=None,
              final_g=None):
    split_input = isinstance(h, tuple)
    bn = (h[0] if split_input else h).shape[0]
    with_mix = mix is not None
    with_proj = proj is not None
    final = final_g is not None
    n_steps = LAT_STEPS if final else N_STEPS
    if split_input:
        args = list(h)
        specs = [pl.BlockSpec((None, STEP, D_MODEL), lambda b, t: (b, jnp.minimum(t, LAT_STEPS - 1), 0)),
                 pl.BlockSpec((None, CTX_LEN, D_MODEL), lambda b, t: (b, 0, 0))]
    else:
        args, specs = [h], [_token_tile(D_MODEL)]
    if with_mix:
        oa, ob, oc, w_out = mix
        args += [oa, ob, oc]
        specs += [_token_tile(NA_WIDTH), _token_tile(POOL_WIDTH), _token_tile(GQA_Q_WIDTH)]
    args += [mods, gains]
    specs += [_mod_block(l, bn), _gain_block(l, which_gain)]
    if with_mix:
        args.append(w_out)
        specs.append(_layer_block(l, (D_MIX, D_MODEL), single_buffer=True))
    args += [w_up, w_dn]
    specs += [pl.BlockSpec(memory_space=pl.ANY), pl.BlockSpec(memory_space=pl.ANY)]
    if final:
        args.append(final_g.reshape(1, D_MODEL))
        specs.append(pl.BlockSpec((1, D_MODEL), lambda b, t: (0, 0)))
    out_specs = [_token_tile(D_MODEL)]
    out_shape = [jax.ShapeDtypeStruct((bn, SEQ if final else TOK, D_MODEL), F32)]
    if with_proj:
        w_in, rope_cs, rope_sn, qg, kg, ones_bd = proj
        args += [gains, w_in, rope_cs, rope_sn, qg, kg, ones_bd]
        specs += [
            _gain_block(l, 1),
            _layer_block(l, (D_MODEL, D_IN), single_buffer=True),
            pl.BlockSpec((STEP, 2 * HEAD_DIM), lambda b, t: (t, 0)),
            pl.BlockSpec((STEP, 2 * HEAD_DIM), lambda b, t: (t, 0)),
            _layer_block(l, (1, GQA_Q_WIDTH)),
            _layer_block(l, (1, GQA_KV_WIDTH)),
            pl.BlockSpec((NORM_BD, NORM_BD), lambda b, t: (0, 0)),
        ]
        rows_of = lambda w, dt: jax.ShapeDtypeStruct((bn, TOK, w), dt)
        out_specs += [_token_tile(NA_WIDTH), _token_tile(NA_WIDTH), _token_tile(NA_WIDTH),
                      _token_tile(POOL_WIDTH), _token_tile(GQA_Q_WIDTH), _token_tile(GQA_KV_WIDTH),
                      pl.BlockSpec((None, GQA_KV_WIDTH, STEP), lambda b, t: (b, 0, t))]
        out_shape += [rows_of(NA_WIDTH, BF16), rows_of(NA_WIDTH, BF16), rows_of(NA_WIDTH, BF16),
                      rows_of(POOL_WIDTH, F32), rows_of(GQA_Q_WIDTH, BF16), rows_of(GQA_KV_WIDTH, BF16),
                      jax.ShapeDtypeStruct((bn, GQA_KV_WIDTH, TOK), BF16)]
    outs = pl.pallas_call(
        functools.partial(_ffn_kernel, layer=l, mod_base=mod_base, split_input=split_input,
                          with_mix=with_mix, with_proj=with_proj, final=final),
        grid=(bn, n_steps),
        in_specs=specs,
        out_specs=out_specs,
        out_shape=out_shape,
        scratch_shapes=[pltpu.VMEM((STEP, D_FF), BF16),
                        pltpu.VMEM((D_MODEL, 2 * D_FF), BF16),
                        pltpu.VMEM((D_FF, D_MODEL), BF16),
                        pltpu.VMEM((2, FFN_UP_CHUNK, 2 * D_FF), F32),
                        pltpu.VMEM((2, FFN_DN_CHUNK, D_MODEL), F32),
                        pltpu.SemaphoreType.DMA((2,)),
                        pltpu.SemaphoreType.DMA((2,))],
        compiler_params=_params(),
        name="ffn_mix" if with_mix else "ffn_proj" if with_proj else "ffn",
    )(*args)
    return outs if with_proj else outs[0]


def _head_norm_rope(z, gain, cs, sn, ones_bd):
    width = z.shape[-1]
    zz = z * z
    hi = zz.astype(BF16)
    lo = (zz - hi.astype(F32)).astype(BF16)
    span = min(width, ones_bd.shape[0])
    bd = ones_bd[:span, :span]
    ms = jnp.concatenate([_dot(hi[:, c:c + span], bd) + _dot(lo[:, c:c + span], bd)
                          for c in range(0, width, span)], axis=-1)
    zn = z * lax.rsqrt(ms + EPS) * gain
    lane = lax.broadcasted_iota(jnp.int32, zn.shape, 1)
    partner = jnp.where((lane & 16) != 0,
                        pltpu.roll(zn, 16, axis=1), pltpu.roll(zn, width - 16, axis=1))
    reps = width // cs.shape[-1]
    if reps > 1:
        cs = jnp.concatenate([cs] * reps, axis=-1)
        sn = jnp.concatenate([sn] * reps, axis=-1)
    return zn * cs + partner * sn


def _proj_rows(x, rows, mod, g_ref, win_ref, cs_ref, sn_ref, qg_ref, kg_ref, bd_ref,
               qa_ref, ka_ref, va_ref, u_ref, qc_ref, kc_ref, vct_ref):
    bd = bd_ref[...]
    y = _rms(x, g_ref[...])
    a = (y * (1.0 + mod[4:5]) + mod[3:4]).astype(BF16)
    p = _dot(a, win_ref[...])
    qa_ref[rows, :] = (p[:, OFF_A_Q:OFF_A_K] * (QK_SCALE * LOG2_E)).astype(BF16)
    ka_ref[rows, :] = p[:, OFF_A_K:OFF_A_V].astype(BF16)
    va_ref[rows, :] = p[:, OFF_A_V:OFF_B_U].astype(BF16)
    u_ref[rows, :] = p[:, OFF_B_U:OFF_C_Q]
    cs, sn = cs_ref[rows, :], sn_ref[rows, :]
    qc = _head_norm_rope(p[:, OFF_C_Q:OFF_C_K], qg_ref[...], cs, sn, bd)
    qc_ref[rows, :] = (qc * (QK_SCALE * LOG2_E)).astype(BF16)
    kc = _head_norm_rope(p[:, OFF_C_K:OFF_C_V], kg_ref[...], cs, sn, bd)
    kc_ref[rows, :] = kc.astype(BF16)
    vct_ref[:, rows] = p[:, OFF_C_V:D_IN].T.astype(BF16)


def _softmax_pv(scores, values):
    m = scores[0].max(axis=-1, keepdims=True)
    for s in scores[1:]:
        m = jnp.maximum(m, s.max(axis=-1, keepdims=True))
    den = None
    acc = None
    for s, v in zip(scores, values):
        e = jnp.exp2(s - m)
        d = e.sum(axis=-1, keepdims=True)
        o = _dot(e.astype(BF16), v)
        den = d if den is None else den + d
        acc = o if acc is None else acc + o
    return acc / den


def _na_kernel(q_ref, k_ref, v_ref, tb_ref, rm_ref, o_ref, *, with_context):
    head_of_lane = lax.broadcasted_iota(jnp.int32, (1, NA_WIDTH), 1) // HEAD_DIM
    k_ctx = k_ref[SEQ:TOK, :]
    v_ctx = v_ref[SEQ:TOK, :]

    def context(rows):
        q = q_ref[rows, :]
        out = jnp.zeros((TM, NA_WIDTH), F32)
        for h in range(NA_HEADS):
            mine = head_of_lane == h
            qh = jnp.where(mine, q, jnp.zeros_like(q))
            out = jnp.where(mine, _softmax_pv([_dot_nt(qh, k_ctx)], [v_ctx]), out)
        o_ref[rows, :] = out.astype(BF16)

    def body(rows, g):
        if g is None:
            return context(rows)
        q = q_ref[rows, :]
        first_row = jnp.clip(g * NA_QROWS - NA_WIN_H // 2, 0, GRID_H - NA_KROWS)
        start = pl.multiple_of(first_row * GRID_W, GRID_W)
        k_lat = k_ref[pl.ds(start, NA_KEYS), :]
        v_lat = v_ref[pl.ds(start, NA_KEYS), :]
        pattern = jnp.minimum(g, 2) + (g == LAT_TILES - 1).astype(jnp.int32)
        slot0 = first_row - g * NA_QROWS + NA_WIN_H - 1 + NA_DY_PAD

        def scores(h):
            bias = jnp.concatenate([
                jnp.concatenate([
                    tb_ref[h, slot0 + 2 * m - qr]
                    + rm_ref[pattern, qr * NA_KPAIRS + m:qr * NA_KPAIRS + m + 1, :]
                    for m in range(NA_KPAIRS)], axis=-1)
                for qr in range(NA_QROWS)], axis=0)
            qh = jnp.where(head_of_lane == h, q, jnp.zeros_like(q))
            return [_dot_nt(qh, k_lat) + bias, _dot_nt(qh, k_ctx)]

        out = jnp.zeros((TM, NA_WIDTH), F32)
        s_next = scores(0)
        for h in range(NA_HEADS):
            s_cur = s_next
            if h + 1 < NA_HEADS:
                s_next = scores(h + 1)
            out = jnp.where(head_of_lane == h, _softmax_pv(s_cur, [v_lat, v_ctx]), out)
        o_ref[rows, :] = out.astype(BF16)

    _for_each_subtile(body, with_context=with_context)


def _na_call(qa, ka, va, l, tb, rm, *, with_context):
    bn = qa.shape[0]
    whole = pl.BlockSpec((None, TOK, NA_WIDTH), lambda b, t: (b, 0, 0))
    return pl.pallas_call(
        functools.partial(_na_kernel, with_context=with_context),
        grid=(bn, N_STEPS if with_context else LAT_STEPS),
        in_specs=[
            _token_tile(NA_WIDTH), whole, whole,
            _layer_block(l, tb.shape[1:], single_buffer=True),
            pl.BlockSpec(rm.shape, lambda b, t: (0, 0, 0)),
        ],
        out_specs=_token_tile(NA_WIDTH),
        out_shape=jax.ShapeDtypeStruct((bn, TOK if with_context else SEQ, NA_WIDTH), BF16),
        compiler_params=_params(),
        name="na",
    )(qa, ka, va, tb, rm)


def _na_static_tables():
    col = np.arange(GRID_W)
    win_c0 = np.clip(col - NA_WIN_W // 2, 0, GRID_W - NA_WIN_W)
    col_ok = (col[None, :] >= win_c0[:, None]) & (col[None, :] < win_c0[:, None] + NA_WIN_W)
    dx = np.clip(col[None, :] - col[:, None], -(NA_WIN_W - 1), NA_WIN_W - 1) + NA_WIN_W - 1
    sel_x = np.eye(2 * NA_WIN_W - 1, dtype=np.float32)[dx]
    row_mask = np.zeros((len(NA_PATTERN_GROUPS), NA_QROWS, NA_KPAIRS, 2, GRID_W), np.float32)
    for p, grp in enumerate(NA_PATTERN_GROUPS):
        first_row = int(np.clip(grp * NA_QROWS - NA_WIN_H // 2, 0, GRID_H - NA_KROWS))
        q_row = grp * NA_QROWS + np.arange(NA_QROWS)
        q_r0 = np.clip(q_row - NA_WIN_H // 2, 0, GRID_H - NA_WIN_H)
        k_row = first_row + np.arange(NA_KROWS)
        row_ok = (k_row[None, :] >= q_r0[:, None]) & (k_row[None, :] < q_r0[:, None] + NA_WIN_H)
        row_mask[p] = np.where(row_ok, 0.0, NEG_INF).reshape(NA_QROWS, NA_KPAIRS, 2, 1)
    row_mask = row_mask.reshape(len(NA_PATTERN_GROUPS), NA_QROWS * NA_KPAIRS, 2 * GRID_W)
    return sel_x, col_ok, row_mask


_NA_SEL_X, _NA_COL_OK, _NA_ROW_MASK = _na_static_tables()


def _na_bias_table(rpb):
    t = jnp.einsum('lhab,xcb->lhaxc', rpb, _NA_SEL_X, precision=lax.Precision.HIGHEST) * LOG2_E
    t = jnp.where(_NA_COL_OK, t, NEG_INF)
    t = jnp.pad(t, ((0, 0), (0, 0), (NA_DY_PAD, NA_DY_PAD), (0, 0), (0, 0)))
    return jnp.concatenate([t[:, :, :-1], t[:, :, 1:]], axis=-1)


def _shift_rows(a, d):
    n = a.shape[0]
    row = lax.broadcasted_iota(jnp.int32, a.shape, 0)
    rolled = pltpu.roll(a, d % n, axis=0)
    ok = (row >= d) if d > 0 else (row < n + d)
    return jnp.where(ok, rolled, 0.0)


def _pool_lanes(u, windows):
    n = u.shape[0]
    row = lax.broadcasted_iota(jnp.int32, u.shape, 0)
    grp = lax.broadcasted_iota(jnp.int32, u.shape, 1) // POOL_GC
    trail, lead = u, u
    total = jnp.zeros_like(u)
    count = jnp.ones_like(u)
    k = 1
    for i, w in enumerate(windows):
        while k < w // 2:
            trail = trail + _shift_rows(trail, k)
            lead = lead + _shift_rows(lead, -k)
            k *= 2
        win = _shift_rows(trail, 1) + lead
        cnt = (jnp.minimum(row + w // 2, n) - jnp.maximum(row - w // 2, 0)).astype(F32)
        total = jnp.where(grp == i, win, total)
        count = jnp.where(grp == i, cnt, count)
    return total / count - u


def _pool_segment(u, w_bd, scale):
    half = 2 * POOL_GC
    y = jnp.concatenate([_pool_lanes(u[:, c:c + half], POOL_WINDOWS[c // POOL_GC:(c + half) // POOL_GC])
                         for c in range(0, POOL_WIDTH, half)], axis=-1).astype(BF16)
    return (_dot(y, w_bd) * scale).astype(BF16)


def _pool_kernel(u_ref, w_ref, s_ref, o_ref, *, with_context):
    w_bd = w_ref[...]
    scale = s_ref[...]
    o_ref[:SEQ, :] = _pool_segment(u_ref[:SEQ, :], w_bd, scale)
    if with_context:
        o_ref[SEQ:, :] = _pool_segment(u_ref[SEQ:, :], w_bd, scale)


def _pool_call(u, l, w_bd, scale, *, with_context):
    bn = u.shape[0]
    rows = TOK if with_context else SEQ
    return pl.pallas_call(
        functools.partial(_pool_kernel, with_context=with_context),
        grid=(bn,),
        in_specs=[pl.BlockSpec((None, TOK, POOL_WIDTH), lambda b: (b, 0, 0)),
                  _layer_block(l, (POOL_WIDTH, POOL_WIDTH)), _layer_block(l, (1, POOL_WIDTH))],
        out_specs=pl.BlockSpec((None, rows, POOL_WIDTH), lambda b: (b, 0, 0)),
        out_shape=jax.ShapeDtypeStruct((bn, rows, POOL_WIDTH), BF16),
        compiler_params=_params(1),
        name="pool",
    )(u, w_bd, scale)


def _gqa_tile(q, k_ref, vt_ref, key_chunks, bounded):
    half_of_lane = lax.broadcasted_iota(jnp.int32, (1, GQA_KV_WIDTH), 1) // HEAD_DIM
    stages = [(j, lo, hi) for j in range(GQA_GROUP) for lo, hi in key_chunks]

    def scores_t(stage):
        j, lo, hi = stage
        qb = q[:, j * GQA_KV_WIDTH:(j + 1) * GQA_KV_WIDTH]
        q2 = jnp.concatenate([jnp.where(half_of_lane == kv, qb, jnp.zeros_like(qb))
                              for kv in range(GQA_KV_HEADS)], axis=0)
        return _dot_nt(k_ref[lo:hi, :], q2)

    partial = {j: [] for j in range(GQA_GROUP)}
    s_next = scores_t(stages[0])
    for i, (j, lo, hi) in enumerate(stages):
        s_t = s_next
        if i + 1 < len(stages):
            s_next = scores_t(stages[i + 1])
        if bounded:
            m = None
            e = jnp.exp2(s_t)
        else:
            m = s_t.max(axis=0, keepdims=True)
            e = jnp.exp2(s_t - m)
        den = e.sum(axis=0, keepdims=True)
        e = e.astype(BF16)
        o_t = [_dot(vt_ref[kv * HEAD_DIM:(kv + 1) * HEAD_DIM, lo:hi], e[:, kv * TM:(kv + 1) * TM])
               for kv in range(GQA_KV_HEADS)]
        partial[j].append((m, den, o_t))

    blocks = []
    for j in range(GQA_GROUP):
        if bounded:
            weights = [1.0] * len(partial[j])
        else:
            m_all = functools.reduce(jnp.maximum, [m for m, _, _ in partial[j]])
            weights = [jnp.exp2(m - m_all) for m, _, _ in partial[j]]
        den = sum(w * d for w, (_, d, _) in zip(weights, partial[j]))
        o_t = jnp.concatenate([
            sum((w if bounded else w[:, kv * TM:(kv + 1) * TM]) * o[kv]
                for w, (_, _, o) in zip(weights, partial[j]))
            / den[:, kv * TM:(kv + 1) * TM]
            for kv in range(GQA_KV_HEADS)], axis=0)
        blocks.append(o_t.T)
    return jnp.concatenate(blocks, axis=-1).astype(BF16)


def _gqa_kernel(q_ref, k_ref, vt_ref, o_ref, *, bounded, with_context):
    def body(rows, lat_tile):
        if lat_tile is None:
            chunks = [(SEQ, TOK)]
        else:
            chunks = [(lo, lo + GQA_KEY_CHUNK) for lo in range(0, TOK, GQA_KEY_CHUNK)]
        o_ref[rows, :] = _gqa_tile(q_ref[rows, :], k_ref, vt_ref, chunks, bounded)

    _for_each_subtile(body, with_context=with_context)


def _gqa_logit_bound(q_gain, k_gain):
    return (HEAD_DIM * jnp.max(jnp.abs(q_gain)) * jnp.max(jnp.abs(k_gain))
            * (QK_SCALE * LOG2_E) * (1.0 + 2.0 ** -7) ** 2)


def _gqa_call(qc, kc, vc_t, *, bounded, with_context):
    bn = qc.shape[0]
    return pl.pallas_call(
        functools.partial(_gqa_kernel, bounded=bounded, with_context=with_context),
        grid=(bn, N_STEPS if with_context else LAT_STEPS),
        in_specs=[_token_tile(GQA_Q_WIDTH),
                  pl.BlockSpec((None, TOK, GQA_KV_WIDTH), lambda b, t: (b, 0, 0)),
                  pl.BlockSpec((None, GQA_KV_WIDTH, TOK), lambda b, t: (b, 0, 0))],
        out_specs=_token_tile(GQA_Q_WIDTH),
        out_shape=jax.ShapeDtypeStruct((bn, TOK if with_context else SEQ, GQA_Q_WIDTH), BF16),
        compiler_params=_params(),
        name="gqa_bounded" if bounded else "gqa",
    )(qc, kc, vc_t)


def _rope_tables():
    pos = np.arange(SEQ)
    inv_freq = ROPE_THETA ** (-np.arange(0, HEAD_DIM // 2, 2, dtype=np.float32) / (HEAD_DIM // 2))
    ang_row = (pos // GRID_W).astype(np.float32)[:, None] * inv_freq[None, :].astype(np.float32)
    ang_col = (pos % GRID_W).astype(np.float32)[:, None] * inv_freq[None, :].astype(np.float32)
    ang = np.concatenate([ang_row, ang_row, ang_col, ang_col], axis=-1).astype(np.float32)
    sign = np.tile(np.repeat(np.array([-1.0, 1.0], np.float32), HEAD_DIM // 4), 2)
    return ang, sign


_ROPE_ANG, _ROPE_SIGN = _rope_tables()


def _group_major(a, axis):
    shape = a.shape
    a = a.reshape(shape[:axis] + (GQA_KV_HEADS, GQA_GROUP, HEAD_DIM) + shape[axis + 1:])
    return jnp.swapaxes(a, axis, axis + 1).reshape(shape)


def kernel(x, c, ctx, c_ctx, w_ada, b_ada, norm_g, ffn1_up, ffn1_down, ffn2_up, ffn2_down,
           w_in, w_out, na_rpb, pool_w, pool_scale, q_norm_g, k_norm_g, final_g):
    bn = x.shape[0]
    depth = w_ada.shape[0]
    assert x.shape == (bn, SEQ, D_MODEL) and ctx.shape == (bn, CTX_LEN, D_MODEL)
    assert bn < ADA_ROWS

    cc = jnp.zeros((ADA_ROWS, D_MODEL), F32).at[:bn].set(c).at[bn].set(c_ctx)
    mods = _ada_call(cc, w_ada, b_ada).reshape(depth, ADA_ROWS, N_MOD, D_MODEL)
    gains = norm_g.reshape(depth, 3, 1, D_MODEL)

    ang = jnp.asarray(_ROPE_ANG)
    cs_lat = jnp.tile(jnp.cos(ang), (1, 2))
    sn_lat = jnp.tile(jnp.sin(ang) * jnp.asarray(_ROPE_SIGN)[None, :], (1, 2))
    rope_cs = jnp.concatenate([cs_lat, jnp.ones((CTX_LEN, 2 * HEAD_DIM), F32)], axis=0)
    rope_sn = jnp.concatenate([sn_lat, jnp.zeros((CTX_LEN, 2 * HEAD_DIM), F32)], axis=0)
    ones_bd = jnp.asarray(
        np.kron(np.eye(NORM_BD // HEAD_DIM, dtype=np.float32),
                np.full((HEAD_DIM, HEAD_DIM), 1.0 / HEAD_DIM, np.float32))).astype(BF16)

    w_in_p = jnp.concatenate([w_in[:, :, :OFF_C_Q], _group_major(w_in[:, :, OFF_C_Q:OFF_C_K], 2),
                              w_in[:, :, OFF_C_K:]], axis=-1).astype(BF16)
    w_out_p = jnp.concatenate([w_out[:, :NA_WIDTH + POOL_WIDTH],
                               _group_major(w_out[:, NA_WIDTH + POOL_WIDTH:], 1)], axis=1).astype(BF16)
    pool_bd = jnp.einsum('gh,lgcd->lgchd', jnp.eye(len(POOL_WINDOWS), dtype=F32), pool_w)
    pool_bd = pool_bd.reshape(depth, POOL_WIDTH, POOL_WIDTH).astype(BF16)
    pool_sc = pool_scale.reshape(depth, 1, POOL_WIDTH)
    qg = jnp.tile(q_norm_g, (1, GQA_Q_HEADS)).reshape(depth, 1, GQA_Q_WIDTH)
    kg = jnp.tile(k_norm_g, (1, GQA_KV_HEADS)).reshape(depth, 1, GQA_KV_WIDTH)
    na_tb = _na_bias_table(na_rpb)
    na_rm = jnp.asarray(_NA_ROW_MASK)

    h = (x, ctx)
    for l in range(depth):
        last = l == depth - 1
        h, qa, ka, va, u, qc, kc, vc_t = _ffn_call(
            h, l, mods, gains, 0, ffn1_up, ffn1_down, mod_base=0,
            proj=(w_in_p, rope_cs, rope_sn, qg, kg, ones_bd))
        oa = _na_call(qa, ka, va, l, na_tb, na_rm, with_context=not last)
        ob = _pool_call(u, l, pool_bd, pool_sc, with_context=not last)
        oc = lax.cond(_gqa_logit_bound(q_norm_g[l], k_norm_g[l]) <= GQA_SAFE_LOGIT,
                      functools.partial(_gqa_call, qc, kc, vc_t, bounded=True, with_context=not last),
                      functools.partial(_gqa_call, qc, kc, vc_t, bounded=False, with_context=not last))
        h = _ffn_call(h, l, mods, gains, 2, ffn2_up, ffn2_down, mod_base=6,
                      mix=(oa, ob, oc, w_out_p), final_g=final_g if last else None)
    return h
```

```python
import functools

import numpy as np
import jax
import jax.numpy as jnp
from jax import lax
from jax.experimental import pallas as pl
from jax.experimental.pallas import tpu as pltpu

D_MODEL = 1024
SEQ = 2048
CTX_LEN = 256
TOK = SEQ + CTX_LEN
GRID_W = 64
GRID_H = SEQ // GRID_W
HEAD_DIM = 64
N_MOD = 9
D_FF = 2816
EPS = 1e-6
NEG_INF = -1e30
NA_HEADS = 4
NA_WIN_H = 8
NA_WIN_W = 16
NA_WIDTH = NA_HEADS * HEAD_DIM
POOL_WINDOWS = (2, 4, 8, 16)
POOL_WIDTH = 256
POOL_GC = POOL_WIDTH // len(POOL_WINDOWS)
GQA_Q_HEADS = 8
GQA_KV_HEADS = 2
GQA_GROUP = GQA_Q_HEADS // GQA_KV_HEADS
GQA_Q_WIDTH = GQA_Q_HEADS * HEAD_DIM
GQA_KV_WIDTH = GQA_KV_HEADS * HEAD_DIM
ROPE_THETA = 10000.0
OFF_A_Q = 0
OFF_A_K = OFF_A_Q + NA_WIDTH
OFF_A_V = OFF_A_K + NA_WIDTH
OFF_B_U = OFF_A_V + NA_WIDTH
OFF_C_Q = OFF_B_U + POOL_WIDTH
OFF_C_K = OFF_C_Q + GQA_Q_WIDTH
OFF_C_V = OFF_C_K + GQA_KV_WIDTH
D_IN = OFF_C_V + GQA_KV_WIDTH
D_MIX = NA_WIDTH + POOL_WIDTH + GQA_Q_WIDTH
QK_SCALE = HEAD_DIM ** -0.5
LOG2_E = float(np.log2(np.e))

TM = 256
LAT_TILES = SEQ // TM
STEP = 2 * TM
LAT_STEPS = SEQ // STEP
N_STEPS = LAT_STEPS + 1
NA_QROWS = TM // GRID_W
NA_KROWS = 12
NA_KEYS = NA_KROWS * GRID_W
NA_KPAIRS = NA_KROWS // 2
NA_DY = 2 * NA_WIN_H - 1
NA_DY_PAD = 4
NA_DY_SLOTS = NA_DY + 2 * NA_DY_PAD - 1
NA_PATTERN_GROUPS = (0, 1, 2, LAT_TILES - 1)
GQA_KEY_CHUNK = 768
GQA_SAFE_LOGIT = 64.0
FF_CHUNK = 256
FFN_UP_CHUNK = 64
FFN_DN_CHUNK = D_FF // 8
NORM_BD = 256
ADA_TN = 1152
ADA_ROWS = 16
VMEM_LIMIT = 56 * 1024 * 1024

BF16 = jnp.bfloat16
F32 = jnp.float32


def _dot(a, b):
    return jnp.dot(a, b, preferred_element_type=F32)


def _dot_nt(a, b):
    return lax.dot_general(a, b, (((1,), (1,)), ((), ())), preferred_element_type=F32)


def _rms(x, g):
    ms = jnp.mean(x * x, axis=-1, keepdims=True)
    return x * lax.rsqrt(ms + EPS) * g


def _layer_block(l, shape, *, single_buffer=False):
    nd = len(shape)
    mode = dict(pipeline_mode=pl.Buffered(1)) if single_buffer else {}
    return pl.BlockSpec((None,) + tuple(shape), lambda *_: (l,) + (0,) * nd, **mode)


def _mod_block(l, bn):
    return pl.BlockSpec((None, None, N_MOD, D_MODEL),
                        lambda b, t: (l, b + (t // LAT_STEPS) * (bn - b), 0, 0))


def _gain_block(l, which):
    return pl.BlockSpec((None, None, 1, D_MODEL), lambda b, t: (l, which, 0, 0))


def _token_tile(width):
    return pl.BlockSpec((None, STEP, width), lambda b, t: (b, t, 0))


def _for_each_subtile(*phases, with_context=True):
    t = pl.program_id(1)
    subtiles = [pl.ds(s * TM, TM) for s in range(STEP // TM)]

    def latent():
        for phase in phases:
            for s, rows in enumerate(subtiles):
                phase(rows, t * (STEP // TM) + s)

    if not with_context:
        latent()
        return
    pl.when(t < LAT_STEPS)(latent)

    @pl.when(t == LAT_STEPS)
    def _context():
        for phase in phases:
            phase(subtiles[0], None)


def _params(n_axes=2):
    return pltpu.CompilerParams(
        dimension_semantics=("arbitrary",) * n_axes, vmem_limit_bytes=VMEM_LIMIT)


def _split_bf16(a):
    hi = a.astype(BF16)
    return hi, (a - hi.astype(F32)).astype(BF16)


def _ada_kernel(c_ref, w_ref, b_ref, o_ref):
    c = c_ref[...]
    rows = c.shape[0]
    s_parts = jnp.concatenate(_split_bf16(c * jax.nn.sigmoid(c)), axis=0)
    acc = b_ref[...]
    for w_part in _split_bf16(w_ref[...]):
        partial = _dot(s_parts, w_part)
        acc = acc + partial[:rows] + partial[rows:]
    o_ref[...] = acc


def _ada_call(cc, w_ada, b_ada):
    depth = w_ada.shape[0]
    n = w_ada.shape[2]
    return pl.pallas_call(
        _ada_kernel,
        grid=(depth, n // ADA_TN),
        in_specs=[
            pl.BlockSpec((ADA_ROWS, D_MODEL), lambda l, j: (0, 0)),
            pl.BlockSpec((None, D_MODEL, ADA_TN), lambda l, j: (l, 0, j)),
            pl.BlockSpec((None, 1, ADA_TN), lambda l, j: (l, 0, j)),
        ],
        out_specs=pl.BlockSpec((None, ADA_ROWS, ADA_TN), lambda l, j: (l, 0, j)),
        out_shape=jax.ShapeDtypeStruct((depth, ADA_ROWS, n), F32),
        compiler_params=_params(),
        name="ada",
    )(cc, w_ada, b_ada.reshape(depth, 1, n))


def _stream_cast(src, dst_ref, stage_ref, sem, chunk):
    n_chunks = src.shape[0] // chunk

    def copy(i):
        return pltpu.make_async_copy(src.at[pl.ds(i * chunk, chunk), :], stage_ref.at[i % 2],
                                     sem.at[i % 2])

    copy(0).start()
    for i in range(n_chunks):
        if i + 1 < n_chunks:
            copy(i + 1).start()
        copy(i).wait()
        dst_ref[pl.ds(i * chunk, chunk), :] = stage_ref[i % 2].astype(BF16)


def _ffn_kernel(*refs, layer, mod_base, split_input, with_mix, with_proj, final):
    it = iter(refs)
    h_ref = next(it)
    if split_input:
        ctx_ref = next(it)
    if with_mix:
        oa_ref, ob_ref, oc_ref = next(it), next(it), next(it)
    mod_ref, g_ref = next(it), next(it)
    if with_mix:
        wout_ref = next(it)
    wup_hbm, wdn_hbm = next(it), next(it)
    if final:
        fg_ref = next(it)
    if with_proj:
        proj_in = [next(it) for _ in range(7)]
    o_ref = next(it)
    if with_proj:
        proj_out = [next(it) for _ in range(7)]
    hid_ref, xn_ref, wup_ref, wdn_ref, stage_up, stage_dn, sem_up, sem_dn = (next(it) for _ in range(8))

    @pl.when((pl.program_id(0) == 0) & (pl.program_id(1) == 0))
    def _load_weights():
        _stream_cast(wup_hbm.at[layer], wup_ref, stage_up, sem_up, FFN_UP_CHUNK)
        _stream_cast(wdn_hbm.at[layer], wdn_ref, stage_dn, sem_dn, FFN_DN_CHUNK)

    mod = mod_ref[...]

    def prepare(rows, lat_tile):
        if split_input and lat_tile is None:
            x = ctx_ref[...]
        else:
            x = h_ref[rows, :]
        if with_mix:
            mixed = jnp.concatenate([oa_ref[rows, :], ob_ref[rows, :], oc_ref[rows, :]], axis=-1)
            x = x + mod[5:6] * _dot(mixed, wout_ref[...])
        o_ref[rows, :] = x
        y = _rms(x, g_ref[...])
        xn_ref[rows, :] = (y * (1.0 + mod[mod_base + 1:mod_base + 2])
                           + mod[mod_base:mod_base + 1]).astype(BF16)

    def ffn(rows, _):
        xn = xn_ref[rows, :]
        for c in range(D_FF // FF_CHUNK):
            lo = c * FF_CHUNK
            a = _dot(xn, wup_ref[:, lo:lo + FF_CHUNK])
            b = _dot(xn, wup_ref[:, D_FF + lo:D_FF + lo + FF_CHUNK])
            hid_ref[rows, lo:lo + FF_CHUNK] = (a * jax.nn.sigmoid(a) * b).astype(BF16)
        out = o_ref[rows, :] + (0.5 * mod[mod_base + 2:mod_base + 3]) * _dot(hid_ref[rows, :], wdn_ref[...])
        if final:
            out = _rms(out, fg_ref[...])
        o_ref[rows, :] = out

    def project(rows, _):
        _proj_rows(o_ref[rows, :], rows, mod, *proj_in, *proj_out)

    phases = [prepare, ffn] + ([project] if with_proj else [])
    _for_each_subtile(*phases, with_context=not final)


def _ffn_call(h, l, mods, gains, which_gain, w_up, w_dn, *, mod_base, mix=None, proj=None,
              final_g=None):
    split_input = isinstance(h, tuple)
    bn = (h[0] if split_input else h).shape[0]
    with_mix = mix is not None
    with_proj = proj is not None
    final = final_g is not None
    n_steps = LAT_STEPS if final else N_STEPS
    if split_input:
        args = list(h)
        specs = [pl.BlockSpec((None, STEP, D_MODEL), lambda b, t: (b, jnp.minimum(t, LAT_STEPS - 1), 0)),
                 pl.BlockSpec((None, CTX_LEN, D_MODEL), lambda b, t: (b, 0, 0))]
    else:
        args, specs = [h], [_token_tile(D_MODEL)]
    if with_mix:
        oa, ob, oc, w_out = mix
        args += [oa, ob, oc]
        specs += [_token_tile(NA_WIDTH), _token_tile(POOL_WIDTH), _token_tile(GQA_Q_WIDTH)]
    args += [mods, gains]
    specs += [_mod_block(l, bn), _gain_block(l, which_gain)]
    if with_mix:
        args.append(w_out)
        specs.append(_layer_block(l, (D_MIX, D_MODEL), single_buffer=True))
    args += [w_up, w_dn]
    specs += [pl.BlockSpec(memory_space=pl.ANY), pl.BlockSpec(memory_space=pl.ANY)]
    if final:
        args.append(final_g.reshape(1, D_MODEL))
        specs.append(pl.BlockSpec((1, D_MODEL), lambda b, t: (0, 0)))
    out_specs = [_token_tile(D_MODEL)]
    out_shape = [jax.ShapeDtypeStruct((bn, SEQ if final else TOK, D_MODEL), F32)]
    if with_proj:
        w_in, rope_cs, rope_sn, qg, kg, ones_bd = proj
        args += [gains, w_in, rope_cs, rope_sn, qg, kg, ones_bd]
        specs += [
            _gain_block(l, 1),
            _layer_block(l, (D_MODEL, D_IN), single_buffer=True),
            pl.BlockSpec((STEP, 2 * HEAD_DIM), lambda b, t: (t, 0)),
            pl.BlockSpec((STEP, 2 * HEAD_DIM), lambda b, t: (t, 0)),
            _layer_block(l, (1, GQA_Q_WIDTH)),
            _layer_block(l, (1, GQA_KV_WIDTH)),
            pl.BlockSpec((NORM_BD, NORM_BD), lambda b, t: (0, 0)),
        ]
        rows_of = lambda w, dt: jax.ShapeDtypeStruct((bn, TOK, w), dt)
        out_specs += [_token_tile(NA_WIDTH), _token_tile(NA_WIDTH), _token_tile(NA_WIDTH),
                      _token_tile(POOL_WIDTH), _token_tile(GQA_Q_WIDTH), _token_tile(GQA_KV_WIDTH),
                      pl.BlockSpec((None, GQA_KV_WIDTH, STEP), lambda b, t: (b, 0, t))]
        out_shape += [rows_of(NA_WIDTH, BF16), rows_of(NA_WIDTH, BF16), rows_of(NA_WIDTH, BF16),
                      rows_of(POOL_WIDTH, F32), rows_of(GQA_Q_WIDTH, BF16), rows_of(GQA_KV_WIDTH, BF16),
                      jax.ShapeDtypeStruct((bn, GQA_KV_WIDTH, TOK), BF16)]
    outs = pl.pallas_call(
        functools.partial(_ffn_kernel, layer=l, mod_base=mod_base, split_input=split_input,
                          with_mix=with_mix, with_proj=with_proj, final=final),
        grid=(bn, n_steps),
        in_specs=specs,
        out_specs=out_specs,
        out_shape=out_shape,
        scratch_shapes=[pltpu.VMEM((STEP, D_FF), BF16),
                        pltpu.VMEM((STEP, D_MODEL), BF16),
                        pltpu.VMEM((D_MODEL, 2 * D_FF), BF16),
                        pltpu.VMEM((D_FF, D_MODEL), BF16),
                        pltpu.VMEM((2, FFN_UP_CHUNK, 2 * D_FF), F32),
                        pltpu.VMEM((2, FFN_DN_CHUNK, D_MODEL), F32),
                        pltpu.SemaphoreType.DMA((2,)),
                        pltpu.SemaphoreType.DMA((2,))],
        compiler_params=_params(),
        name="ffn_mix" if with_mix else "ffn_proj" if with_proj else "ffn",
    )(*args)
    return outs if with_proj else outs[0]


def _head_norm_rope(z, gain, cs, sn, ones_bd):
    width = z.shape[-1]
    zz = z * z
    hi = zz.astype(BF16)
    lo = (zz - hi.astype(F32)).astype(BF16)
    span = min(width, ones_bd.shape[0])
    bd = ones_bd[:span, :span]
    ms = jnp.concatenate([_dot(hi[:, c:c + span], bd) + _dot(lo[:, c:c + span], bd)
                          for c in range(0, width, span)], axis=-1)
    zn = z * lax.rsqrt(ms + EPS) * gain
    lane = lax.broadcasted_iota(jnp.int32, zn.shape, 1)
    partner = jnp.where((lane & 16) != 0,
                        pltpu.roll(zn, 16, axis=1), pltpu.roll(zn, width - 16, axis=1))
    reps = width // cs.shape[-1]
    if reps > 1:
        cs = jnp.concatenate([cs] * reps, axis=-1)
        sn = jnp.concatenate([sn] * reps, axis=-1)
    return zn * cs + partner * sn


def _proj_rows(x, rows, mod, g_ref, win_ref, cs_ref, sn_ref, qg_ref, kg_ref, bd_ref,
               qa_ref, ka_ref, va_ref, u_ref, qc_ref, kc_ref, vct_ref):
    bd = bd_ref[...]
    y = _rms(x, g_ref[...])
    a = (y * (1.0 + mod[4:5]) + mod[3:4]).astype(BF16)
    p = _dot(a, win_ref[...])
    qa_ref[rows, :] = (p[:, OFF_A_Q:OFF_A_K] * (QK_SCALE * LOG2_E)).astype(BF16)
    ka_ref[rows, :] = p[:, OFF_A_K:OFF_A_V].astype(BF16)
    va_ref[rows, :] = p[:, OFF_A_V:OFF_B_U].astype(BF16)
    u_ref[rows, :] = p[:, OFF_B_U:OFF_C_Q]
    cs, sn = cs_ref[rows, :], sn_ref[rows, :]
    qc = _head_norm_rope(p[:, OFF_C_Q:OFF_C_K], qg_ref[...], cs, sn, bd)
    qc_ref[rows, :] = (qc * (QK_SCALE * LOG2_E)).astype(BF16)
    kc = _head_norm_rope(p[:, OFF_C_K:OFF_C_V], kg_ref[...], cs, sn, bd)
    kc_ref[rows, :] = kc.astype(BF16)
    vct_ref[:, rows] = p[:, OFF_C_V:D_IN].T.astype(BF16)


def _softmax_pv(scores, values):
    m = scores[0].max(axis=-1, keepdims=True)
    for s in scores[1:]:
        m = jnp.maximum(m, s.max(axis=-1, keepdims=True))
    den = None
    acc = None
    for s, v in zip(scores, values):
        e = jnp.exp2(s - m)
        d = e.sum(axis=-1, keepdims=True)
        o = _dot(e.astype(BF16), v)
        den = d if den is None else den + d
        acc = o if acc is None else acc + o
    return acc / den


def _na_kernel(q_ref, k_ref, v_ref, tb_ref, rm_ref, o_ref, *, with_context):
    head_of_lane = lax.broadcasted_iota(jnp.int32, (1, NA_WIDTH), 1) // HEAD_DIM
    k_ctx = k_ref[SEQ:TOK, :]
    v_ctx = v_ref[SEQ:TOK, :]

    def context(rows):
        q = q_ref[rows, :]
        out = jnp.zeros((TM, NA_WIDTH), F32)
        for h in range(NA_HEADS):
            mine = head_of_lane == h
            qh = jnp.where(mine, q, jnp.zeros_like(q))
            out = jnp.where(mine, _softmax_pv([_dot_nt(qh, k_ctx)], [v_ctx]), out)
        o_ref[rows, :] = out.astype(BF16)

    def body(rows, g):
        if g is None:
            return context(rows)
        q = q_ref[rows, :]
        first_row = jnp.clip(g * NA_QROWS - NA_WIN_H // 2, 0, GRID_H - NA_KROWS)
        start = pl.multiple_of(first_row * GRID_W, GRID_W)
        k_lat = k_ref[pl.ds(start, NA_KEYS), :]
        v_lat = v_ref[pl.ds(start, NA_KEYS), :]
        pattern = jnp.minimum(g, 2) + (g == LAT_TILES - 1).astype(jnp.int32)
        slot0 = first_row - g * NA_QROWS + NA_WIN_H - 1 + NA_DY_PAD

        def scores(h):
            bias = jnp.concatenate([
                jnp.concatenate([
                    tb_ref[h, slot0 + 2 * m - qr]
                    + rm_ref[pattern, qr * NA_KPAIRS + m:qr * NA_KPAIRS + m + 1, :]
                    for m in range(NA_KPAIRS)], axis=-1)
                for qr in range(NA_QROWS)], axis=0)
            qh = jnp.where(head_of_lane == h, q, jnp.zeros_like(q))
            return [_dot_nt(qh, k_lat) + bias, _dot_nt(qh, k_ctx)]

        out = jnp.zeros((TM, NA_WIDTH), F32)
        s_next = scores(0)
        for h in range(NA_HEADS):
            s_cur = s_next
            if h + 1 < NA_HEADS:
                s_next = scores(h + 1)
            out = jnp.where(head_of_lane == h, _softmax_pv(s_cur, [v_lat, v_ctx]), out)
        o_ref[rows, :] = out.astype(BF16)

    _for_each_subtile(body, with_context=with_context)


def _na_call(qa, ka, va, l, tb, rm, *, with_context):
    bn = qa.shape[0]
    whole = pl.BlockSpec((None, TOK, NA_WIDTH), lambda b, t: (b, 0, 0))
    return pl.pallas_call(
        functools.partial(_na_kernel, with_context=with_context),
        grid=(bn, N_STEPS if with_context else LAT_STEPS),
        in_specs=[
            _token_tile(NA_WIDTH), whole, whole,
            _layer_block(l, tb.shape[1:], single_buffer=True),
            pl.BlockSpec(rm.shape, lambda b, t: (0, 0, 0)),
        ],
        out_specs=_token_tile(NA_WIDTH),
        out_shape=jax.ShapeDtypeStruct((bn, TOK if with_context else SEQ, NA_WIDTH), BF16),
        compiler_params=_params(),
        name="na",
    )(qa, ka, va, tb, rm)


def _na_static_tables():
    col = np.arange(GRID_W)
    win_c0 = np.clip(col - NA_WIN_W // 2, 0, GRID_W - NA_WIN_W)
    col_ok = (col[None, :] >= win_c0[:, None]) & (col[None, :] < win_c0[:, None] + NA_WIN_W)
    dx = np.clip(col[None, :] - col[:, None], -(NA_WIN_W - 1), NA_WIN_W - 1) + NA_WIN_W - 1
    sel_x = np.eye(2 * NA_WIN_W - 1, dtype=np.float32)[dx]
    row_mask = np.zeros((len(NA_PATTERN_GROUPS), NA_QROWS, NA_KPAIRS, 2, GRID_W), np.float32)
    for p, grp in enumerate(NA_PATTERN_GROUPS):
        first_row = int(np.clip(grp * NA_QROWS - NA_WIN_H // 2, 0, GRID_H - NA_KROWS))
        q_row = grp * NA_QROWS + np.arange(NA_QROWS)
        q_r0 = np.clip(q_row - NA_WIN_H // 2, 0, GRID_H - NA_WIN_H)
        k_row = first_row + np.arange(NA_KROWS)
        row_ok = (k_row[None, :] >= q_r0[:, None]) & (k_row[None, :] < q_r0[:, None] + NA_WIN_H)
        row_mask[p] = np.where(row_ok, 0.0, NEG_INF).reshape(NA_QROWS, NA_KPAIRS, 2, 1)
    row_mask = row_mask.reshape(len(NA_PATTERN_GROUPS), NA_QROWS * NA_KPAIRS, 2 * GRID_W)
    return sel_x, col_ok, row_mask


_NA_SEL_X, _NA_COL_OK, _NA_ROW_MASK = _na_static_tables()


def _na_bias_table(rpb):
    t = jnp.einsum('lhab,xcb->lhaxc', rpb, _NA_SEL_X, precision=lax.Precision.HIGHEST) * LOG2_E
    t = jnp.where(_NA_COL_OK, t, NEG_INF)
    t = jnp.pad(t, ((0, 0), (0, 0), (NA_DY_PAD, NA_DY_PAD), (0, 0), (0, 0)))
    return jnp.concatenate([t[:, :, :-1], t[:, :, 1:]], axis=-1)


def _shift_rows(a, d):
    n = a.shape[0]
    row = lax.broadcasted_iota(jnp.int32, a.shape, 0)
    rolled = pltpu.roll(a, d % n, axis=0)
    ok = (row >= d) if d > 0 else (row < n + d)
    return jnp.where(ok, rolled, 0.0)


def _pool_lanes(u, windows):
    n = u.shape[0]
    row = lax.broadcasted_iota(jnp.int32, u.shape, 0)
    grp = lax.broadcasted_iota(jnp.int32, u.shape, 1) // POOL_GC
    trail, lead = u, u
    total = jnp.zeros_like(u)
    count = jnp.ones_like(u)
    k = 1
    for i, w in enumerate(windows):
        while k < w // 2:
            trail = trail + _shift_rows(trail, k)
            lead = lead + _shift_rows(lead, -k)
            k *= 2
        win = _shift_rows(trail, 1) + lead
        cnt = (jnp.minimum(row + w // 2, n) - jnp.maximum(row - w // 2, 0)).astype(F32)
        total = jnp.where(grp == i, win, total)
        count = jnp.where(grp == i, cnt, count)
    return total / count - u


def _pool_segment(u, w_bd, scale):
    half = 2 * POOL_GC
    y = jnp.concatenate([_pool_lanes(u[:, c:c + half], POOL_WINDOWS[c // POOL_GC:(c + half) // POOL_GC])
                         for c in range(0, POOL_WIDTH, half)], axis=-1).astype(BF16)
    return (_dot(y, w_bd) * scale).astype(BF16)


def _pool_kernel(u_ref, w_ref, s_ref, o_ref, *, with_context):
    w_bd = w_ref[...]
    scale = s_ref[...]
    o_ref[:SEQ, :] = _pool_segment(u_ref[:SEQ, :], w_bd, scale)
    if with_context:
        o_ref[SEQ:, :] = _pool_segment(u_ref[SEQ:, :], w_bd, scale)


def _pool_call(u, l, w_bd, scale, *, with_context):
    bn = u.shape[0]
    rows = TOK if with_context else SEQ
    return pl.pallas_call(
        functools.partial(_pool_kernel, with_context=with_context),
        grid=(bn,),
        in_specs=[pl.BlockSpec((None, TOK, POOL_WIDTH), lambda b: (b, 0, 0)),
                  _layer_block(l, (POOL_WIDTH, POOL_WIDTH)), _layer_block(l, (1, POOL_WIDTH))],
        out_specs=pl.BlockSpec((None, rows, POOL_WIDTH), lambda b: (b, 0, 0)),
        out_shape=jax.ShapeDtypeStruct((bn, rows, POOL_WIDTH), BF16),
        compiler_params=_params(1),
        name="pool",
    )(u, w_bd, scale)


def _gqa_tile(q, k_ref, vt_ref, key_chunks, bounded):
    half_of_lane = lax.broadcasted_iota(jnp.int32, (1, GQA_KV_WIDTH), 1) // HEAD_DIM
    stages = [(j, lo, hi) for j in range(GQA_GROUP) for lo, hi in key_chunks]

    def scores_t(stage):
        j, lo, hi = stage
        qb = q[:, j * GQA_KV_WIDTH:(j + 1) * GQA_KV_WIDTH]
        q2 = jnp.concatenate([jnp.where(half_of_lane == kv, qb, jnp.zeros_like(qb))
                              for kv in range(GQA_KV_HEADS)], axis=0)
        return _dot_nt(k_ref[lo:hi, :], q2)

    partial = {j: [] for j in range(GQA_GROUP)}
    s_next = scores_t(stages[0])
    for i, (j, lo, hi) in enumerate(stages):
        s_t = s_next
        if i + 1 < len(stages):
            s_next = scores_t(stages[i + 1])
        if bounded:
            m = None
            e = jnp.exp2(s_t)
        else:
            m = s_t.max(axis=0, keepdims=True)
            e = jnp.exp2(s_t - m)
        den = e.sum(axis=0, keepdims=True)
        e = e.astype(BF16)
        o_t = [_dot(vt_ref[kv * HEAD_DIM:(kv + 1) * HEAD_DIM, lo:hi], e[:, kv * TM:(kv + 1) * TM])
               for kv in range(GQA_KV_HEADS)]
        partial[j].append((m, den, o_t))

    blocks = []
    for j in range(GQA_GROUP):
        if bounded:
            weights = [1.0] * len(partial[j])
        else:
            m_all = functools.reduce(jnp.maximum, [m for m, _, _ in partial[j]])
            weights = [jnp.exp2(m - m_all) for m, _, _ in partial[j]]
        den = sum(w * d for w, (_, d, _) in zip(weights, partial[j]))
        o_t = jnp.concatenate([
            sum((w if bounded else w[:, kv * TM:(kv + 1) * TM]) * o[kv]
                for w, (_, _, o) in zip(weights, partial[j]))
            / den[:, kv * TM:(kv + 1) * TM]
            for kv in range(GQA_KV_HEADS)], axis=0)
        blocks.append(o_t.T)
    return jnp.concatenate(blocks, axis=-1).astype(BF16)


def _gqa_kernel(q_ref, k_ref, vt_ref, o_ref, *, bounded, with_context):
    def body(rows, lat_tile):
        if lat_tile is None:
            chunks = [(SEQ, TOK)]
        else:
            chunks = [(lo, lo + GQA_KEY_CHUNK) for lo in range(0, TOK, GQA_KEY_CHUNK)]
        o_ref[rows, :] = _gqa_tile(q_ref[rows, :], k_ref, vt_ref, chunks, bounded)

    _for_each_subtile(body, with_context=with_context)


def _gqa_logit_bound(q_gain, k_gain):
    return (HEAD_DIM * jnp.max(jnp.abs(q_gain)) * jnp.max(jnp.abs(k_gain))
            * (QK_SCALE * LOG2_E) * (1.0 + 2.0 ** -7) ** 2)


def _gqa_call(qc, kc, vc_t, *, bounded, with_context):
    bn = qc.shape[0]
    return pl.pallas_call(
        functools.partial(_gqa_kernel, bounded=bounded, with_context=with_context),
        grid=(bn, N_STEPS if with_context else LAT_STEPS),
        in_specs=[_token_tile(GQA_Q_WIDTH),
                  pl.BlockSpec((None, TOK, GQA_KV_WIDTH), lambda b, t: (b, 0, 0)),
                  pl.BlockSpec((None, GQA_KV_WIDTH, TOK), lambda b, t: (b, 0, 0))],
        out_specs=_token_tile(GQA_Q_WIDTH),
        out_shape=jax.ShapeDtypeStruct((bn, TOK if with_context else SEQ, GQA_Q_WIDTH), BF16),
        compiler_params=_params(),
        name="gqa_bounded" if bounded else "gqa",
    )(qc, kc, vc_t)


def _rope_tables():
    pos = np.arange(SEQ)
    inv_freq = ROPE_THETA ** (-np.arange(0, HEAD_DIM // 2, 2, dtype=np.float32) / (HEAD_DIM // 2))
    ang_row = (pos // GRID_W).astype(np.float32)[:, None] * inv_freq[None, :].astype(np.float32)
    ang_col = (pos % GRID_W).astype(np.float32)[:, None] * inv_freq[None, :].astype(np.float32)
    ang = np.concatenate([ang_row, ang_row, ang_col, ang_col], axis=-1).astype(np.float32)
    sign = np.tile(np.repeat(np.array([-1.0, 1.0], np.float32), HEAD_DIM // 4), 2)
    return ang, sign


_ROPE_ANG, _ROPE_SIGN = _rope_tables()


def _group_major(a, axis):
    shape = a.shape
    a = a.reshape(shape[:axis] + (GQA_KV_HEADS, GQA_GROUP, HEAD_DIM) + shape[axis + 1:])
    return jnp.swapaxes(a, axis, axis + 1).reshape(shape)


def kernel(x, c, ctx, c_ctx, w_ada, b_ada, norm_g, ffn1_up, ffn1_down, ffn2_up, ffn2_down,
           w_in, w_out, na_rpb, pool_w, pool_scale, q_norm_g, k_norm_g, final_g):
    bn = x.shape[0]
    depth = w_ada.shape[0]
    assert x.shape == (bn, SEQ, D_MODEL) and ctx.shape == (bn, CTX_LEN, D_MODEL)
    assert bn < ADA_ROWS

    cc = jnp.zeros((ADA_ROWS, D_MODEL), F32).at[:bn].set(c).at[bn].set(c_ctx)
    mods = _ada_call(cc, w_ada, b_ada).reshape(depth, ADA_ROWS, N_MOD, D_MODEL)
    gains = norm_g.reshape(depth, 3, 1, D_MODEL)

    ang = jnp.asarray(_ROPE_ANG)
    cs_lat = jnp.tile(jnp.cos(ang), (1, 2))
    sn_lat = jnp.tile(jnp.sin(ang) * jnp.asarray(_ROPE_SIGN)[None, :], (1, 2))
    rope_cs = jnp.concatenate([cs_lat, jnp.ones((CTX_LEN, 2 * HEAD_DIM), F32)], axis=0)
    rope_sn = jnp.concatenate([sn_lat, jnp.zeros((CTX_LEN, 2 * HEAD_DIM), F32)], axis=0)
    ones_bd = jnp.asarray(
        np.kron(np.eye(NORM_BD // HEAD_DIM, dtype=np.float32),
                np.full((HEAD_DIM, HEAD_DIM), 1.0 / HEAD_DIM, np.float32))).astype(BF16)

    w_in_p = jnp.concatenate([w_in[:, :, :OFF_C_Q], _group_major(w_in[:, :, OFF_C_Q:OFF_C_K], 2),
                              w_in[:, :, OFF_C_K:]], axis=-1).astype(BF16)
    w_out_p = jnp.concatenate([w_out[:, :NA_WIDTH + POOL_WIDTH],
                               _group_major(w_out[:, NA_WIDTH + POOL_WIDTH:], 1)], axis=1).astype(BF16)
    pool_bd = jnp.einsum('gh,lgcd->lgchd', jnp.eye(len(POOL_WINDOWS), dtype=F32), pool_w)
    pool_bd = pool_bd.reshape(depth, POOL_WIDTH, POOL_WIDTH).astype(BF16)
    pool_sc = pool_scale.reshape(depth, 1, POOL_WIDTH)
    qg = jnp.tile(q_norm_g, (1, GQA_Q_HEADS)).reshape(depth, 1, GQA_Q_WIDTH)
    kg = jnp.tile(k_norm_g, (1, GQA_KV_HEADS)).reshape(depth, 1, GQA_KV_WIDTH)
    na_tb = _na_bias_table(na_rpb)
    na_rm = jnp.asarray(_NA_ROW_MASK)

    h = (x, ctx)
    for l in range(depth):
        last = l == depth - 1
        h, qa, ka, va, u, qc, kc, vc_t = _ffn_call(
            h, l, mods, gains, 0, ffn1_up, ffn1_down, mod_base=0,
            proj=(w_in_p, rope_cs, rope_sn, qg, kg, ones_bd))
        oa = _na_call(qa, ka, va, l, na_tb, na_rm, with_context=not last)
        ob = _pool_call(u, l, pool_bd, pool_sc, with_context=not last)
        oc = lax.cond(_gqa_logit_bound(q_norm_g[l], k_norm_g[l]) <= GQA_SAFE_LOGIT,
                      functools.partial(_gqa_call, qc, kc, vc_t, bounded=True, with_context=not last),
                      functools.partial(_gqa_call, qc, kc, vc_t, bounded=False, with_context=not last))
        h = _ffn_call(h, l, mods, gains, 2, ffn2_up, ffn2_down, mod_base=6,
                      mix=(oa, ob, oc, w_out_p), final_g=final_g if last else None)
    return h
```

```python
import functools

import numpy as np
import jax
import jax.numpy as jnp
from jax import lax
from jax.experimental import pallas as pl
from jax.experimental.pallas import tpu as pltpu

D_MODEL = 1024
SEQ = 2048
CTX_LEN = 256
TOK = SEQ + CTX_LEN
GRID_W = 64
GRID_H = SEQ // GRID_W
HEAD_DIM = 64
N_MOD = 9
D_FF = 2816
EPS = 1e-6
NEG_INF = -1e30
NA_HEADS = 4
NA_WIN_H = 8
NA_WIN_W = 16
NA_WIDTH = NA_HEADS * HEAD_DIM
POOL_WINDOWS = (2, 4, 8, 16)
POOL_WIDTH = 256
POOL_GC = POOL_WIDTH // len(POOL_WINDOWS)
GQA_Q_HEADS = 8
GQA_KV_HEADS = 2
GQA_GROUP = GQA_Q_HEADS // GQA_KV_HEADS
GQA_Q_WIDTH = GQA_Q_HEADS * HEAD_DIM
GQA_KV_WIDTH = GQA_KV_HEADS * HEAD_DIM
ROPE_THETA = 10000.0
OFF_A_Q = 0
OFF_A_K = OFF_A_Q + NA_WIDTH
OFF_A_V = OFF_A_K + NA_WIDTH
OFF_B_U = OFF_A_V + NA_WIDTH
OFF_C_Q = OFF_B_U + POOL_WIDTH
OFF_C_K = OFF_C_Q + GQA_Q_WIDTH
OFF_C_V = OFF_C_K + GQA_KV_WIDTH
D_IN = OFF_C_V + GQA_KV_WIDTH
D_MIX = NA_WIDTH + POOL_WIDTH + GQA_Q_WIDTH
QK_SCALE = HEAD_DIM ** -0.5
LOG2_E = float(np.log2(np.e))

TM = 256
LAT_TILES = SEQ // TM
STEP = 2 * TM
LAT_STEPS = SEQ // STEP
N_STEPS = LAT_STEPS + 1
NA_QROWS = TM // GRID_W
NA_KROWS = 12
NA_KEYS = NA_KROWS * GRID_W
NA_KPAIRS = NA_KROWS // 2
NA_DY = 2 * NA_WIN_H - 1
NA_DY_PAD = 4
NA_DY_SLOTS = NA_DY + 2 * NA_DY_PAD - 1
NA_PATTERN_GROUPS = (0, 1, 2, LAT_TILES - 1)
GQA_KEY_CHUNK = 768
GQA_SAFE_LOGIT = 64.0
FF_CHUNK = 256
FFN_UP_CHUNK = 64
FFN_DN_CHUNK = D_FF // 8
NORM_BD = 256
ADA_TN = 2304
ADA_ROWS = 16
VMEM_LIMIT = 56 * 1024 * 1024

BF16 = jnp.bfloat16
F32 = jnp.float32


def _dot(a, b):
    return jnp.dot(a, b, preferred_element_type=F32)


def _dot_nt(a, b):
    return lax.dot_general(a, b, (((1,), (1,)), ((), ())), preferred_element_type=F32)


def _rms(x, g):
    ms = jnp.mean(x * x, axis=-1, keepdims=True)
    return x * lax.rsqrt(ms + EPS) * g


def _layer_block(l, shape, *, single_buffer=False):
    nd = len(shape)
    mode = dict(pipeline_mode=pl.Buffered(1)) if single_buffer else {}
    return pl.BlockSpec((None,) + tuple(shape), lambda *_: (l,) + (0,) * nd, **mode)


def _mod_block(l, bn):
    return pl.BlockSpec((None, None, N_MOD, D_MODEL),
                        lambda b, t: (l, b + (t // LAT_STEPS) * (bn - b), 0, 0))


def _gain_block(l, which):
    return pl.BlockSpec((None, None, 1, D_MODEL), lambda b, t: (l, which, 0, 0))


def _token_tile(width):
    return pl.BlockSpec((None, STEP, width), lambda b, t: (b, t, 0))


def _for_each_subtile(*phases, with_context=True):
    t = pl.program_id(1)
    subtiles = [pl.ds(s * TM, TM) for s in range(STEP // TM)]

    def latent():
        for phase in phases:
            for s, rows in enumerate(subtiles):
                phase(rows, t * (STEP // TM) + s)

    if not with_context:
        latent()
        return
    pl.when(t < LAT_STEPS)(latent)

    @pl.when(t == LAT_STEPS)
    def _context():
        for phase in phases:
            phase(subtiles[0], None)


def _params(n_axes=2):
    return pltpu.CompilerParams(
        dimension_semantics=("arbitrary",) * n_axes, vmem_limit_bytes=VMEM_LIMIT)


def _split_bf16(a):
    hi = a.astype(BF16)
    return hi, (a - hi.astype(F32)).astype(BF16)


def _ada_kernel(c_ref, w_ref, b_ref, o_ref):
    c = c_ref[...]
    rows = c.shape[0]
    s_parts = jnp.concatenate(_split_bf16(c * jax.nn.sigmoid(c)), axis=0)
    acc = b_ref[...]
    for w_part in _split_bf16(w_ref[...]):
        partial = _dot(s_parts, w_part)
        acc = acc + partial[:rows] + partial[rows:]
    o_ref[...] = acc


def _ada_call(cc, w_ada, b_ada):
    depth = w_ada.shape[0]
    n = w_ada.shape[2]
    return pl.pallas_call(
        _ada_kernel,
        grid=(depth, n // ADA_TN),
        in_specs=[
            pl.BlockSpec((ADA_ROWS, D_MODEL), lambda l, j: (0, 0)),
            pl.BlockSpec((None, D_MODEL, ADA_TN), lambda l, j: (l, 0, j)),
            pl.BlockSpec((None, 1, ADA_TN), lambda l, j: (l, 0, j)),
        ],
        out_specs=pl.BlockSpec((None, ADA_ROWS, ADA_TN), lambda l, j: (l, 0, j)),
        out_shape=jax.ShapeDtypeStruct((depth, ADA_ROWS, n), F32),
        compiler_params=_params(),
        name="ada",
    )(cc, w_ada, b_ada.reshape(depth, 1, n))


def _stream_cast(src, dst_ref, stage_ref, sem, chunk):
    n_chunks = src.shape[0] // chunk

    def copy(i):
        return pltpu.make_async_copy(src.at[pl.ds(i * chunk, chunk), :], stage_ref.at[i % 2],
                                     sem.at[i % 2])

    copy(0).start()
    for i in range(n_chunks):
        if i + 1 < n_chunks:
            copy(i + 1).start()
        copy(i).wait()
        dst_ref[pl.ds(i * chunk, chunk), :] = stage_ref[i % 2].astype(BF16)


def _ffn_kernel(*refs, layer, mod_base, split_input, with_mix, with_proj, final):
    it = iter(refs)
    h_ref = next(it)
    if split_input:
        ctx_ref = next(it)
    if with_mix:
        oa_ref, ob_ref, oc_ref = next(it), next(it), next(it)
    mod_ref, g_ref = next(it), next(it)
    if with_mix:
        wout_ref = next(it)
    wup_hbm, wdn_hbm = next(it), next(it)
    if final:
        fg_ref = next(it)
    if with_proj:
        proj_in = [next(it) for _ in range(7)]
    o_ref = next(it)
    if with_proj:
        proj_out = [next(it) for _ in range(7)]
    hid_ref, xn_ref, wup_ref, wdn_ref, stage_up, stage_dn, sem_up, sem_dn = (next(it) for _ in range(8))

    @pl.when((pl.program_id(0) == 0) & (pl.program_id(1) == 0))
    def _load_weights():
        _stream_cast(wup_hbm.at[layer], wup_ref, stage_up, sem_up, FFN_UP_CHUNK)
        _stream_cast(wdn_hbm.at[layer], wdn_ref, stage_dn, sem_dn, FFN_DN_CHUNK)

    mod = mod_ref[...]

    def prepare(rows, lat_tile):
        if split_input and lat_tile is None:
            x = ctx_ref[...]
        else:
            x = h_ref[rows, :]
        if with_mix:
            mixed = jnp.concatenate([oa_ref[rows, :], ob_ref[rows, :], oc_ref[rows, :]], axis=-1)
            x = x + mod[5:6] * _dot(mixed, wout_ref[...])
        o_ref[rows, :] = x
        y = _rms(x, g_ref[...])
        xn_ref[rows, :] = (y * (1.0 + mod[mod_base + 1:mod_base + 2])
                           + mod[mod_base:mod_base + 1]).astype(BF16)

    def ffn(rows, _):
        xn = xn_ref[rows, :]
        for c in range(D_FF // FF_CHUNK):
            lo = c * FF_CHUNK
            a = _dot(xn, wup_ref[:, lo:lo + FF_CHUNK])
            b = _dot(xn, wup_ref[:, D_FF + lo:D_FF + lo + FF_CHUNK])
            hid_ref[rows, lo:lo + FF_CHUNK] = (a * jax.nn.sigmoid(a) * b).astype(BF16)
        out = o_ref[rows, :] + (0.5 * mod[mod_base + 2:mod_base + 3]) * _dot(hid_ref[rows, :], wdn_ref[...])
        if final:
            out = _rms(out, fg_ref[...])
        o_ref[rows, :] = out

    def project(rows, _):
        _proj_rows(o_ref[rows, :], rows, mod, *proj_in, *proj_out)

    phases = [prepare, ffn] + ([project] if with_proj else [])
    _for_each_subtile(*phases, with_context=not final)


def _ffn_call(h, l, mods, gains, which_gain, w_up, w_dn, *, mod_base, mix=None, proj=None,
              final_g=None):
    split_input = isinstance(h, tuple)
    bn = (h[0] if split_input else h).shape[0]
    with_mix = mix is not None
    with_proj = proj is not None
    final = final_g is not None
    n_steps = LAT_STEPS if final else N_STEPS
    if split_input:
        args = list(h)
        specs = [pl.BlockSpec((None, STEP, D_MODEL), lambda b, t: (b, jnp.minimum(t, LAT_STEPS - 1), 0)),
                 pl.BlockSpec((None, CTX_LEN, D_MODEL), lambda b, t: (b, 0, 0))]
    else:
        args, specs = [h], [_token_tile(D_MODEL)]
    if with_mix:
        oa, ob, oc, w_out = mix
        args += [oa, ob, oc]
        specs += [_token_tile(NA_WIDTH), _token_tile(POOL_WIDTH), _token_tile(GQA_Q_WIDTH)]
    args += [mods, gains]
    specs += [_mod_block(l, bn), _gain_block(l, which_gain)]
    if with_mix:
        args.append(w_out)
        specs.append(_layer_block(l, (D_MIX, D_MODEL), single_buffer=True))
    args += [w_up, w_dn]
    specs += [pl.BlockSpec(memory_space=pl.ANY), pl.BlockSpec(memory_space=pl.ANY)]
    if final:
        args.append(final_g.reshape(1, D_MODEL))
        specs.append(pl.BlockSpec((1, D_MODEL), lambda b, t: (0, 0)))
    out_specs = [_token_tile(D_MODEL)]
    out_shape = [jax.ShapeDtypeStruct((bn, SEQ if final else TOK, D_MODEL), F32)]
    if with_proj:
        w_in, rope_cs, rope_sn, qg, kg, ones_bd = proj
        args += [gains, w_in, rope_cs, rope_sn, qg, kg, ones_bd]
        specs += [
            _gain_block(l, 1),
            _layer_block(l, (D_MODEL, D_IN), single_buffer=True),
            pl.BlockSpec((STEP, 2 * HEAD_DIM), lambda b, t: (t, 0)),
            pl.BlockSpec((STEP, 2 * HEAD_DIM), lambda b, t: (t, 0)),
            _layer_block(l, (1, GQA_Q_WIDTH)),
            _layer_block(l, (1, GQA_KV_WIDTH)),
            pl.BlockSpec((NORM_BD, NORM_BD), lambda b, t: (0, 0)),
        ]
        rows_of = lambda w, dt: jax.ShapeDtypeStruct((bn, TOK, w), dt)
        out_specs += [_token_tile(NA_WIDTH), _token_tile(NA_WIDTH), _token_tile(NA_WIDTH),
                      _token_tile(POOL_WIDTH), _token_tile(GQA_Q_WIDTH), _token_tile(GQA_KV_WIDTH),
                      pl.BlockSpec((None, GQA_KV_WIDTH, STEP), lambda b, t: (b, 0, t))]
        out_shape += [rows_of(NA_WIDTH, BF16), rows_of(NA_WIDTH, BF16), rows_of(NA_WIDTH, BF16),
                      rows_of(POOL_WIDTH, F32), rows_of(GQA_Q_WIDTH, BF16), rows_of(GQA_KV_WIDTH, BF16),
                      jax.ShapeDtypeStruct((bn, GQA_KV_WIDTH, TOK), BF16)]
    outs = pl.pallas_call(
        functools.partial(_ffn_kernel, layer=l, mod_base=mod_base, split_input=split_input,
                          with_mix=with_mix, with_proj=with_proj, final=final),
        grid=(bn, n_steps),
        in_specs=specs,
        out_specs=out_specs,
        out_shape=out_shape,
        scratch_shapes=[pltpu.VMEM((STEP, D_FF), BF16),
                        pltpu.VMEM((STEP, D_MODEL), BF16),
                        pltpu.VMEM((D_MODEL, 2 * D_FF), BF16),
                        pltpu.VMEM((D_FF, D_MODEL), BF16),
                        pltpu.VMEM((2, FFN_UP_CHUNK, 2 * D_FF), F32),
                        pltpu.VMEM((2, FFN_DN_CHUNK, D_MODEL), F32),
                        pltpu.SemaphoreType.DMA((2,)),
                        pltpu.SemaphoreType.DMA((2,))],
        compiler_params=_params(),
        name="ffn_mix" if with_mix else "ffn_proj" if with_proj else "ffn",
    )(*args)
    return outs if with_proj else outs[0]


def _head_norm_rope(z, gain, cs, sn, ones_bd):
    width = z.shape[-1]
    zz = z * z
    hi = zz.astype(BF16)
    lo = (zz - hi.astype(F32)).astype(BF16)
    span = min(width, ones_bd.shape[0])
    bd = ones_bd[:span, :span]
    ms = jnp.concatenate([_dot(hi[:, c:c + span], bd) + _dot(lo[:, c:c + span], bd)
                          for c in range(0, width, span)], axis=-1)
    zn = z * lax.rsqrt(ms + EPS) * gain
    lane = lax.broadcasted_iota(jnp.int32, zn.shape, 1)
    partner = jnp.where((lane & 16) != 0,
                        pltpu.roll(zn, 16, axis=1), pltpu.roll(zn, width - 16, axis=1))
    reps = width // cs.shape[-1]
    if reps > 1:
        cs = jnp.concatenate([cs] * reps, axis=-1)
        sn = jnp.concatenate([sn] * reps, axis=-1)
    return zn * cs + partner * sn


def _proj_rows(x, rows, mod, g_ref, win_ref, cs_ref, sn_ref, qg_ref, kg_ref, bd_ref,
               qa_ref, ka_ref, va_ref, u_ref, qc_ref, kc_ref, vct_ref):
    bd = bd_ref[...]
    y = _rms(x, g_ref[...])
    a = (y * (1.0 + mod[4:5]) + mod[3:4]).astype(BF16)
    p = _dot(a, win_ref[...])
    qa_ref[rows, :] = (p[:, OFF_A_Q:OFF_A_K] * (QK_SCALE * LOG2_E)).astype(BF16)
    ka_ref[rows, :] = p[:, OFF_A_K:OFF_A_V].astype(BF16)
    va_ref[rows, :] = p[:, OFF_A_V:OFF_B_U].astype(BF16)
    u_ref[rows, :] = p[:, OFF_B_U:OFF_C_Q]
    cs, sn = cs_ref[rows, :], sn_ref[rows, :]
    qc = _head_norm_rope(p[:, OFF_C_Q:OFF_C_K], qg_ref[...], cs, sn, bd)
    qc_ref[rows, :] = (qc * (QK_SCALE * LOG2_E)).astype(BF16)
    kc = _head_norm_rope(p[:, OFF_C_K:OFF_C_V], kg_ref[...], cs, sn, bd)
    kc_ref[rows, :] = kc.astype(BF16)
    vct_ref[:, rows] = p[:, OFF_C_V:D_IN].T.astype(BF16)


def _softmax_pv(scores, values):
    m = scores[0].max(axis=-1, keepdims=True)
    for s in scores[1:]:
        m = jnp.maximum(m, s.max(axis=-1, keepdims=True))
    den = None
    acc = None
    for s, v in zip(scores, values):
        e = jnp.exp2(s - m)
        d = e.sum(axis=-1, keepdims=True)
        o = _dot(e.astype(BF16), v)
        den = d if den is None else den + d
        acc = o if acc is None else acc + o
    return acc / den


def _na_kernel(q_ref, k_ref, v_ref, tb_ref, rm_ref, o_ref, *, with_context):
    head_of_lane = lax.broadcasted_iota(jnp.int32, (1, NA_WIDTH), 1) // HEAD_DIM
    k_ctx = k_ref[SEQ:TOK, :]
    v_ctx = v_ref[SEQ:TOK, :]

    def context(rows):
        q = q_ref[rows, :]
        out = jnp.zeros((TM, NA_WIDTH), F32)
        for h in range(NA_HEADS):
            mine = head_of_lane == h
            qh = jnp.where(mine, q, jnp.zeros_like(q))
            out = jnp.where(mine, _softmax_pv([_dot_nt(qh, k_ctx)], [v_ctx]), out)
        o_ref[rows, :] = out.astype(BF16)

    def body(rows, g):
        if g is None:
            return context(rows)
        q = q_ref[rows, :]
        first_row = jnp.clip(g * NA_QROWS - NA_WIN_H // 2, 0, GRID_H - NA_KROWS)
        start = pl.multiple_of(first_row * GRID_W, GRID_W)
        k_lat = k_ref[pl.ds(start, NA_KEYS), :]
        v_lat = v_ref[pl.ds(start, NA_KEYS), :]
        pattern = jnp.minimum(g, 2) + (g == LAT_TILES - 1).astype(jnp.int32)
        slot0 = first_row - g * NA_QROWS + NA_WIN_H - 1 + NA_DY_PAD

        def scores(h):
            bias = jnp.concatenate([
                jnp.concatenate([
                    tb_ref[h, slot0 + 2 * m - qr]
                    + rm_ref[pattern, qr * NA_KPAIRS + m:qr * NA_KPAIRS + m + 1, :]
                    for m in range(NA_KPAIRS)], axis=-1)
                for qr in range(NA_QROWS)], axis=0)
            qh = jnp.where(head_of_lane == h, q, jnp.zeros_like(q))
            return [_dot_nt(qh, k_lat) + bias, _dot_nt(qh, k_ctx)]

        out = jnp.zeros((TM, NA_WIDTH), F32)
        s_next = scores(0)
        for h in range(NA_HEADS):
            s_cur = s_next
            if h + 1 < NA_HEADS:
                s_next = scores(h + 1)
            out = jnp.where(head_of_lane == h, _softmax_pv(s_cur, [v_lat, v_ctx]), out)
        o_ref[rows, :] = out.astype(BF16)

    _for_each_subtile(body, with_context=with_context)


def _na_call(qa, ka, va, l, tb, rm, *, with_context):
    bn = qa.shape[0]
    whole = pl.BlockSpec((None, TOK, NA_WIDTH), lambda b, t: (b, 0, 0))
    return pl.pallas_call(
        functools.partial(_na_kernel, with_context=with_context),
        grid=(bn, N_STEPS if with_context else LAT_STEPS),
        in_specs=[
            _token_tile(NA_WIDTH), whole, whole,
            _layer_block(l, tb.shape[1:], single_buffer=True),
            pl.BlockSpec(rm.shape, lambda b, t: (0, 0, 0)),
        ],
        out_specs=_token_tile(NA_WIDTH),
        out_shape=jax.ShapeDtypeStruct((bn, TOK if with_context else SEQ, NA_WIDTH), BF16),
        compiler_params=_params(),
        name="na",
    )(qa, ka, va, tb, rm)


def _na_static_tables():
    col = np.arange(GRID_W)
    win_c0 = np.clip(col - NA_WIN_W // 2, 0, GRID_W - NA_WIN_W)
    col_ok = (col[None, :] >= win_c0[:, None]) & (col[None, :] < win_c0[:, None] + NA_WIN_W)
    dx = np.clip(col[None, :] - col[:, None], -(NA_WIN_W - 1), NA_WIN_W - 1) + NA_WIN_W - 1
    sel_x = np.eye(2 * NA_WIN_W - 1, dtype=np.float32)[dx]
    row_mask = np.zeros((len(NA_PATTERN_GROUPS), NA_QROWS, NA_KPAIRS, 2, GRID_W), np.float32)
    for p, grp in enumerate(NA_PATTERN_GROUPS):
        first_row = int(np.clip(grp * NA_QROWS - NA_WIN_H // 2, 0, GRID_H - NA_KROWS))
        q_row = grp * NA_QROWS + np.arange(NA_QROWS)
        q_r0 = np.clip(q_row - NA_WIN_H // 2, 0, GRID_H - NA_WIN_H)
        k_row = first_row + np.arange(NA_KROWS)
        row_ok = (k_row[None, :] >= q_r0[:, None]) & (k_row[None, :] < q_r0[:, None] + NA_WIN_H)
        row_mask[p] = np.where(row_ok, 0.0, NEG_INF).reshape(NA_QROWS, NA_KPAIRS, 2, 1)
    row_mask = row_mask.reshape(len(NA_PATTERN_GROUPS), NA_QROWS * NA_KPAIRS, 2 * GRID_W)
    return sel_x, col_ok, row_mask


_NA_SEL_X, _NA_COL_OK, _NA_ROW_MASK = _na_static_tables()


def _na_bias_table(rpb):
    t = jnp.einsum('lhab,xcb->lhaxc', rpb, _NA_SEL_X, precision=lax.Precision.HIGHEST) * LOG2_E
    t = jnp.where(_NA_COL_OK, t, NEG_INF)
    t = jnp.pad(t, ((0, 0), (0, 0), (NA_DY_PAD, NA_DY_PAD), (0, 0), (0, 0)))
    return jnp.concatenate([t[:, :, :-1], t[:, :, 1:]], axis=-1)


def _shift_rows(a, d):
    n = a.shape[0]
    row = lax.broadcasted_iota(jnp.int32, a.shape, 0)
    rolled = pltpu.roll(a, d % n, axis=0)
    ok = (row >= d) if d > 0 else (row < n + d)
    return jnp.where(ok, rolled, 0.0)


def _pool_lanes(u, windows):
    n = u.shape[0]
    row = lax.broadcasted_iota(jnp.int32, u.shape, 0)
    grp = lax.broadcasted_iota(jnp.int32, u.shape, 1) // POOL_GC
    trail, lead = u, u
    total = jnp.zeros_like(u)
    count = jnp.ones_like(u)
    k = 1
    for i, w in enumerate(windows):
        while k < w // 2:
            trail = trail + _shift_rows(trail, k)
            lead = lead + _shift_rows(lead, -k)
            k *= 2
        win = _shift_rows(trail, 1) + lead
        cnt = (jnp.minimum(row + w // 2, n) - jnp.maximum(row - w // 2, 0)).astype(F32)
        total = jnp.where(grp == i, win, total)
        count = jnp.where(grp == i, cnt, count)
    return total / count - u


def _pool_segment(u, w_bd, scale):
    half = 2 * POOL_GC
    y = jnp.concatenate([_pool_lanes(u[:, c:c + half], POOL_WINDOWS[c // POOL_GC:(c + half) // POOL_GC])
                         for c in range(0, POOL_WIDTH, half)], axis=-1).astype(BF16)
    return (_dot(y, w_bd) * scale).astype(BF16)


def _pool_kernel(u_ref, w_ref, s_ref, o_ref, *, with_context):
    w_bd = w_ref[...]
    scale = s_ref[...]
    o_ref[:SEQ, :] = _pool_segment(u_ref[:SEQ, :], w_bd, scale)
    if with_context:
        o_ref[SEQ:, :] = _pool_segment(u_ref[SEQ:, :], w_bd, scale)


def _pool_call(u, l, w_bd, scale, *, with_context):
    bn = u.shape[0]
    rows = TOK if with_context else SEQ
    return pl.pallas_call(
        functools.partial(_pool_kernel, with_context=with_context),
        grid=(bn,),
        in_specs=[pl.BlockSpec((None, TOK, POOL_WIDTH), lambda b: (b, 0, 0)),
                  _layer_block(l, (POOL_WIDTH, POOL_WIDTH)), _layer_block(l, (1, POOL_WIDTH))],
        out_specs=pl.BlockSpec((None, rows, POOL_WIDTH), lambda b: (b, 0, 0)),
        out_shape=jax.ShapeDtypeStruct((bn, rows, POOL_WIDTH), BF16),
        compiler_params=_params(1),
        name="pool",
    )(u, w_bd, scale)


def _gqa_tile(q, k_ref, vt_ref, key_chunks, bounded):
    half_of_lane = lax.broadcasted_iota(jnp.int32, (1, GQA_KV_WIDTH), 1) // HEAD_DIM
    stages = [(j, lo, hi) for j in range(GQA_GROUP) for lo, hi in key_chunks]

    def scores_t(stage):
        j, lo, hi = stage
        qb = q[:, j * GQA_KV_WIDTH:(j + 1) * GQA_KV_WIDTH]
        q2 = jnp.concatenate([jnp.where(half_of_lane == kv, qb, jnp.zeros_like(qb))
                              for kv in range(GQA_KV_HEADS)], axis=0)
        return _dot_nt(k_ref[lo:hi, :], q2)

    partial = {j: [] for j in range(GQA_GROUP)}
    s_next = scores_t(stages[0])
    for i, (j, lo, hi) in enumerate(stages):
        s_t = s_next
        if i + 1 < len(stages):
            s_next = scores_t(stages[i + 1])
        if bounded:
            m = None
            e = jnp.exp2(s_t)
        else:
            m = s_t.max(axis=0, keepdims=True)
            e = jnp.exp2(s_t - m)
        den = e.sum(axis=0, keepdims=True)
        e = e.astype(BF16)
        o_t = [_dot(vt_ref[kv * HEAD_DIM:(kv + 1) * HEAD_DIM, lo:hi], e[:, kv * TM:(kv + 1) * TM])
               for kv in range(GQA_KV_HEADS)]
        partial[j].append((m, den, o_t))

    blocks = []
    for j in range(GQA_GROUP):
        if bounded:
            weights = [1.0] * len(partial[j])
        else:
            m_all = functools.reduce(jnp.maximum, [m for m, _, _ in partial[j]])
            weights = [jnp.exp2(m - m_all) for m, _, _ in partial[j]]
        den = sum(w * d for w, (_, d, _) in zip(weights, partial[j]))
        o_t = jnp.concatenate([
            sum((w if bounded else w[:, kv * TM:(kv + 1) * TM]) * o[kv]
                for w, (_, _, o) in zip(weights, partial[j]))
            / den[:, kv * TM:(kv + 1) * TM]
            for kv in range(GQA_KV_HEADS)], axis=0)
        blocks.append(o_t.T)
    return jnp.concatenate(blocks, axis=-1).astype(BF16)


def _gqa_kernel(q_ref, k_ref, vt_ref, o_ref, *, bounded, with_context):
    def body(rows, lat_tile):
        if lat_tile is None:
            chunks = [(SEQ, TOK)]
        else:
            chunks = [(lo, lo + GQA_KEY_CHUNK) for lo in range(0, TOK, GQA_KEY_CHUNK)]
        o_ref[rows, :] = _gqa_tile(q_ref[rows, :], k_ref, vt_ref, chunks, bounded)

    _for_each_subtile(body, with_context=with_context)


def _gqa_logit_bound(q_gain, k_gain):
    return (HEAD_DIM * jnp.max(jnp.abs(q_gain)) * jnp.max(jnp.abs(k_gain))
            * (QK_SCALE * LOG2_E) * (1.0 + 2.0 ** -7) ** 2)


def _gqa_call(qc, kc, vc_t, *, bounded, with_context):
    bn = qc.shape[0]
    return pl.pallas_call(
        functools.partial(_gqa_kernel, bounded=bounded, with_context=with_context),
        grid=(bn, N_STEPS if with_context else LAT_STEPS),
        in_specs=[_token_tile(GQA_Q_WIDTH),
                  pl.BlockSpec((None, TOK, GQA_KV_WIDTH), lambda b, t: (b, 0, 0)),
                  pl.BlockSpec((None, GQA_KV_WIDTH, TOK), lambda b, t: (b, 0, 0))],
        out_specs=_token_tile(GQA_Q_WIDTH),
        out_shape=jax.ShapeDtypeStruct((bn, TOK if with_context else SEQ, GQA_Q_WIDTH), BF16),
        compiler_params=_params(),
        name="gqa_bounded" if bounded else "gqa",
    )(qc, kc, vc_t)


def _rope_tables():
    pos = np.arange(SEQ)
    inv_freq = ROPE_THETA ** (-np.arange(0, HEAD_DIM // 2, 2, dtype=np.float32) / (HEAD_DIM // 2))
    ang_row = (pos // GRID_W).astype(np.float32)[:, None] * inv_freq[None, :].astype(np.float32)
    ang_col = (pos % GRID_W).astype(np.float32)[:, None] * inv_freq[None, :].astype(np.float32)
    ang = np.concatenate([ang_row, ang_row, ang_col, ang_col], axis=-1).astype(np.float32)
    sign = np.tile(np.repeat(np.array([-1.0, 1.0], np.float32), HEAD_DIM // 4), 2)
    cs = np.concatenate([np.cos(ang), np.ones((CTX_LEN, HEAD_DIM), np.float32)], axis=0)
    sn = np.concatenate([np.sin(ang) * sign[None, :], np.zeros((CTX_LEN, HEAD_DIM), np.float32)], axis=0)
    return np.tile(cs, (1, 2)).astype(np.float32), np.tile(sn, (1, 2)).astype(np.float32)


_ROPE_CS, _ROPE_SN = _rope_tables()


def _group_major(a, axis):
    shape = a.shape
    a = a.reshape(shape[:axis] + (GQA_KV_HEADS, GQA_GROUP, HEAD_DIM) + shape[axis + 1:])
    return jnp.swapaxes(a, axis, axis + 1).reshape(shape)


def kernel(x, c, ctx, c_ctx, w_ada, b_ada, norm_g, ffn1_up, ffn1_down, ffn2_up, ffn2_down,
           w_in, w_out, na_rpb, pool_w, pool_scale, q_norm_g, k_norm_g, final_g):
    bn = x.shape[0]
    depth = w_ada.shape[0]
    assert x.shape == (bn, SEQ, D_MODEL) and ctx.shape == (bn, CTX_LEN, D_MODEL)
    assert bn < ADA_ROWS

    cc = jnp.zeros((ADA_ROWS, D_MODEL), F32).at[:bn].set(c).at[bn].set(c_ctx)
    mods = _ada_call(cc, w_ada, b_ada).reshape(depth, ADA_ROWS, N_MOD, D_MODEL)
    gains = norm_g.reshape(depth, 3, 1, D_MODEL)

    rope_cs, rope_sn = jnp.asarray(_ROPE_CS), jnp.asarray(_ROPE_SN)
    ones_bd = jnp.asarray(
        np.kron(np.eye(NORM_BD // HEAD_DIM, dtype=np.float32),
                np.full((HEAD_DIM, HEAD_DIM), 1.0 / HEAD_DIM, np.float32))).astype(BF16)

    w_in_p = jnp.concatenate([w_in[:, :, :OFF_C_Q], _group_major(w_in[:, :, OFF_C_Q:OFF_C_K], 2),
                              w_in[:, :, OFF_C_K:]], axis=-1).astype(BF16)
    w_out_p = jnp.concatenate([w_out[:, :NA_WIDTH + POOL_WIDTH],
                               _group_major(w_out[:, NA_WIDTH + POOL_WIDTH:], 1)], axis=1).astype(BF16)
    pool_bd = jnp.einsum('gh,lgcd->lgchd', jnp.eye(len(POOL_WINDOWS), dtype=F32), pool_w)
    pool_bd = pool_bd.reshape(depth, POOL_WIDTH, POOL_WIDTH).astype(BF16)
    pool_sc = pool_scale.reshape(depth, 1, POOL_WIDTH)
    qg = jnp.tile(q_norm_g, (1, GQA_Q_HEADS)).reshape(depth, 1, GQA_Q_WIDTH)
    kg = jnp.tile(k_norm_g, (1, GQA_KV_HEADS)).reshape(depth, 1, GQA_KV_WIDTH)
    na_tb = _na_bias_table(na_rpb)
    na_rm = jnp.asarray(_NA_ROW_MASK)

    h = (x, ctx)
    for l in range(depth):
        last = l == depth - 1
        h, qa, ka, va, u, qc, kc, vc_t = _ffn_call(
            h, l, mods, gains, 0, ffn1_up, ffn1_down, mod_base=0,
            proj=(w_in_p, rope_cs, rope_sn, qg, kg, ones_bd))
        oa = _na_call(qa, ka, va, l, na_tb, na_rm, with_context=not last)
        ob = _pool_call(u, l, pool_bd, pool_sc, with_context=not last)
        oc = lax.cond(_gqa_logit_bound(q_norm_g[l], k_norm_g[l]) <= GQA_SAFE_LOGIT,
                      functools.partial(_gqa_call, qc, kc, vc_t, bounded=True, with_context=not last),
                      functools.partial(_gqa_call, qc, kc, vc_t, bounded=False, with_context=not last))
        h = _ffn_call(h, l, mods, gains, 2, ffn2_up, ffn2_down, mod_base=6,
                      mix=(oa, ob, oc, w_out_p), final_g=final_g if last else None)
    return h
```
